```python
import math
import jax, jax.numpy as jnp
from jax import lax
import numpy as np

D_MODEL = 1024
BATCH = 4
SEQ = 4096
DEPTH = 2

N_MEM = 256
NORM_EPS = 1e-6
ROPE_THETA = 500000.0
HEAD_DIM = 64
ROT_DIM = HEAD_DIM // 4
BLOCK = 128
LRU_WIDTH = D_MODEL
LRU_HEADS = 4
LRU_HEAD_DIM = LRU_WIDTH // LRU_HEADS
CONV_WIDTH = 4
LRU_C = 8.0
B_HEADS = 8
B_WIDTH = B_HEADS * HEAD_DIM
DILATED_PATTERN = ((128, 1), (512, 4), (2048, 16))
C_HEADS = 16
C_KV_HEADS = 2
C_WINDOW = 128
C_Q_WIDTH = C_HEADS * HEAD_DIM
C_KV_WIDTH = C_KV_HEADS * HEAD_DIM
C_QKV = C_Q_WIDTH + 2 * C_KV_WIDTH
XA_HEADS = 4
XA_HEAD_DIM = 128
XA_WIDTH = XA_HEADS * XA_HEAD_DIM
D_FF = -(-8 * D_MODEL // (3 * 256)) * 256
AB_IN = 2 * LRU_WIDTH + 3 * B_WIDTH
AB_SPLITS = (LRU_WIDTH, 2 * LRU_WIDTH, 2 * LRU_WIDTH + B_WIDTH, 2 * LRU_WIDTH + 2 * B_WIDTH)
AB_OUT = LRU_WIDTH + B_WIDTH
N_EVEN = (DEPTH + 1) // 2
N_ODD = DEPTH // 2
NEG = -1e30

kernel_name = "hybrid_rglru_dilated_swa_sink_block"


def rms_norm(x, g):
    xf = x.astype(jnp.float32)
    xf = xf * lax.rsqrt(jnp.mean(xf * xf, axis=-1, keepdims=True) + NORM_EPS)
    return (xf * g.astype(jnp.float32)).astype(x.dtype)


def rope_tables(L):
    inv = ROPE_THETA ** (-jnp.arange(0, ROT_DIM, 2, dtype=jnp.float32) / ROT_DIM)
    ang = jnp.arange(L, dtype=jnp.float32)[:, None] * inv[None, :]
    return jnp.cos(ang), jnp.sin(ang)


def partial_rope(x, cos, sin):
    half = ROT_DIM // 2
    cos = cos.astype(x.dtype)
    sin = sin.astype(x.dtype)
    x1 = x[..., :half]
    x2 = x[..., half:ROT_DIM]
    return jnp.concatenate([x1 * cos - x2 * sin, x2 * cos + x1 * sin, x[..., ROT_DIM:]], axis=-1)


def banded_attention(q, k, v, max_dist, sinks=None):
    Bsz, Hq, L, hd = q.shape
    Hkv = k.shape[1]
    G = Hq // Hkv
    nb = -(-L // BLOCK)
    Lp = nb * BLOCK
    pad = ((0, 0), (0, 0), (0, Lp - L), (0, 0))
    qb = jnp.pad(q, pad).reshape(Bsz, Hkv, G, nb, BLOCK, hd)
    kb = jnp.pad(k, pad).reshape(Bsz, Hkv, nb, BLOCK, hd)
    vb = jnp.pad(v, pad).reshape(Bsz, Hkv, nb, BLOCK, hd)

    def band(t):
        prev = jnp.pad(t, ((0, 0), (0, 0), (1, 0), (0, 0), (0, 0)))[:, :, :nb]
        return jnp.concatenate([prev, t], axis=3)

    kband, vband = band(kb), band(vb)
    s = jnp.einsum('bhgnqd,bhnkd->bhgnqk', qb, kband).astype(jnp.float32) * (hd ** -0.5)
    blk = jnp.arange(nb)[:, None, None]
    qpos = blk * BLOCK + jnp.arange(BLOCK)[None, :, None]
    kpos = (blk - 1) * BLOCK + jnp.arange(2 * BLOCK)[None, None, :]
    dist = qpos - kpos
    mask = (dist >= 0) & (dist <= max_dist) & (kpos >= 0)
    s = jnp.where(mask, s, NEG)
    m = jnp.max(s, axis=-1, keepdims=True)
    if sinks is not None:
        sk = sinks.astype(jnp.float32).reshape(1, Hkv, G, 1, 1, 1)
        m = jnp.maximum(m, sk)
        e = jnp.exp(s - m)
        den = jnp.sum(e, axis=-1, keepdims=True) + jnp.exp(sk - m)
    else:
        e = jnp.exp(s - m)
        den = jnp.sum(e, axis=-1, keepdims=True)
    p = e / den
    lse = (m + jnp.log(den))[..., 0]
    o = jnp.einsum('bhgnqk,bhnkd->bhgnqd', p.astype(v.dtype), vband)
    o = o.reshape(Bsz, Hq, Lp, hd)[:, :, :L]
    lse = lse.reshape(Bsz, Hq, Lp)[:, :, :L]
    return o, lse


def dilated_attention(q, k, v):
    Bsz, H, L, hd = q.shape
    outs, lses = [], []
    for window, d in DILATED_PATTERN:
        Ld = L // d

        def to_strided(t):
            return t.reshape(Bsz, H, Ld, d, hd).transpose(0, 1, 3, 2, 4).reshape(Bsz, H * d, Ld, hd)

        o, lse = banded_attention(to_strided(q), to_strided(k), to_strided(v), window // d)
        outs.append(o.reshape(Bsz, H, d, Ld, hd).transpose(0, 1, 3, 2, 4).reshape(Bsz, H, L, hd))
        lses.append(lse.reshape(Bsz, H, d, Ld).transpose(0, 1, 3, 2).reshape(Bsz, H, L))
    w = jax.nn.softmax(jnp.stack(lses, axis=0), axis=0)
    o = jnp.sum(w[..., None] * jnp.stack(outs, axis=0).astype(jnp.float32), axis=0)
    return o.astype(q.dtype)


def causal_depthwise_conv(x, w, b):
    L = x.shape[1]
    xp = jnp.pad(x, ((0, 0), (CONV_WIDTH - 1, 0), (0, 0)))
    y = b + xp[:, 0:L] * w[0]
    for tap in range(1, CONV_WIDTH):
        y = y + xp[:, tap:tap + L] * w[tap]
    return y


def block_diag_linear(x, w, b):
    Bsz, L, _ = x.shape
    xh = x.reshape(Bsz, L, LRU_HEADS, LRU_HEAD_DIM)
    y = jnp.einsum('blhi,hij->blhj', xh, w) + b
    return y.reshape(Bsz, L, LRU_WIDTH)


def rg_lru(x, wa, ba, wx, bx, lam):
    r = jax.nn.sigmoid(block_diag_linear(x, wa, ba).astype(jnp.float32))
    i = jax.nn.sigmoid(block_diag_linear(x, wx, bx).astype(jnp.float32))
    log_a = -LRU_C * r * jax.nn.softplus(-lam.astype(jnp.float32))
    a = jnp.exp(log_a)
    u = jnp.sqrt(-jnp.expm1(2.0 * log_a)) * (i * x.astype(jnp.float32))

    def combine(left, right):
        a1, b1 = left
        a2, b2 = right
        return a1 * a2, a2 * b1 + b2

    _, h = lax.associative_scan(combine, (a, u), axis=1)
    return h.astype(x.dtype)


def lru_dilated_mixer(h, cos, sin, w_in, conv_w, conv_b, wa, ba, wx, bx, lam, w_out):
    Bsz, L, _ = h.shape
    proj = h @ w_in
    x_br, y_br, q, k, v = jnp.split(proj, list(AB_SPLITS), axis=-1)
    rec = rg_lru(causal_depthwise_conv(x_br, conv_w, conv_b), wa, ba, wx, bx, lam) * jax.nn.gelu(y_br)

    def heads(t):
        return t.reshape(Bsz, L, B_HEADS, HEAD_DIM).transpose(0, 2, 1, 3)

    q = partial_rope(heads(q), cos, sin)
    k = partial_rope(heads(k), cos, sin)
    att = dilated_attention(q, k, heads(v)).transpose(0, 2, 1, 3).reshape(Bsz, L, B_WIDTH)
    return jnp.concatenate([rec, att], axis=-1) @ w_out


def swa_sink_mixer(h, cos, sin, w_qkv, b_qkv, sinks, w_out, b_out):
    Bsz, L, _ = h.shape
    proj = h @ w_qkv + b_qkv
    q, k, v = jnp.split(proj, [C_Q_WIDTH, C_Q_WIDTH + C_KV_WIDTH], axis=-1)
    q = partial_rope(q.reshape(Bsz, L, C_HEADS, HEAD_DIM).transpose(0, 2, 1, 3), cos, sin)
    k = partial_rope(k.reshape(Bsz, L, C_KV_HEADS, HEAD_DIM).transpose(0, 2, 1, 3), cos, sin)
    v = v.reshape(Bsz, L, C_KV_HEADS, HEAD_DIM).transpose(0, 2, 1, 3)
    o, _ = banded_attention(q, k, v, C_WINDOW - 1, sinks)
    return o.transpose(0, 2, 1, 3).reshape(Bsz, L, C_Q_WIDTH) @ w_out + b_out


def memory_cross_attention(h, mem_n, wq, wkv, wo):
    Bsz, L, _ = h.shape
    M = mem_n.shape[1]
    q = (h @ wq).reshape(Bsz, L, XA_HEADS, XA_HEAD_DIM)
    k, v = jnp.split(mem_n @ wkv, 2, axis=-1)
    k = k.reshape(Bsz, M, XA_HEADS, XA_HEAD_DIM)
    v = v.reshape(Bsz, M, XA_HEADS, XA_HEAD_DIM)
    s = jnp.einsum('blhd,bmhd->bhlm', q, k).astype(jnp.float32) * (XA_HEAD_DIM ** -0.5)
    p = jax.nn.softmax(s, axis=-1)
    o = jnp.einsum('bhlm,bmhd->blhd', p.astype(v.dtype), v).reshape(Bsz, L, XA_WIDTH)
    return o @ wo


def swiglu(h, w_gate_up, w_down):
    gate, up = jnp.split(h @ w_gate_up, 2, axis=-1)
    return (jax.nn.silu(gate) * up) @ w_down


def setup_inputs(seed: int = 0) -> dict:
    key = jax.random.key(seed)
    ks = jax.random.split(key, 32)
    f32 = jnp.float32

    def nrm(k, shape, fan_in):
        return jax.random.normal(k, shape, f32) * (fan_in ** -0.5)

    def gain(k, shape):
        return 1.0 + 0.02 * jax.random.normal(k, shape, f32)

    def small(k, shape):
        return 0.01 * jax.random.normal(k, shape, f32)

    a_c = jax.random.uniform(ks[8], (N_EVEN, LRU_WIDTH), f32, 0.9, 0.999)
    a0 = a_c ** (1.0 / LRU_C)
    lru_lambda = jnp.log(a0) - jnp.log1p(-a0)

    return {
        "x": jax.random.normal(ks[0], (BATCH, SEQ, D_MODEL), f32),
        "mem": jax.random.normal(ks[1], (BATCH, N_MEM, D_MODEL), f32),
        "mix_norm": gain(ks[2], (DEPTH, D_MODEL)),
        "ab_w_in": nrm(ks[3], (N_EVEN, D_MODEL, AB_IN), D_MODEL),
        "lru_conv_w": nrm(ks[4], (N_EVEN, CONV_WIDTH, LRU_WIDTH), CONV_WIDTH),
        "lru_conv_b": small(ks[5], (N_EVEN, LRU_WIDTH)),
        "lru_wa": nrm(ks[6], (N_EVEN, LRU_HEADS, LRU_HEAD_DIM, LRU_HEAD_DIM), LRU_HEAD_DIM),
        "lru_ba": small(ks[7], (N_EVEN, LRU_HEADS, LRU_HEAD_DIM)),
        "lru_wx": nrm(ks[9], (N_EVEN, LRU_HEADS, LRU_HEAD_DIM, LRU_HEAD_DIM), LRU_HEAD_DIM),
        "lru_bx": small(ks[10], (N_EVEN, LRU_HEADS, LRU_HEAD_DIM)),
        "lru_lambda": lru_lambda,
        "ab_w_out": nrm(ks[11], (N_EVEN, AB_OUT, D_MODEL), AB_OUT),
        "c_w_qkv": nrm(ks[12], (N_ODD, D_MODEL, C_QKV), D_MODEL),
        "c_b_qkv": small(ks[13], (N_ODD, C_QKV)),
        "c_sinks": 0.5 * jax.random.normal(ks[14], (N_ODD, C_HEADS), f32),
        "c_w_out": nrm(ks[15], (N_ODD, C_Q_WIDTH, D_MODEL), C_Q_WIDTH),
        "c_b_out": small(ks[16], (N_ODD, D_MODEL)),
        "xa_norm": gain(ks[17], (DEPTH, D_MODEL)),
        "xa_mem_norm": gain(ks[18], (DEPTH, D_MODEL)),
        "xa_wq": nrm(ks[19], (DEPTH, D_MODEL, XA_WIDTH), D_MODEL),
        "xa_wkv": nrm(ks[20], (DEPTH, D_MODEL, 2 * XA_WIDTH), D_MODEL),
        "xa_wo": nrm(ks[21], (DEPTH, XA_WIDTH, D_MODEL), XA_WIDTH),
        "ffn_norm": gain(ks[22], (DEPTH, D_MODEL)),
        "ffn_w_gate_up": nrm(ks[23], (DEPTH, D_MODEL, 2 * D_FF), D_MODEL),
        "ffn_w_down": nrm(ks[24], (DEPTH, D_FF, D_MODEL), D_FF),
        "final_norm": gain(ks[25], (D_MODEL,)),
    }


def reference(x, mem, mix_norm, ab_w_in, lru_conv_w, lru_conv_b, lru_wa, lru_ba, lru_wx, lru_bx,
              lru_lambda, ab_w_out, c_w_qkv, c_b_qkv, c_sinks, c_w_out, c_b_out, xa_norm,
              xa_mem_norm, xa_wq, xa_wkv, xa_wo, ffn_norm, ffn_w_gate_up, ffn_w_down, final_norm):
    cos, sin = rope_tables(x.shape[1])
    h = x
    for layer in range(DEPTH):
        j = layer // 2
        hn = rms_norm(h, mix_norm[layer])
        if layer % 2 == 0:
            h = h + lru_dilated_mixer(hn, cos, sin, ab_w_in[j], lru_conv_w[j], lru_conv_b[j],
                                      lru_wa[j], lru_ba[j], lru_wx[j], lru_bx[j], lru_lambda[j],
                                      ab_w_out[j])
        else:
            h = h + swa_sink_mixer(hn, cos, sin, c_w_qkv[j], c_b_qkv[j], c_sinks[j],
                                   c_w_out[j], c_b_out[j])
        h = h + memory_cross_attention(rms_norm(h, xa_norm[layer]), rms_norm(mem, xa_mem_norm[layer]),
                                       xa_wq[layer], xa_wkv[layer], xa_wo[layer])
        h = h + swiglu(rms_norm(h, ffn_norm[layer]), ffn_w_gate_up[layer], ffn_w_down[layer])
    return rms_norm(h, final_norm)
```

```python
import functools
import math

import jax
import jax.numpy as jnp
from jax import lax
from jax.experimental import pallas as pl
from jax.experimental.pallas import tpu as pltpu

F32 = jnp.float32
BF16 = jnp.bfloat16

D_MODEL = 1024
N_MEM = 256
NORM_EPS = 1e-6
ROPE_THETA = 500000.0
HEAD_DIM = 64
ROT_DIM = HEAD_DIM // 4
ROT_HALF = ROT_DIM // 2
LANES = 128
SUBLANES = 8
BLOCK = 128
LRU_WIDTH = D_MODEL
LRU_HEADS = 4
LRU_HEAD_DIM = LRU_WIDTH // LRU_HEADS
CONV_WIDTH = 4
LRU_C = 8.0
B_HEADS = 8
B_WIDTH = B_HEADS * HEAD_DIM
DILATED_PATTERN = ((128, 1), (512, 4), (2048, 16))
C_HEADS = 16
C_KV_HEADS = 2
C_WINDOW = 128
C_Q_WIDTH = C_HEADS * HEAD_DIM
C_KV_WIDTH = C_KV_HEADS * HEAD_DIM
XA_HEADS = 4
XA_HEAD_DIM = 128
XA_WIDTH = XA_HEADS * XA_HEAD_DIM
NEG = -1e30
VMEM_LIMIT = 56 * 1024 * 1024

TOK_TILE = 512
FF_CHUNK = 1408


def _dot(a, b):
    return jnp.dot(a, b, preferred_element_type=F32)


def _dot_nt(a, b):
    return lax.dot_general(a, b, (((1,), (1,)), ((), ())), preferred_element_type=F32)


def _rms(x, g):
    return x * lax.rsqrt(jnp.mean(x * x, axis=-1, keepdims=True) + NORM_EPS) * g


def _params(*sem):
    return pltpu.CompilerParams(dimension_semantics=sem, vmem_limit_bytes=VMEM_LIMIT)


def _const_spec(shape):
    nd = len(shape)
    return pl.BlockSpec(shape, lambda *_: (0,) * nd, pipeline_mode=pl.Buffered(1))


def _rope(x, c, s1, s2):
    reps = x.shape[1] // LANES
    c, s1, s2 = (jnp.concatenate([t] * reps, axis=1) for t in (c, s1, s2))
    width = x.shape[1]
    return x * c + pltpu.roll(x, ROT_HALF, 1) * s1 + pltpu.roll(x, width - ROT_HALF, 1) * s2


def _rope_tables(seq):
    inv = ROPE_THETA ** (-jnp.arange(0, ROT_DIM, 2, dtype=F32) / ROT_DIM)
    ang = jnp.arange(seq, dtype=F32)[:, None] * inv[None, :]
    cos, sin = jnp.cos(ang), jnp.sin(ang)
    zeros = jnp.zeros((seq, HEAD_DIM - ROT_DIM), F32)
    zh = jnp.zeros((seq, ROT_HALF), F32)
    c = jnp.concatenate([cos, cos, zeros + 1.0], axis=1)
    s1 = jnp.concatenate([zh, sin, zeros], axis=1)
    s2 = jnp.concatenate([-sin, zh, zeros], axis=1)
    return tuple(jnp.concatenate([t, t], axis=1) for t in (c, s1, s2))


def _lru_proj_kernel(h_ref, g_ref, w_ref, cw_ref, cb_ref, wa_ref, ba_ref, wx_ref, bx_ref,
                     lam_ref, rc_ref, rs1_ref, rs2_ref,
                     rec_ref, q_ref, k_ref, v_ref,
                     xs_ref, as_ref, us_ref, hs_ref, hprev_ref, *, tl):
    pad = SUBLANES

    @pl.when(pl.program_id(1) == 0)
    def _():
        xs_ref[0:pad, :] = jnp.zeros((pad, LRU_WIDTH), F32)
        hprev_ref[...] = jnp.zeros_like(hprev_ref)

    as_ref[0:pad, :] = jnp.ones((pad, LRU_WIDTH), F32)
    us_ref[0:pad, :] = jnp.zeros((pad, LRU_WIDTH), F32)

    hn = _rms(h_ref[...], g_ref[...]).astype(BF16)

    x = _dot(hn, w_ref[:, 0:LRU_WIDTH])
    xs_ref[pad:pad + tl, :] = x
    xc = cb_ref[...] + xs_ref[pad - 3:pad - 3 + tl, :] * cw_ref[0:1, :]
    xc = xc + xs_ref[pad - 2:pad - 2 + tl, :] * cw_ref[1:2, :]
    xc = xc + xs_ref[pad - 1:pad - 1 + tl, :] * cw_ref[2:3, :]
    xc = xc + x * cw_ref[3:4, :]
    xs_ref[0:pad, :] = xs_ref[tl:tl + pad, :]

    xcb = xc.astype(BF16)
    r_parts, i_parts = [], []
    for hd in range(LRU_HEADS):
        sl = slice(hd * LRU_HEAD_DIM, (hd + 1) * LRU_HEAD_DIM)
        r_parts.append(_dot(xcb[:, sl], wa_ref[hd]))
        i_parts.append(_dot(xcb[:, sl], wx_ref[hd]))
    r = jax.nn.sigmoid(jnp.concatenate(r_parts, axis=1) + ba_ref[...])
    ig = jax.nn.sigmoid(jnp.concatenate(i_parts, axis=1) + bx_ref[...])

    nlam = -lam_ref[...]
    softplus = jnp.maximum(nlam, 0.0) + jnp.log1p(jnp.exp(-jnp.abs(nlam)))
    log_a = (-LRU_C) * r * softplus
    a = jnp.exp(log_a)
    u = jnp.sqrt(jnp.tanh(-log_a) * (1.0 + a * a)) * (ig * xc)

    as_ref[pad:pad + tl, :] = a
    us_ref[pad:pad + tl, :] = u
    for s in (1, 2, 4):
        a_cur = as_ref[pad:pad + tl, :]
        u_cur = us_ref[pad:pad + tl, :]
        a_sh = as_ref[pad - s:pad - s + tl, :]
        u_sh = us_ref[pad - s:pad - s + tl, :]
        us_ref[pad:pad + tl, :] = a_cur * u_sh + u_cur
        as_ref[pad:pad + tl, :] = a_cur * a_sh

    def chain(g, hcar):
        r0 = pl.multiple_of(pad + g * SUBLANES, SUBLANES)
        hcar = as_ref[pl.ds(r0, SUBLANES), :] * hcar + us_ref[pl.ds(r0, SUBLANES), :]
        hs_ref[pl.ds(pl.multiple_of(g * SUBLANES, SUBLANES), SUBLANES), :] = hcar
        return hcar

    h0 = jnp.broadcast_to(hprev_ref[SUBLANES - 1:SUBLANES, :], (SUBLANES, LRU_WIDTH))
    hprev_ref[...] = lax.fori_loop(0, tl // SUBLANES, chain, h0, unroll=8)

    y = _dot(hn, w_ref[:, LRU_WIDTH:2 * LRU_WIDTH])
    gelu = 0.5 * y * (1.0 + jnp.tanh(math.sqrt(2.0 / math.pi) * (y + 0.044715 * (y * y * y))))
    rec_ref[...] = (hs_ref[...] * gelu).astype(BF16)

    c, s1, s2 = rc_ref[...], rs1_ref[...], rs2_ref[...]
    o = 2 * LRU_WIDTH
    q = _dot(hn, w_ref[:, o:o + B_WIDTH])
    q_ref[...] = (_rope(q, c, s1, s2) * (HEAD_DIM ** -0.5)).astype(BF16)
    k = _dot(hn, w_ref[:, o + B_WIDTH:o + 2 * B_WIDTH])
    k_ref[...] = _rope(k, c, s1, s2).astype(BF16)
    v_ref[...] = _dot(hn, w_ref[:, o + 2 * B_WIDTH:o + 3 * B_WIDTH]).astype(BF16)


def _lru_proj(h, g, w_in, conv_w, conv_b, wa, ba, wx, bx, lam, tables, tl=TOK_TILE):
    bsz, seq, d = h.shape
    row = lambda a: a.reshape(1, -1)
    tok = lambda w: pl.BlockSpec((None, tl, w), lambda b, i: (b, i, 0))
    tab = pl.BlockSpec((tl, LANES), lambda b, i: (i, 0))
    out_shape = (jax.ShapeDtypeStruct((bsz, seq, LRU_WIDTH), BF16),) + tuple(
        jax.ShapeDtypeStruct((bsz, seq, B_WIDTH), BF16) for _ in range(3))
    return pl.pallas_call(
        functools.partial(_lru_proj_kernel, tl=tl),
        grid=(bsz, seq // tl),
        in_specs=[tok(d), _const_spec((1, d)), _const_spec(w_in.shape),
                  _const_spec(conv_w.shape), _const_spec((1, LRU_WIDTH)),
                  _const_spec(wa.shape), _const_spec((1, LRU_WIDTH)),
                  _const_spec(wx.shape), _const_spec((1, LRU_WIDTH)),
                  _const_spec((1, LRU_WIDTH)), tab, tab, tab],
        out_specs=(tok(LRU_WIDTH), tok(B_WIDTH), tok(B_WIDTH), tok(B_WIDTH)),
        out_shape=out_shape,
        scratch_shapes=[pltpu.VMEM((tl + SUBLANES, LRU_WIDTH), F32),
                        pltpu.VMEM((tl + SUBLANES, LRU_WIDTH), F32),
                        pltpu.VMEM((tl + SUBLANES, LRU_WIDTH), F32),
                        pltpu.VMEM((tl, LRU_WIDTH), F32),
                        pltpu.VMEM((SUBLANES, LRU_WIDTH), F32)],
        compiler_params=_params("arbitrary", "arbitrary"),
        name="lru_proj",
    )(h, row(g), w_in, conv_w, row(conv_b), wa, row(ba), wx, row(bx), row(lam), *tables)


def _band_block(q_ref, k_ref, v_ref, i, max_dist, sink_vals):
    q0 = pl.multiple_of(i * BLOCK, BLOCK)
    ks = pl.multiple_of(jnp.maximum(i - 1, 0) * BLOCK, BLOCK)
    qb = q_ref[pl.ds(q0, BLOCK), :]
    kb = k_ref[pl.ds(ks, 2 * BLOCK), :]
    vb = v_ref[pl.ds(ks, 2 * BLOCK), :]
    dist = (q0 - ks) + (lax.broadcasted_iota(jnp.int32, (BLOCK, 2 * BLOCK), 0)
                        - lax.broadcasted_iota(jnp.int32, (BLOCK, 2 * BLOCK), 1))
    valid = jnp.logical_and(dist >= 0, dist <= max_dist)
    lane = lax.broadcasted_iota(jnp.int32, (2 * BLOCK, LANES), 1)
    zero = jnp.zeros((), kb.dtype)
    o = None
    stats = []
    for hh in range(2):
        mine = (lane < HEAD_DIM) if hh == 0 else (lane >= HEAD_DIM)
        s = _dot_nt(qb, jnp.where(mine, kb, zero))
        s = jnp.where(valid, s, NEG)
        m = jnp.max(s, axis=-1, keepdims=True)
        if sink_vals is not None:
            m = jnp.maximum(m, sink_vals[hh])
        e = jnp.exp(s - m)
        den = jnp.sum(e, axis=-1, keepdims=True)
        if sink_vals is not None:
            den = den + jnp.exp(sink_vals[hh] - m)
        oh = _dot(e.astype(vb.dtype), jnp.where(mine, vb, zero)) * (1.0 / den)
        o = oh if o is None else o + oh
        stats.append((m, den))
    return q0, o, stats


def _dilated_kernel(q_ref, k_ref, v_ref, o_ref, l_ref, *, nblk, max_dist):
    lane = lax.broadcasted_iota(jnp.int32, (BLOCK, LANES), 1)

    def body(i, carry):
        q0, o, stats = _band_block(q_ref, k_ref, v_ref, i, max_dist, None)
        lse = [m + jnp.log(den) for m, den in stats]
        o_ref[pl.ds(q0, BLOCK), :] = o
        l_ref[pl.ds(q0, BLOCK), :] = jnp.where(lane < HEAD_DIM, lse[0], lse[1])
        return carry

    lax.fori_loop(0, nblk, body, 0)


def _dilated_branch(q, k, v, window, d):
    bsz, seq, w = q.shape
    ld = seq // d
    cols = d * (w // LANES)
    view = lambda t: t.reshape(bsz, ld, d * w)
    spec = pl.BlockSpec((None, ld, LANES), lambda b, c: (b, 0, c))
    o, lse = pl.pallas_call(
        functools.partial(_dilated_kernel, nblk=ld // BLOCK, max_dist=window // d),
        grid=(bsz, cols),
        in_specs=[spec, spec, spec],
        out_specs=(spec, spec),
        out_shape=(jax.ShapeDtypeStruct((bsz, ld, d * w), F32),) * 2,
        compiler_params=_params("arbitrary", "arbitrary"),
        name=f"dilated_d{d}",
    )(view(q), view(k), view(v))
    return o.reshape(bsz * seq, w), lse.reshape(bsz * seq, w)


def _swa_kernel(sink_ref, q_ref, k_ref, v_ref, o_ref, *, nblk, max_dist):
    pair = pl.program_id(1)
    sink_vals = (sink_ref[2 * pair], sink_ref[2 * pair + 1])

    def body(i, carry):
        q0, o, _ = _band_block(q_ref, k_ref, v_ref, i, max_dist, sink_vals)
        o_ref[pl.ds(q0, BLOCK), :] = o.astype(o_ref.dtype)
        return carry

    lax.fori_loop(0, nblk, body, 0)


def _swa_attention(q, kdup, vdup, sinks):
    bsz, seq, w = q.shape
    pairs = w // LANES
    group = pairs // C_KV_HEADS
    qspec = pl.BlockSpec((None, seq, LANES), lambda b, p: (b, 0, p))
    kvspec = pl.BlockSpec((None, seq, LANES), lambda b, p: (b, 0, p // group))
    return pl.pallas_call(
        functools.partial(_swa_kernel, nblk=seq // BLOCK, max_dist=C_WINDOW - 1),
        grid=(bsz, pairs),
        in_specs=[pl.BlockSpec(memory_space=pltpu.SMEM), qspec, kvspec, kvspec],
        out_specs=qspec,
        out_shape=jax.ShapeDtypeStruct((bsz, seq, w), BF16),
        compiler_params=_params("arbitrary", "arbitrary"),
        name="swa_sink",
    )(sinks, q, kdup, vdup)


def _merge_out_kernel(h_ref, rec_ref, o1_ref, o2_ref, o3_ref, l1_ref, l2_ref, l3_ref, w_ref,
                      out_ref):
    l1, l2, l3 = l1_ref[...], l2_ref[...], l3_ref[...]
    m = jnp.maximum(jnp.maximum(l1, l2), l3)
    w1, w2, w3 = jnp.exp(l1 - m), jnp.exp(l2 - m), jnp.exp(l3 - m)
    att = (w1 * o1_ref[...] + w2 * o2_ref[...] + w3 * o3_ref[...]) * (1.0 / (w1 + w2 + w3))
    out = _dot(rec_ref[...], w_ref[0:LRU_WIDTH, :])
    out = out + _dot(att.astype(BF16), w_ref[LRU_WIDTH:LRU_WIDTH + B_WIDTH, :])
    out_ref[...] = h_ref[...] + out


def _merge_out(h, rec, outs, lses, w_out, tm=TOK_TILE):
    t, d = h.shape
    tok = lambda w: pl.BlockSpec((tm, w), lambda i: (i, 0))
    return pl.pallas_call(
        _merge_out_kernel,
        grid=(t // tm,),
        in_specs=[tok(d), tok(LRU_WIDTH)] + [tok(B_WIDTH)] * 6 + [_const_spec(w_out.shape)],
        out_specs=tok(d),
        out_shape=jax.ShapeDtypeStruct((t, d), F32),
        compiler_params=_params("arbitrary"),
        name="merge_out",
    )(h, rec, *outs, *lses, w_out)


def _swa_proj_kernel(h_ref, g_ref, w_ref, b_ref, rc_ref, rs1_ref, rs2_ref, q_ref, k_ref, v_ref):
    hn = _rms(h_ref[...], g_ref[...]).astype(BF16)
    c, s1, s2 = rc_ref[...], rs1_ref[...], rs2_ref[...]
    kw = 2 * C_KV_WIDTH
    q = _dot(hn, w_ref[:, 0:C_Q_WIDTH]) + b_ref[:, 0:C_Q_WIDTH]
    q_ref[...] = (_rope(q, c, s1, s2) * (HEAD_DIM ** -0.5)).astype(BF16)
    k = _dot(hn, w_ref[:, C_Q_WIDTH:C_Q_WIDTH + kw]) + b_ref[:, C_Q_WIDTH:C_Q_WIDTH + kw]
    k_ref[...] = _rope(k, c, s1, s2).astype(BF16)
    o = C_Q_WIDTH + kw
    v_ref[...] = (_dot(hn, w_ref[:, o:o + kw]) + b_ref[:, o:o + kw]).astype(BF16)


def _swa_proj(h, g, w, b, tables, tm=TOK_TILE):
    bsz, seq, d = h.shape
    kw = 2 * C_KV_WIDTH
    tok = lambda width: pl.BlockSpec((None, tm, width), lambda bi, i: (bi, i, 0))
    tab = pl.BlockSpec((tm, LANES), lambda bi, i: (i, 0))
    return pl.pallas_call(
        _swa_proj_kernel,
        grid=(bsz, seq // tm),
        in_specs=[tok(d), _const_spec((1, d)), _const_spec(w.shape), _const_spec((1, w.shape[1])),
                  tab, tab, tab],
        out_specs=(tok(C_Q_WIDTH), tok(kw), tok(kw)),
        out_shape=(jax.ShapeDtypeStruct((bsz, seq, C_Q_WIDTH), BF16),
                   jax.ShapeDtypeStruct((bsz, seq, kw), BF16),
                   jax.ShapeDtypeStruct((bsz, seq, kw), BF16)),
        compiler_params=_params("arbitrary", "arbitrary"),
        name="swa_proj",
    )(h, g.reshape(1, -1), w, b.reshape(1, -1), *tables)


def _proj_out_kernel(h_ref, a_ref, w_ref, b_ref, out_ref):
    out_ref[...] = h_ref[...] + _dot(a_ref[...], w_ref[...]) + b_ref[...]


def _proj_out(h, a, w, b, tm=TOK_TILE):
    t, d = h.shape
    return pl.pallas_call(
        _proj_out_kernel,
        grid=(t // tm,),
        in_specs=[pl.BlockSpec((tm, d), lambda i: (i, 0)),
                  pl.BlockSpec((tm, a.shape[1]), lambda i: (i, 0)),
                  _const_spec(w.shape), _const_spec((1, d))],
        out_specs=pl.BlockSpec((tm, d), lambda i: (i, 0)),
        out_shape=jax.ShapeDtypeStruct((t, d), F32),
        compiler_params=_params("arbitrary"),
        name="proj_out",
    )(h, a, w, b.reshape(1, -1))


def _mem_kv_kernel(mem_ref, g_ref, w_ref, k_ref, v_ref):
    mn = _rms(mem_ref[...], g_ref[...]).astype(BF16)
    k_ref[...] = _dot(mn, w_ref[:, 0:XA_WIDTH]).astype(BF16)
    v_ref[...] = _dot(mn, w_ref[:, XA_WIDTH:2 * XA_WIDTH]).astype(BF16)


def _mem_kv(mem, g, wkv):
    bsz, m, d = mem.shape
    spec = pl.BlockSpec((None, m, XA_WIDTH), lambda b: (b, 0, 0))
    return pl.pallas_call(
        _mem_kv_kernel,
        grid=(bsz,),
        in_specs=[pl.BlockSpec((None, m, d), lambda b: (b, 0, 0)), _const_spec((1, d)),
                  _const_spec(wkv.shape)],
        out_specs=(spec, spec),
        out_shape=(jax.ShapeDtypeStruct((bsz, m, XA_WIDTH), BF16),) * 2,
        compiler_params=_params("arbitrary"),
        name="mem_kv",
    )(mem, g.reshape(1, -1), wkv)


def _xattn_kernel(h_ref, g_ref, wq_ref, k_ref, v_ref, wo_ref, out_ref):
    h = h_ref[...]
    q = _dot(_rms(h, g_ref[...]).astype(BF16), wq_ref[...]).astype(BF16)
    scale = XA_HEAD_DIM ** -0.5
    heads = []
    for hd in range(XA_HEADS):
        sl = slice(hd * XA_HEAD_DIM, (hd + 1) * XA_HEAD_DIM)
        s = _dot_nt(q[:, sl], k_ref[:, sl]) * scale
        e = jnp.exp(s - jnp.max(s, axis=-1, keepdims=True))
        den = jnp.sum(e, axis=-1, keepdims=True)
        heads.append((_dot(e.astype(BF16), v_ref[:, sl]) * (1.0 / den)).astype(BF16))
    out_ref[...] = h + _dot(jnp.concatenate(heads, axis=1), wo_ref[...])


def _xattn(h, g, wq, k, v, wo, tm=TOK_TILE):
    bsz, seq, d = h.shape
    tok = pl.BlockSpec((None, tm, d), lambda b, i: (b, i, 0))
    kv = pl.BlockSpec((None, N_MEM, XA_WIDTH), lambda b, i: (b, 0, 0))
    return pl.pallas_call(
        _xattn_kernel,
        grid=(bsz, seq // tm),
        in_specs=[tok, _const_spec((1, d)), _const_spec(wq.shape), kv, kv, _const_spec(wo.shape)],
        out_specs=tok,
        out_shape=jax.ShapeDtypeStruct((bsz, seq, d), F32),
        compiler_params=_params("arbitrary", "arbitrary"),
        name="xattn",
    )(h, g.reshape(1, -1), wq, k, v, wo)


def _ffn_kernel(h_ref, g_ref, wgu_ref, wd_ref, fg_ref, out_ref, *, d_ff, final_norm):
    h = h_ref[...]
    hn = _rms(h, g_ref[...]).astype(BF16)
    acc = h
    for c0 in range(0, d_ff, FF_CHUNK):
        gate = _dot(hn, wgu_ref[:, c0:c0 + FF_CHUNK])
        up = _dot(hn, wgu_ref[:, d_ff + c0:d_ff + c0 + FF_CHUNK])
        act = (gate * jax.nn.sigmoid(gate) * up).astype(BF16)
        acc = acc + _dot(act, wd_ref[c0:c0 + FF_CHUNK, :])
    out_ref[...] = _rms(acc, fg_ref[...]) if final_norm else acc


def _ffn(h, g, wgu, wd, final_g, final_norm, tm=TOK_TILE):
    t, d = h.shape
    d_ff = wd.shape[0]
    assert d_ff % FF_CHUNK == 0
    tok = pl.BlockSpec((tm, d), lambda i: (i, 0))
    return pl.pallas_call(
        functools.partial(_ffn_kernel, d_ff=d_ff, final_norm=final_norm),
        grid=(t // tm,),
        in_specs=[tok, _const_spec((1, d)), _const_spec(wgu.shape), _const_spec(wd.shape),
                  _const_spec((1, d))],
        out_specs=tok,
        out_shape=jax.ShapeDtypeStruct((t, d), F32),
        compiler_params=_params("arbitrary"),
        name="ffn",
    )(h, g.reshape(1, -1), wgu, wd, final_g.reshape(1, -1))


def _channel_block(h, mem, layer, bsz, seq, xa_norm, xa_mem_norm, xa_wq, xa_wkv, xa_wo,
                   ffn_norm, ffn_w_gate_up, ffn_w_down, final_norm, last):
    d = h.shape[-1]
    mk, mv = _mem_kv(mem, xa_mem_norm[layer], xa_wkv[layer].astype(BF16))
    h = _xattn(h.reshape(bsz, seq, d), xa_norm[layer], xa_wq[layer].astype(BF16), mk, mv,
               xa_wo[layer].astype(BF16)).reshape(bsz * seq, d)
    return _ffn(h, ffn_norm[layer], ffn_w_gate_up[layer].astype(BF16),
                ffn_w_down[layer].astype(BF16), final_norm, last)


def kernel(x, mem, mix_norm, ab_w_in, lru_conv_w, lru_conv_b, lru_wa, lru_ba, lru_wx, lru_bx, lru_lambda, ab_w_out, c_w_qkv, c_b_qkv, c_sinks, c_w_out, c_b_out, xa_norm, xa_mem_norm, xa_wq, xa_wkv, xa_wo, ffn_norm, ffn_w_gate_up, ffn_w_down, final_norm):
    bsz, seq, d = x.shape
    t = bsz * seq
    tables = _rope_tables(seq)
    channel = functools.partial(
        _channel_block, mem=mem, bsz=bsz, seq=seq, xa_norm=xa_norm, xa_mem_norm=xa_mem_norm,
        xa_wq=xa_wq, xa_wkv=xa_wkv, xa_wo=xa_wo, ffn_norm=ffn_norm,
        ffn_w_gate_up=ffn_w_gate_up, ffn_w_down=ffn_w_down, final_norm=final_norm)

    rec, q, k, v = _lru_proj(
        x, mix_norm[0], ab_w_in[0].astype(BF16), lru_conv_w[0], lru_conv_b[0],
        lru_wa[0].astype(BF16), lru_ba[0], lru_wx[0].astype(BF16), lru_bx[0], lru_lambda[0],
        tables)
    outs, lses = zip(*[_dilated_branch(q, k, v, window, dil) for window, dil in DILATED_PATTERN])
    h = _merge_out(x.reshape(t, d), rec.reshape(t, LRU_WIDTH), outs, lses,
                   ab_w_out[0].astype(BF16))
    h = channel(h, layer=0, last=False)

    w_qkv, b_qkv = c_w_qkv[0], c_b_qkv[0]

    def dup_heads(a):
        parts = a.reshape(a.shape[0], C_KV_HEADS, 1, HEAD_DIM)
        return jnp.broadcast_to(parts, (a.shape[0], C_KV_HEADS, 2, HEAD_DIM)).reshape(
            a.shape[0], 2 * C_KV_WIDTH)

    def widen(a):
        qq = a[:, :C_Q_WIDTH]
        kk = a[:, C_Q_WIDTH:C_Q_WIDTH + C_KV_WIDTH]
        vv = a[:, C_Q_WIDTH + C_KV_WIDTH:]
        return jnp.concatenate([qq, dup_heads(kk), dup_heads(vv)], axis=1)

    q1, k1, v1 = _swa_proj(h.reshape(bsz, seq, d), mix_norm[1], widen(w_qkv).astype(BF16),
                           widen(b_qkv.reshape(1, -1))[0], tables)
    att = _swa_attention(q1, k1, v1, c_sinks[0])
    h = _proj_out(h, att.reshape(t, C_Q_WIDTH), c_w_out[0].astype(BF16), c_b_out[0])
    h = channel(h, layer=1, last=True)
    return h.reshape(bsz, seq, d)
```

```python
import functools
import math

import jax
import jax.numpy as jnp
from jax import lax
from jax.experimental import pallas as pl
from jax.experimental.pallas import tpu as pltpu

F32 = jnp.float32
BF16 = jnp.bfloat16

D_MODEL = 1024
N_MEM = 256
NORM_EPS = 1e-6
ROPE_THETA = 500000.0
HEAD_DIM = 64
ROT_DIM = HEAD_DIM // 4
ROT_HALF = ROT_DIM // 2
LANES = 128
SUBLANES = 8
BLOCK = 128
LRU_WIDTH = D_MODEL
LRU_HEADS = 4
LRU_HEAD_DIM = LRU_WIDTH // LRU_HEADS
CONV_WIDTH = 4
LRU_C = 8.0
B_HEADS = 8
B_WIDTH = B_HEADS * HEAD_DIM
DILATED_PATTERN = ((128, 1), (512, 4), (2048, 16))
C_HEADS = 16
C_KV_HEADS = 2
C_WINDOW = 128
C_Q_WIDTH = C_HEADS * HEAD_DIM
C_KV_WIDTH = C_KV_HEADS * HEAD_DIM
XA_HEADS = 4
XA_HEAD_DIM = 128
XA_WIDTH = XA_HEADS * XA_HEAD_DIM
NEG = -1e30
VMEM_LIMIT = 56 * 1024 * 1024

TOK_TILE = 512
FF_CHUNK = 1408


def _dot(a, b):
    return jnp.dot(a, b, preferred_element_type=F32)


def _dot_nt(a, b):
    return lax.dot_general(a, b, (((1,), (1,)), ((), ())), preferred_element_type=F32)


def _rms(x, g):
    return x * lax.rsqrt(jnp.mean(x * x, axis=-1, keepdims=True) + NORM_EPS) * g


def _params(*sem):
    return pltpu.CompilerParams(dimension_semantics=sem, vmem_limit_bytes=VMEM_LIMIT)


def _const_spec(shape):
    nd = len(shape)
    return pl.BlockSpec(shape, lambda *_: (0,) * nd, pipeline_mode=pl.Buffered(1))


def _rope(x, c, s1, s2):
    reps = x.shape[1] // LANES
    c, s1, s2 = (jnp.concatenate([t] * reps, axis=1) for t in (c, s1, s2))
    width = x.shape[1]
    return x * c + pltpu.roll(x, ROT_HALF, 1) * s1 + pltpu.roll(x, width - ROT_HALF, 1) * s2


def _rope_tables(seq):
    inv = ROPE_THETA ** (-jnp.arange(0, ROT_DIM, 2, dtype=F32) / ROT_DIM)
    ang = jnp.arange(seq, dtype=F32)[:, None] * inv[None, :]
    cos, sin = jnp.cos(ang), jnp.sin(ang)
    zeros = jnp.zeros((seq, HEAD_DIM - ROT_DIM), F32)
    zh = jnp.zeros((seq, ROT_HALF), F32)
    c = jnp.concatenate([cos, cos, zeros + 1.0], axis=1)
    s1 = jnp.concatenate([zh, sin, zeros], axis=1)
    s2 = jnp.concatenate([-sin, zh, zeros], axis=1)
    return tuple(jnp.concatenate([t, t], axis=1) for t in (c, s1, s2))


def _lru_proj_kernel(h_ref, g_ref, w_ref, cw_ref, cb_ref, wa_ref, ba_ref, wx_ref, bx_ref,
                     lam_ref, rc_ref, rs1_ref, rs2_ref,
                     rec_ref, q_ref, k_ref, v_ref,
                     xs_ref, as_ref, us_ref, hs_ref, hprev_ref, *, tl):
    pad = SUBLANES

    @pl.when(pl.program_id(1) == 0)
    def _():
        xs_ref[0:pad, :] = jnp.zeros((pad, LRU_WIDTH), F32)
        hprev_ref[...] = jnp.zeros_like(hprev_ref)

    as_ref[0:pad, :] = jnp.ones((pad, LRU_WIDTH), F32)
    us_ref[0:pad, :] = jnp.zeros((pad, LRU_WIDTH), F32)

    hn = _rms(h_ref[...], g_ref[...]).astype(BF16)

    x = _dot(hn, w_ref[:, 0:LRU_WIDTH])
    xs_ref[pad:pad + tl, :] = x
    xc = cb_ref[...] + xs_ref[pad - 3:pad - 3 + tl, :] * cw_ref[0:1, :]
    xc = xc + xs_ref[pad - 2:pad - 2 + tl, :] * cw_ref[1:2, :]
    xc = xc + xs_ref[pad - 1:pad - 1 + tl, :] * cw_ref[2:3, :]
    xc = xc + x * cw_ref[3:4, :]
    xs_ref[0:pad, :] = xs_ref[tl:tl + pad, :]

    xcb = xc.astype(BF16)
    r_parts, i_parts = [], []
    for hd in range(LRU_HEADS):
        sl = slice(hd * LRU_HEAD_DIM, (hd + 1) * LRU_HEAD_DIM)
        r_parts.append(_dot(xcb[:, sl], wa_ref[hd]))
        i_parts.append(_dot(xcb[:, sl], wx_ref[hd]))
    r = jax.nn.sigmoid(jnp.concatenate(r_parts, axis=1) + ba_ref[...])
    ig = jax.nn.sigmoid(jnp.concatenate(i_parts, axis=1) + bx_ref[...])

    nlam = -lam_ref[...]
    softplus = jnp.maximum(nlam, 0.0) + jnp.log1p(jnp.exp(-jnp.abs(nlam)))
    log_a = (-LRU_C) * r * softplus
    a = jnp.exp(log_a)
    u = jnp.sqrt(jnp.tanh(-log_a) * (1.0 + a * a)) * (ig * xc)

    as_ref[pad:pad + tl, :] = a
    us_ref[pad:pad + tl, :] = u
    for s in (1, 2, 4):
        a_cur = as_ref[pad:pad + tl, :]
        u_cur = us_ref[pad:pad + tl, :]
        a_sh = as_ref[pad - s:pad - s + tl, :]
        u_sh = us_ref[pad - s:pad - s + tl, :]
        us_ref[pad:pad + tl, :] = a_cur * u_sh + u_cur
        as_ref[pad:pad + tl, :] = a_cur * a_sh

    def chain(g, hcar):
        r0 = pl.multiple_of(pad + g * SUBLANES, SUBLANES)
        hcar = as_ref[pl.ds(r0, SUBLANES), :] * hcar + us_ref[pl.ds(r0, SUBLANES), :]
        hs_ref[pl.ds(pl.multiple_of(g * SUBLANES, SUBLANES), SUBLANES), :] = hcar
        return hcar

    h0 = jnp.broadcast_to(hprev_ref[SUBLANES - 1:SUBLANES, :], (SUBLANES, LRU_WIDTH))
    hprev_ref[...] = lax.fori_loop(0, tl // SUBLANES, chain, h0, unroll=8)

    y = _dot(hn, w_ref[:, LRU_WIDTH:2 * LRU_WIDTH])
    gelu = 0.5 * y * (1.0 + jnp.tanh(math.sqrt(2.0 / math.pi) * (y + 0.044715 * (y * y * y))))
    rec_ref[...] = (hs_ref[...] * gelu).astype(BF16)

    c, s1, s2 = rc_ref[...], rs1_ref[...], rs2_ref[...]
    o = 2 * LRU_WIDTH
    q = _dot(hn, w_ref[:, o:o + B_WIDTH])
    q_ref[...] = (_rope(q, c, s1, s2) * (HEAD_DIM ** -0.5)).astype(BF16)
    k = _dot(hn, w_ref[:, o + B_WIDTH:o + 2 * B_WIDTH])
    k_ref[...] = _rope(k, c, s1, s2).astype(BF16)
    v_ref[...] = _dot(hn, w_ref[:, o + 2 * B_WIDTH:o + 3 * B_WIDTH]).astype(BF16)


def _lru_proj(h, g, w_in, conv_w, conv_b, wa, ba, wx, bx, lam, tables, tl=TOK_TILE):
    bsz, seq, d = h.shape
    row = lambda a: a.reshape(1, -1)
    tok = lambda w: pl.BlockSpec((None, tl, w), lambda b, i: (b, i, 0))
    tab = pl.BlockSpec((tl, LANES), lambda b, i: (i, 0))
    out_shape = (jax.ShapeDtypeStruct((bsz, seq, LRU_WIDTH), BF16),) + tuple(
        jax.ShapeDtypeStruct((bsz, seq, B_WIDTH), BF16) for _ in range(3))
    return pl.pallas_call(
        functools.partial(_lru_proj_kernel, tl=tl),
        grid=(bsz, seq // tl),
        in_specs=[tok(d), _const_spec((1, d)), _const_spec(w_in.shape),
                  _const_spec(conv_w.shape), _const_spec((1, LRU_WIDTH)),
                  _const_spec(wa.shape), _const_spec((1, LRU_WIDTH)),
                  _const_spec(wx.shape), _const_spec((1, LRU_WIDTH)),
                  _const_spec((1, LRU_WIDTH)), tab, tab, tab],
        out_specs=(tok(LRU_WIDTH), tok(B_WIDTH), tok(B_WIDTH), tok(B_WIDTH)),
        out_shape=out_shape,
        scratch_shapes=[pltpu.VMEM((tl + SUBLANES, LRU_WIDTH), F32),
                        pltpu.VMEM((tl + SUBLANES, LRU_WIDTH), F32),
                        pltpu.VMEM((tl + SUBLANES, LRU_WIDTH), F32),
                        pltpu.VMEM((tl, LRU_WIDTH), F32),
                        pltpu.VMEM((SUBLANES, LRU_WIDTH), F32)],
        compiler_params=_params("arbitrary", "arbitrary"),
        name="lru_proj",
    )(h, row(g), w_in, conv_w, row(conv_b), wa, row(ba), wx, row(bx), row(lam), *tables)


def _fill_band_bias(bias_ref, max_dist, reps):
    rows = lax.broadcasted_iota(jnp.int32, (BLOCK, 2 * BLOCK), 0)
    cols = lax.broadcasted_iota(jnp.int32, (BLOCK, 2 * BLOCK), 1)
    for first, off in ((0, 0), (1, BLOCK)):
        dist = off + rows - cols
        valid = jnp.logical_and(dist >= 0, dist <= max_dist)
        tile = jnp.where(valid, 0.0, NEG).astype(F32)
        for rep in range(reps):
            bias_ref[first, rep * BLOCK:(rep + 1) * BLOCK, :] = tile


def _band_attend(qb, kb, vb, bias, sinks, want_lse):
    lane_k = lax.broadcasted_iota(jnp.int32, kb.shape, 1)
    zero = jnp.zeros((), kb.dtype)
    o, lses = None, []
    for hh in range(2):
        mine = (lane_k < HEAD_DIM) if hh == 0 else (lane_k >= HEAD_DIM)
        s = _dot_nt(qb, jnp.where(mine, kb, zero)) + bias
        m = jnp.max(s, axis=-1, keepdims=True)
        if sinks is not None:
            m = jnp.maximum(m, sinks[hh])
        e = jnp.exp(s - m)
        den = jnp.sum(e, axis=-1, keepdims=True)
        if sinks is not None:
            den = den + jnp.exp(sinks[hh] - m)
        oh = _dot(e.astype(vb.dtype), jnp.where(mine, vb, zero)) * (1.0 / den)
        o = oh if o is None else o + oh
        if want_lse:
            lses.append(m + jnp.log(den))
    if not want_lse:
        return o, None
    lane_o = lax.broadcasted_iota(jnp.int32, o.shape, 1)
    return o, jnp.where(lane_o < HEAD_DIM, lses[0], lses[1])


def _band_window(n, nblk):
    first = jnp.minimum(lax.rem(n, nblk), 1)
    q0 = pl.multiple_of(n * BLOCK, BLOCK)
    ks = pl.multiple_of(q0 - first * BLOCK, BLOCK)
    return q0, ks, first


def _dilated_kernel(q_ref, k_ref, v_ref, out_ref, stage_ref, p4_ref, perm_ref, o_ref, l_ref,
                    bias_ref, *, seq, max_dist):
    _fill_band_bias(bias_ref, max_dist, 1)

    l4, l16 = seq // 4, seq // 16
    for a, x_ref in enumerate((q_ref, k_ref, v_ref)):
        stage_ref[...] = x_ref[...].astype(F32)
        for r in range(4):
            blk = stage_ref[pl.ds(r, l4, stride=4), :]
            p4_ref[r * l4:(r + 1) * l4, :] = blk
            perm_ref[0, a, r * l4:(r + 1) * l4, :] = blk.astype(BF16)
        for r in range(4):
            for rb in range(4):
                blk = p4_ref[pl.ds(r * l4 + rb, l16, stride=4), :]
                res = r + 4 * rb
                perm_ref[1, a, res * l16:(res + 1) * l16, :] = blk.astype(BF16)

    branches = ((1, (q_ref, k_ref, v_ref)),
                (4, tuple(perm_ref.at[0, a] for a in range(3))),
                (16, tuple(perm_ref.at[1, a] for a in range(3))))
    for bi, (d, (qr, kr, vr)) in enumerate(branches):
        nblk = seq // d // BLOCK

        def body(n, carry, d=d, bi=bi, qr=qr, kr=kr, vr=vr, nblk=nblk):
            q0, ks, first = _band_window(n, nblk)
            o, lse = _band_attend(qr[pl.ds(q0, BLOCK), :], kr[pl.ds(ks, 2 * BLOCK), :],
                                  vr[pl.ds(ks, 2 * BLOCK), :], bias_ref[first], None, True)
            if d == 1:
                rows = pl.ds(q0, BLOCK)
            else:
                rows = pl.ds(lax.rem(n, nblk) * (BLOCK * d) + n // nblk, BLOCK, stride=d)
            o_ref.at[bi][rows, :] = o
            l_ref.at[bi][rows, :] = lse
            return carry

        lax.fori_loop(0, seq // BLOCK, body, 0, unroll=2)

    chunk = 4 * BLOCK

    def merge(c, carry):
        rows = pl.ds(pl.multiple_of(c * chunk, chunk), chunk)
        ls = [l_ref[bi, rows, :] for bi in range(3)]
        m = jnp.maximum(jnp.maximum(ls[0], ls[1]), ls[2])
        ws = [jnp.exp(l - m) for l in ls]
        acc = ws[0] * o_ref[0, rows, :] + ws[1] * o_ref[1, rows, :] + ws[2] * o_ref[2, rows, :]
        out_ref[rows, :] = (acc * (1.0 / (ws[0] + ws[1] + ws[2]))).astype(out_ref.dtype)
        return carry

    lax.fori_loop(0, seq // chunk, merge, 0)


def _dilated_attention(q, k, v):
    assert tuple(d for _, d in DILATED_PATTERN) == (1, 4, 16)
    assert len({w // d for w, d in DILATED_PATTERN}) == 1
    max_dist = DILATED_PATTERN[0][0] // DILATED_PATTERN[0][1]
    bsz, seq, w = q.shape
    spec = pl.BlockSpec((None, seq, LANES), lambda b, c: (b, 0, c))
    return pl.pallas_call(
        functools.partial(_dilated_kernel, seq=seq, max_dist=max_dist),
        grid=(bsz, w // LANES),
        in_specs=[spec, spec, spec],
        out_specs=spec,
        out_shape=jax.ShapeDtypeStruct((bsz, seq, w), BF16),
        scratch_shapes=[pltpu.VMEM((seq, LANES), F32), pltpu.VMEM((seq, LANES), F32),
                        pltpu.VMEM((2, 3, seq, LANES), BF16),
                        pltpu.VMEM((3, seq, LANES), F32), pltpu.VMEM((3, seq, LANES), F32),
                        pltpu.VMEM((2, BLOCK, 2 * BLOCK), F32)],
        compiler_params=_params("arbitrary", "arbitrary"),
        name="dilated",
    )(q, k, v)


def _swa_kernel(sink_ref, q_ref, k_ref, v_ref, o_ref, bias_ref, *, nblk, max_dist, pairs):
    _fill_band_bias(bias_ref, max_dist, pairs)
    head0 = pl.program_id(1) * (2 * pairs)
    sinks = [jnp.concatenate([jnp.full((BLOCK, 1), sink_ref[head0 + 2 * p + hh], F32)
                              for p in range(pairs)], axis=0) for hh in range(2)]

    def body(i, carry):
        q0, ks, first = _band_window(i, nblk)
        qrow = q_ref[pl.ds(q0, BLOCK), :]
        qb = jnp.concatenate([qrow[:, p * LANES:(p + 1) * LANES] for p in range(pairs)], axis=0)
        o, _ = _band_attend(qb, k_ref[pl.ds(ks, 2 * BLOCK), :], v_ref[pl.ds(ks, 2 * BLOCK), :],
                            bias_ref[first], sinks, False)
        o_ref[pl.ds(q0, BLOCK), :] = jnp.concatenate(
            [o[p * BLOCK:(p + 1) * BLOCK, :] for p in range(pairs)], axis=1).astype(o_ref.dtype)
        return carry

    lax.fori_loop(0, nblk, body, 0)


def _swa_attention(q, kdup, vdup, sinks):
    bsz, seq, w = q.shape
    pairs = w // LANES // C_KV_HEADS
    qspec = pl.BlockSpec((None, seq, pairs * LANES), lambda b, g: (b, 0, g))
    kvspec = pl.BlockSpec((None, seq, LANES), lambda b, g: (b, 0, g))
    return pl.pallas_call(
        functools.partial(_swa_kernel, nblk=seq // BLOCK, max_dist=C_WINDOW - 1, pairs=pairs),
        grid=(bsz, C_KV_HEADS),
        in_specs=[pl.BlockSpec(memory_space=pltpu.SMEM), qspec, kvspec, kvspec],
        out_specs=qspec,
        out_shape=jax.ShapeDtypeStruct((bsz, seq, w), BF16),
        scratch_shapes=[pltpu.VMEM((2, pairs * BLOCK, 2 * BLOCK), F32)],
        compiler_params=_params("arbitrary", "arbitrary"),
        name="swa_sink",
    )(sinks, q, kdup, vdup)


def _ab_out_kernel(h_ref, rec_ref, att_ref, w_ref, out_ref):
    out = _dot(rec_ref[...], w_ref[0:LRU_WIDTH, :])
    out = out + _dot(att_ref[...], w_ref[LRU_WIDTH:LRU_WIDTH + B_WIDTH, :])
    out_ref[...] = h_ref[...] + out


def _ab_out(h, rec, att, w_out, tm=TOK_TILE):
    t, d = h.shape
    tok = lambda w: pl.BlockSpec((tm, w), lambda i: (i, 0))
    return pl.pallas_call(
        _ab_out_kernel,
        grid=(t // tm,),
        in_specs=[tok(d), tok(LRU_WIDTH), tok(B_WIDTH), _const_spec(w_out.shape)],
        out_specs=tok(d),
        out_shape=jax.ShapeDtypeStruct((t, d), F32),
        compiler_params=_params("arbitrary"),
        name="ab_out",
    )(h, rec, att, w_out)


def _swa_proj_kernel(h_ref, g_ref, w_ref, b_ref, rc_ref, rs1_ref, rs2_ref, q_ref, k_ref, v_ref):
    hn = _rms(h_ref[...], g_ref[...]).astype(BF16)
    c, s1, s2 = rc_ref[...], rs1_ref[...], rs2_ref[...]
    kw = 2 * C_KV_WIDTH
    q = _dot(hn, w_ref[:, 0:C_Q_WIDTH]) + b_ref[:, 0:C_Q_WIDTH]
    q_ref[...] = (_rope(q, c, s1, s2) * (HEAD_DIM ** -0.5)).astype(BF16)
    k = _dot(hn, w_ref[:, C_Q_WIDTH:C_Q_WIDTH + kw]) + b_ref[:, C_Q_WIDTH:C_Q_WIDTH + kw]
    k_ref[...] = _rope(k, c, s1, s2).astype(BF16)
    o = C_Q_WIDTH + kw
    v_ref[...] = (_dot(hn, w_ref[:, o:o + kw]) + b_ref[:, o:o + kw]).astype(BF16)


def _swa_proj(h, g, w, b, tables, tm=TOK_TILE):
    bsz, seq, d = h.shape
    kw = 2 * C_KV_WIDTH
    tok = lambda width: pl.BlockSpec((None, tm, width), lambda bi, i: (bi, i, 0))
    tab = pl.BlockSpec((tm, LANES), lambda bi, i: (i, 0))
    return pl.pallas_call(
        _swa_proj_kernel,
        grid=(bsz, seq // tm),
        in_specs=[tok(d), _const_spec((1, d)), _const_spec(w.shape), _const_spec((1, w.shape[1])),
                  tab, tab, tab],
        out_specs=(tok(C_Q_WIDTH), tok(kw), tok(kw)),
        out_shape=(jax.ShapeDtypeStruct((bsz, seq, C_Q_WIDTH), BF16),
                   jax.ShapeDtypeStruct((bsz, seq, kw), BF16),
                   jax.ShapeDtypeStruct((bsz, seq, kw), BF16)),
        compiler_params=_params("arbitrary", "arbitrary"),
        name="swa_proj",
    )(h, g.reshape(1, -1), w, b.reshape(1, -1), *tables)


def _proj_out_kernel(h_ref, a_ref, w_ref, b_ref, out_ref):
    out_ref[...] = h_ref[...] + _dot(a_ref[...], w_ref[...]) + b_ref[...]


def _proj_out(h, a, w, b, tm=TOK_TILE):
    t, d = h.shape
    return pl.pallas_call(
        _proj_out_kernel,
        grid=(t // tm,),
        in_specs=[pl.BlockSpec((tm, d), lambda i: (i, 0)),
                  pl.BlockSpec((tm, a.shape[1]), lambda i: (i, 0)),
                  _const_spec(w.shape), _const_spec((1, d))],
        out_specs=pl.BlockSpec((tm, d), lambda i: (i, 0)),
        out_shape=jax.ShapeDtypeStruct((t, d), F32),
        compiler_params=_params("arbitrary"),
        name="proj_out",
    )(h, a, w, b.reshape(1, -1))


def _mem_kv_kernel(mem_ref, g_ref, w_ref, k_ref, v_ref):
    mn = _rms(mem_ref[...], g_ref[...]).astype(BF16)
    k_ref[...] = _dot(mn, w_ref[:, 0:XA_WIDTH]).astype(BF16)
    v_ref[...] = _dot(mn, w_ref[:, XA_WIDTH:2 * XA_WIDTH]).astype(BF16)


def _mem_kv(mem, g, wkv):
    bsz, m, d = mem.shape
    spec = pl.BlockSpec((None, m, XA_WIDTH), lambda b: (b, 0, 0))
    return pl.pallas_call(
        _mem_kv_kernel,
        grid=(bsz,),
        in_specs=[pl.BlockSpec((None, m, d), lambda b: (b, 0, 0)), _const_spec((1, d)),
                  _const_spec(wkv.shape)],
        out_specs=(spec, spec),
        out_shape=(jax.ShapeDtypeStruct((bsz, m, XA_WIDTH), BF16),) * 2,
        compiler_params=_params("arbitrary"),
        name="mem_kv",
    )(mem, g.reshape(1, -1), wkv)


def _xattn_kernel(h_ref, g_ref, wq_ref, k_ref, v_ref, wo_ref, out_ref):
    h = h_ref[...]
    q = _dot(_rms(h, g_ref[...]).astype(BF16), wq_ref[...]).astype(BF16)
    scale = XA_HEAD_DIM ** -0.5
    heads = []
    for hd in range(XA_HEADS):
        sl = slice(hd * XA_HEAD_DIM, (hd + 1) * XA_HEAD_DIM)
        s = _dot_nt(q[:, sl], k_ref[:, sl]) * scale
        e = jnp.exp(s - jnp.max(s, axis=-1, keepdims=True))
        den = jnp.sum(e, axis=-1, keepdims=True)
        heads.append((_dot(e.astype(BF16), v_ref[:, sl]) * (1.0 / den)).astype(BF16))
    out_ref[...] = h + _dot(jnp.concatenate(heads, axis=1), wo_ref[...])


def _xattn(h, g, wq, k, v, wo, tm=TOK_TILE):
    bsz, seq, d = h.shape
    tok = pl.BlockSpec((None, tm, d), lambda b, i: (b, i, 0))
    kv = pl.BlockSpec((None, N_MEM, XA_WIDTH), lambda b, i: (b, 0, 0))
    return pl.pallas_call(
        _xattn_kernel,
        grid=(bsz, seq // tm),
        in_specs=[tok, _const_spec((1, d)), _const_spec(wq.shape), kv, kv, _const_spec(wo.shape)],
        out_specs=tok,
        out_shape=jax.ShapeDtypeStruct((bsz, seq, d), F32),
        compiler_params=_params("arbitrary", "arbitrary"),
        name="xattn",
    )(h, g.reshape(1, -1), wq, k, v, wo)


def _ffn_kernel(h_ref, g_ref, wgu_ref, wd_ref, fg_ref, out_ref, *, d_ff, final_norm):
    h = h_ref[...]
    hn = _rms(h, g_ref[...]).astype(BF16)
    acc = h
    for c0 in range(0, d_ff, FF_CHUNK):
        gate = _dot(hn, wgu_ref[:, c0:c0 + FF_CHUNK])
        up = _dot(hn, wgu_ref[:, d_ff + c0:d_ff + c0 + FF_CHUNK])
        act = (gate * jax.nn.sigmoid(gate) * up).astype(BF16)
        acc = acc + _dot(act, wd_ref[c0:c0 + FF_CHUNK, :])
    out_ref[...] = _rms(acc, fg_ref[...]) if final_norm else acc


def _ffn(h, g, wgu, wd, final_g, final_norm, tm=TOK_TILE):
    t, d = h.shape
    d_ff = wd.shape[0]
    assert d_ff % FF_CHUNK == 0
    tok = pl.BlockSpec((tm, d), lambda i: (i, 0))
    return pl.pallas_call(
        functools.partial(_ffn_kernel, d_ff=d_ff, final_norm=final_norm),
        grid=(t // tm,),
        in_specs=[tok, _const_spec((1, d)), _const_spec(wgu.shape), _const_spec(wd.shape),
                  _const_spec((1, d))],
        out_specs=tok,
        out_shape=jax.ShapeDtypeStruct((t, d), F32),
        compiler_params=_params("arbitrary"),
        name="ffn",
    )(h, g.reshape(1, -1), wgu, wd, final_g.reshape(1, -1))


def _channel_block(h, mem, layer, bsz, seq, xa_norm, xa_mem_norm, xa_wq, xa_wkv, xa_wo,
                   ffn_norm, ffn_w_gate_up, ffn_w_down, final_norm, last):
    d = h.shape[-1]
    mk, mv = _mem_kv(mem, xa_mem_norm[layer], xa_wkv[layer].astype(BF16))
    h = _xattn(h.reshape(bsz, seq, d), xa_norm[layer], xa_wq[layer].astype(BF16), mk, mv,
               xa_wo[layer].astype(BF16)).reshape(bsz * seq, d)
    return _ffn(h, ffn_norm[layer], ffn_w_gate_up[layer].astype(BF16),
                ffn_w_down[layer].astype(BF16), final_norm, last)


def kernel(x, mem, mix_norm, ab_w_in, lru_conv_w, lru_conv_b, lru_wa, lru_ba, lru_wx, lru_bx, lru_lambda, ab_w_out, c_w_qkv, c_b_qkv, c_sinks, c_w_out, c_b_out, xa_norm, xa_mem_norm, xa_wq, xa_wkv, xa_wo, ffn_norm, ffn_w_gate_up, ffn_w_down, final_norm):
    bsz, seq, d = x.shape
    t = bsz * seq
    tables = _rope_tables(seq)
    channel = functools.partial(
        _channel_block, mem=mem, bsz=bsz, seq=seq, xa_norm=xa_norm, xa_mem_norm=xa_mem_norm,
        xa_wq=xa_wq, xa_wkv=xa_wkv, xa_wo=xa_wo, ffn_norm=ffn_norm,
        ffn_w_gate_up=ffn_w_gate_up, ffn_w_down=ffn_w_down, final_norm=final_norm)

    rec, q, k, v = _lru_proj(
        x, mix_norm[0], ab_w_in[0].astype(BF16), lru_conv_w[0], lru_conv_b[0],
        lru_wa[0].astype(BF16), lru_ba[0], lru_wx[0].astype(BF16), lru_bx[0], lru_lambda[0],
        tables)
    att0 = _dilated_attention(q, k, v)
    h = _ab_out(x.reshape(t, d), rec.reshape(t, LRU_WIDTH), att0.reshape(t, B_WIDTH),
                ab_w_out[0].astype(BF16))
    h = channel(h, layer=0, last=False)

    w_qkv, b_qkv = c_w_qkv[0], c_b_qkv[0]

    def dup_heads(a):
        parts = a.reshape(a.shape[0], C_KV_HEADS, 1, HEAD_DIM)
        return jnp.broadcast_to(parts, (a.shape[0], C_KV_HEADS, 2, HEAD_DIM)).reshape(
            a.shape[0], 2 * C_KV_WIDTH)

    def widen(a):
        qq = a[:, :C_Q_WIDTH]
        kk = a[:, C_Q_WIDTH:C_Q_WIDTH + C_KV_WIDTH]
        vv = a[:, C_Q_WIDTH + C_KV_WIDTH:]
        return jnp.concatenate([qq, dup_heads(kk), dup_heads(vv)], axis=1)

    q1, k1, v1 = _swa_proj(h.reshape(bsz, seq, d), mix_norm[1], widen(w_qkv).astype(BF16),
                           widen(b_qkv.reshape(1, -1))[0], tables)
    att = _swa_attention(q1, k1, v1, c_sinks[0])
    h = _proj_out(h, att.reshape(t, C_Q_WIDTH), c_w_out[0].astype(BF16), c_b_out[0])
    h = channel(h, layer=1, last=True)
    return h.reshape(bsz, seq, d)
```

```python
import functools
import math

import jax
import jax.numpy as jnp
from jax import lax
from jax.experimental import pallas as pl
from jax.experimental.pallas import tpu as pltpu

F32 = jnp.float32
BF16 = jnp.bfloat16

D_MODEL = 1024
N_MEM = 256
NORM_EPS = 1e-6
ROPE_THETA = 500000.0
HEAD_DIM = 64
ROT_DIM = HEAD_DIM // 4
ROT_HALF = ROT_DIM // 2
LANES = 128
SUBLANES = 8
BLOCK = 128
LRU_WIDTH = D_MODEL
LRU_HEADS = 4
LRU_HEAD_DIM = LRU_WIDTH // LRU_HEADS
CONV_WIDTH = 4
LRU_C = 8.0
B_HEADS = 8
B_WIDTH = B_HEADS * HEAD_DIM
DILATED_PATTERN = ((128, 1), (512, 4), (2048, 16))
C_HEADS = 16
C_KV_HEADS = 2
C_WINDOW = 128
C_Q_WIDTH = C_HEADS * HEAD_DIM
C_KV_WIDTH = C_KV_HEADS * HEAD_DIM
XA_HEADS = 4
XA_HEAD_DIM = 128
XA_WIDTH = XA_HEADS * XA_HEAD_DIM
NEG = -1e30
VMEM_LIMIT = 56 * 1024 * 1024

TOK_TILE = 512
FF_CHUNK = 1408


def _dot(a, b):
    return jnp.dot(a, b, preferred_element_type=F32)


def _dot_nt(a, b):
    return lax.dot_general(a, b, (((1,), (1,)), ((), ())), preferred_element_type=F32)


def _rms(x, g):
    return x * lax.rsqrt(jnp.mean(x * x, axis=-1, keepdims=True) + NORM_EPS) * g


def _params(*sem):
    return pltpu.CompilerParams(dimension_semantics=sem, vmem_limit_bytes=VMEM_LIMIT)


def _const_spec(shape):
    nd = len(shape)
    return pl.BlockSpec(shape, lambda *_: (0,) * nd, pipeline_mode=pl.Buffered(1))


def _rope(x, c, s1, s2):
    reps = x.shape[1] // LANES
    c, s1, s2 = (jnp.concatenate([t] * reps, axis=1) for t in (c, s1, s2))
    width = x.shape[1]
    return x * c + pltpu.roll(x, ROT_HALF, 1) * s1 + pltpu.roll(x, width - ROT_HALF, 1) * s2


def _rope_tables(seq):
    inv = ROPE_THETA ** (-jnp.arange(0, ROT_DIM, 2, dtype=F32) / ROT_DIM)
    ang = jnp.arange(seq, dtype=F32)[:, None] * inv[None, :]
    cos, sin = jnp.cos(ang), jnp.sin(ang)
    zeros = jnp.zeros((seq, HEAD_DIM - ROT_DIM), F32)
    zh = jnp.zeros((seq, ROT_HALF), F32)
    c = jnp.concatenate([cos, cos, zeros + 1.0], axis=1)
    s1 = jnp.concatenate([zh, sin, zeros], axis=1)
    s2 = jnp.concatenate([-sin, zh, zeros], axis=1)
    return tuple(jnp.concatenate([t, t], axis=1) for t in (c, s1, s2))


def _lru_proj_kernel(h_ref, g_ref, w_ref, cw_ref, cb_ref, wa_ref, ba_ref, wx_ref, bx_ref,
                     lam_ref, rc_ref, rs1_ref, rs2_ref,
                     rec_ref, q_ref, k_ref, v_ref,
                     xs_ref, as_ref, us_ref, hs_ref, hprev_ref, *, tl):
    pad = SUBLANES

    @pl.when(pl.program_id(1) == 0)
    def _():
        xs_ref[0:pad, :] = jnp.zeros((pad, LRU_WIDTH), F32)
        hprev_ref[...] = jnp.zeros_like(hprev_ref)

    as_ref[0:pad, :] = jnp.ones((pad, LRU_WIDTH), F32)
    us_ref[0:pad, :] = jnp.zeros((pad, LRU_WIDTH), F32)

    hn = _rms(h_ref[...], g_ref[...]).astype(BF16)

    x = _dot(hn, w_ref[:, 0:LRU_WIDTH])
    xs_ref[pad:pad + tl, :] = x
    xc = cb_ref[...] + xs_ref[pad - 3:pad - 3 + tl, :] * cw_ref[0:1, :]
    xc = xc + xs_ref[pad - 2:pad - 2 + tl, :] * cw_ref[1:2, :]
    xc = xc + xs_ref[pad - 1:pad - 1 + tl, :] * cw_ref[2:3, :]
    xc = xc + x * cw_ref[3:4, :]
    xs_ref[0:pad, :] = xs_ref[tl:tl + pad, :]

    xcb = xc.astype(BF16)
    r_parts, i_parts = [], []
    for hd in range(LRU_HEADS):
        sl = slice(hd * LRU_HEAD_DIM, (hd + 1) * LRU_HEAD_DIM)
        r_parts.append(_dot(xcb[:, sl], wa_ref[hd]))
        i_parts.append(_dot(xcb[:, sl], wx_ref[hd]))
    r = jax.nn.sigmoid(jnp.concatenate(r_parts, axis=1) + ba_ref[...])
    ig = jax.nn.sigmoid(jnp.concatenate(i_parts, axis=1) + bx_ref[...])

    nlam = -lam_ref[...]
    softplus = jnp.maximum(nlam, 0.0) + jnp.log1p(jnp.exp(-jnp.abs(nlam)))
    log_a = (-LRU_C) * r * softplus
    a = jnp.exp(log_a)
    u = jnp.sqrt(jnp.tanh(-log_a) * (1.0 + a * a)) * (ig * xc)

    as_ref[pad:pad + tl, :] = a
    us_ref[pad:pad + tl, :] = u
    for s in (1, 2, 4):
        a_cur = as_ref[pad:pad + tl, :]
        u_cur = us_ref[pad:pad + tl, :]
        a_sh = as_ref[pad - s:pad - s + tl, :]
        u_sh = us_ref[pad - s:pad - s + tl, :]
        us_ref[pad:pad + tl, :] = a_cur * u_sh + u_cur
        as_ref[pad:pad + tl, :] = a_cur * a_sh

    def chain(g, hcar):
        r0 = pl.multiple_of(pad + g * SUBLANES, SUBLANES)
        hcar = as_ref[pl.ds(r0, SUBLANES), :] * hcar + us_ref[pl.ds(r0, SUBLANES), :]
        hs_ref[pl.ds(pl.multiple_of(g * SUBLANES, SUBLANES), SUBLANES), :] = hcar
        return hcar

    h0 = jnp.broadcast_to(hprev_ref[SUBLANES - 1:SUBLANES, :], (SUBLANES, LRU_WIDTH))
    hprev_ref[...] = lax.fori_loop(0, tl // SUBLANES, chain, h0, unroll=8)

    y = _dot(hn, w_ref[:, LRU_WIDTH:2 * LRU_WIDTH])
    gelu = 0.5 * y * (1.0 + jnp.tanh(math.sqrt(2.0 / math.pi) * (y + 0.044715 * (y * y * y))))
    rec_ref[...] = (hs_ref[...] * gelu).astype(BF16)

    c, s1, s2 = rc_ref[...], rs1_ref[...], rs2_ref[...]
    o = 2 * LRU_WIDTH
    q = _dot(hn, w_ref[:, o:o + B_WIDTH])
    q_ref[...] = (_rope(q, c, s1, s2) * (HEAD_DIM ** -0.5)).astype(BF16)
    k = _dot(hn, w_ref[:, o + B_WIDTH:o + 2 * B_WIDTH])
    k_ref[...] = _rope(k, c, s1, s2).astype(BF16)
    v_ref[...] = _dot(hn, w_ref[:, o + 2 * B_WIDTH:o + 3 * B_WIDTH]).astype(BF16)


def _lru_proj(h, g, w_in, conv_w, conv_b, wa, ba, wx, bx, lam, tables, tl=TOK_TILE):
    bsz, seq, d = h.shape
    row = lambda a: a.reshape(1, -1)
    tok = lambda w: pl.BlockSpec((None, tl, w), lambda b, i: (b, i, 0))
    tab = pl.BlockSpec((tl, LANES), lambda b, i: (i, 0))
    out_shape = (jax.ShapeDtypeStruct((bsz, seq, LRU_WIDTH), BF16),) + tuple(
        jax.ShapeDtypeStruct((bsz, seq, B_WIDTH), BF16) for _ in range(3))
    return pl.pallas_call(
        functools.partial(_lru_proj_kernel, tl=tl),
        grid=(bsz, seq // tl),
        in_specs=[tok(d), _const_spec((1, d)), _const_spec(w_in.shape),
                  _const_spec(conv_w.shape), _const_spec((1, LRU_WIDTH)),
                  _const_spec(wa.shape), _const_spec((1, LRU_WIDTH)),
                  _const_spec(wx.shape), _const_spec((1, LRU_WIDTH)),
                  _const_spec((1, LRU_WIDTH)), tab, tab, tab],
        out_specs=(tok(LRU_WIDTH), tok(B_WIDTH), tok(B_WIDTH), tok(B_WIDTH)),
        out_shape=out_shape,
        scratch_shapes=[pltpu.VMEM((tl + SUBLANES, LRU_WIDTH), F32),
                        pltpu.VMEM((tl + SUBLANES, LRU_WIDTH), F32),
                        pltpu.VMEM((tl + SUBLANES, LRU_WIDTH), F32),
                        pltpu.VMEM((tl, LRU_WIDTH), F32),
                        pltpu.VMEM((SUBLANES, LRU_WIDTH), F32)],
        compiler_params=_params("arbitrary", "arbitrary"),
        name="lru_proj",
    )(h, row(g), w_in, conv_w, row(conv_b), wa, row(ba), wx, row(bx), row(lam), *tables)


def _sink_column(first):
    return (2 * BLOCK - 1) * (1 - first)


def _fill_band_bias(bias_ref, max_dist, reps, sink_of_rep=None):
    rows = lax.broadcasted_iota(jnp.int32, (BLOCK, 2 * BLOCK), 0)
    cols = lax.broadcasted_iota(jnp.int32, (BLOCK, 2 * BLOCK), 1)
    for first, off in ((0, 0), (1, BLOCK)):
        dist = off + rows - cols
        valid = jnp.logical_and(dist >= 0, dist <= max_dist)
        tile = jnp.where(valid, 0.0, NEG).astype(F32)
        for rep in range(reps):
            rep_tile = tile
            if sink_of_rep is not None:
                assert max_dist < BLOCK
                rep_tile = jnp.where(cols == _sink_column(first), sink_of_rep(rep), tile)
            bias_ref[first, rep * BLOCK:(rep + 1) * BLOCK, :] = rep_tile


def _band_attend(qb, kb, vb, bias, want_lse):
    r = qb.shape[0]
    lane_q = lax.broadcasted_iota(jnp.int32, qb.shape, 1)
    zero = jnp.zeros((), qb.dtype)
    qz = jnp.concatenate([jnp.where(lane_q < HEAD_DIM, qb, zero),
                          jnp.where(lane_q >= HEAD_DIM, qb, zero)], axis=0)
    s = _dot_nt(qz, kb) + bias
    m = jnp.max(s, axis=-1, keepdims=True)
    e = jnp.exp(s - m).astype(vb.dtype)
    ob = _dot(e, jnp.concatenate([vb, jnp.ones_like(vb)], axis=1))
    first_head = lax.broadcasted_iota(jnp.int32, (r, LANES), 1) < HEAD_DIM
    num = jnp.where(first_head, ob[:r, :LANES], ob[r:, :LANES])
    den = jnp.where(first_head, ob[:r, LANES:], ob[r:, LANES:])
    o = num * (1.0 / den)
    if not want_lse:
        return o, None
    return o, jnp.where(first_head, m[:r], m[r:]) + jnp.log(den)


def _band_window(n, nblk):
    first = jnp.minimum(lax.rem(n, nblk), 1)
    q0 = pl.multiple_of(n * BLOCK, BLOCK)
    ks = pl.multiple_of(q0 - first * BLOCK, BLOCK)
    return q0, ks, first


def _dilated_kernel(q_ref, k_ref, v_ref, out_ref, stage_ref, p4_ref, perm_ref, o_ref, l_ref,
                    bias_ref, *, seq, max_dist):
    _fill_band_bias(bias_ref, max_dist, 2)

    l4, l16 = seq // 4, seq // 16
    for a, x_ref in enumerate((q_ref, k_ref, v_ref)):
        stage_ref[...] = x_ref[...].astype(F32)
        for r in range(4):
            blk = stage_ref[pl.ds(r, l4, stride=4), :]
            p4_ref[r * l4:(r + 1) * l4, :] = blk
            perm_ref[0, a, r * l4:(r + 1) * l4, :] = blk.astype(BF16)
        for r in range(4):
            for rb in range(4):
                blk = p4_ref[pl.ds(r * l4 + rb, l16, stride=4), :]
                res = r + 4 * rb
                perm_ref[1, a, res * l16:(res + 1) * l16, :] = blk.astype(BF16)

    branches = ((1, (q_ref, k_ref, v_ref)),
                (4, tuple(perm_ref.at[0, a] for a in range(3))),
                (16, tuple(perm_ref.at[1, a] for a in range(3))))
    for bi, (d, (qr, kr, vr)) in enumerate(branches):
        nblk = seq // d // BLOCK

        def body(n, carry, d=d, bi=bi, qr=qr, kr=kr, vr=vr, nblk=nblk):
            q0, ks, first = _band_window(n, nblk)
            o, lse = _band_attend(qr[pl.ds(q0, BLOCK), :], kr[pl.ds(ks, 2 * BLOCK), :],
                                  vr[pl.ds(ks, 2 * BLOCK), :], bias_ref[first], True)
            if d == 1:
                rows = pl.ds(q0, BLOCK)
            else:
                rows = pl.ds(lax.rem(n, nblk) * (BLOCK * d) + n // nblk, BLOCK, stride=d)
            o_ref.at[bi][rows, :] = o
            l_ref.at[bi][rows, :] = lse
            return carry

        lax.fori_loop(0, seq // BLOCK, body, 0, unroll=8)

    chunk = 4 * BLOCK

    def merge(c, carry):
        rows = pl.ds(pl.multiple_of(c * chunk, chunk), chunk)
        ls = [l_ref[bi, rows, :] for bi in range(3)]
        m = jnp.maximum(jnp.maximum(ls[0], ls[1]), ls[2])
        ws = [jnp.exp(l - m) for l in ls]
        acc = ws[0] * o_ref[0, rows, :] + ws[1] * o_ref[1, rows, :] + ws[2] * o_ref[2, rows, :]
        out_ref[rows, :] = (acc * (1.0 / (ws[0] + ws[1] + ws[2]))).astype(out_ref.dtype)
        return carry

    lax.fori_loop(0, seq // chunk, merge, 0)


def _dilated_attention(q, k, v):
    assert tuple(d for _, d in DILATED_PATTERN) == (1, 4, 16)
    assert len({w // d for w, d in DILATED_PATTERN}) == 1
    max_dist = DILATED_PATTERN[0][0] // DILATED_PATTERN[0][1]
    bsz, seq, w = q.shape
    spec = pl.BlockSpec((None, seq, LANES), lambda b, c: (b, 0, c))
    return pl.pallas_call(
        functools.partial(_dilated_kernel, seq=seq, max_dist=max_dist),
        grid=(bsz, w // LANES),
        in_specs=[spec, spec, spec],
        out_specs=spec,
        out_shape=jax.ShapeDtypeStruct((bsz, seq, w), BF16),
        scratch_shapes=[pltpu.VMEM((seq, LANES), F32), pltpu.VMEM((seq, LANES), F32),
                        pltpu.VMEM((2, 3, seq, LANES), BF16),
                        pltpu.VMEM((3, seq, LANES), F32), pltpu.VMEM((3, seq, LANES), F32),
                        pltpu.VMEM((2, 2 * BLOCK, 2 * BLOCK), F32)],
        compiler_params=_params("arbitrary", "arbitrary"),
        name="dilated",
    )(q, k, v)


def _swa_kernel(sink_ref, q_ref, k_ref, v_ref, o_ref, bias_ref, *, nblk, max_dist, pairs):
    head0 = pl.program_id(1) * (2 * pairs)
    _fill_band_bias(bias_ref, max_dist, 2 * pairs,
                    lambda rep: sink_ref[head0 + 2 * (rep % pairs) + rep // pairs])
    key_row = lax.broadcasted_iota(jnp.int32, (2 * BLOCK, LANES), 0)

    def body(i, carry):
        q0, ks, first = _band_window(i, nblk)
        qrow = q_ref[pl.ds(q0, BLOCK), :]
        qb = jnp.concatenate([qrow[:, p * LANES:(p + 1) * LANES] for p in range(pairs)], axis=0)
        unseen = key_row == _sink_column(first)
        zero = jnp.zeros((), k_ref.dtype)
        kb = jnp.where(unseen, zero, k_ref[pl.ds(ks, 2 * BLOCK), :])
        vb = jnp.where(unseen, zero, v_ref[pl.ds(ks, 2 * BLOCK), :])
        o, _ = _band_attend(qb, kb, vb, bias_ref[first], False)
        o_ref[pl.ds(q0, BLOCK), :] = jnp.concatenate(
            [o[p * BLOCK:(p + 1) * BLOCK, :] for p in range(pairs)], axis=1).astype(o_ref.dtype)
        return carry

    lax.fori_loop(0, nblk, body, 0, unroll=2)


def _swa_attention(q, kdup, vdup, sinks):
    bsz, seq, w = q.shape
    pairs = w // LANES // C_KV_HEADS
    qspec = pl.BlockSpec((None, seq, pairs * LANES), lambda b, g: (b, 0, g))
    kvspec = pl.BlockSpec((None, seq, LANES), lambda b, g: (b, 0, g))
    return pl.pallas_call(
        functools.partial(_swa_kernel, nblk=seq // BLOCK, max_dist=C_WINDOW - 1, pairs=pairs),
        grid=(bsz, C_KV_HEADS),
        in_specs=[pl.BlockSpec(memory_space=pltpu.SMEM), qspec, kvspec, kvspec],
        out_specs=qspec,
        out_shape=jax.ShapeDtypeStruct((bsz, seq, w), BF16),
        scratch_shapes=[pltpu.VMEM((2, 2 * pairs * BLOCK, 2 * BLOCK), F32)],
        compiler_params=_params("arbitrary", "arbitrary"),
        name="swa_sink",
    )(sinks, q, kdup, vdup)


def _ab_out_kernel(h_ref, rec_ref, att_ref, w_ref, out_ref):
    out = _dot(rec_ref[...], w_ref[0:LRU_WIDTH, :])
    out = out + _dot(att_ref[...], w_ref[LRU_WIDTH:LRU_WIDTH + B_WIDTH, :])
    out_ref[...] = h_ref[...] + out


def _ab_out(h, rec, att, w_out, tm=TOK_TILE):
    t, d = h.shape
    tok = lambda w: pl.BlockSpec((tm, w), lambda i: (i, 0))
    return pl.pallas_call(
        _ab_out_kernel,
        grid=(t // tm,),
        in_specs=[tok(d), tok(LRU_WIDTH), tok(B_WIDTH), _const_spec(w_out.shape)],
        out_specs=tok(d),
        out_shape=jax.ShapeDtypeStruct((t, d), F32),
        compiler_params=_params("arbitrary"),
        name="ab_out",
    )(h, rec, att, w_out)


def _swa_proj_kernel(h_ref, g_ref, w_ref, b_ref, rc_ref, rs1_ref, rs2_ref, q_ref, k_ref, v_ref):
    hn = _rms(h_ref[...], g_ref[...]).astype(BF16)
    c, s1, s2 = rc_ref[...], rs1_ref[...], rs2_ref[...]
    kw = 2 * C_KV_WIDTH
    q = _dot(hn, w_ref[:, 0:C_Q_WIDTH]) + b_ref[:, 0:C_Q_WIDTH]
    q_ref[...] = (_rope(q, c, s1, s2) * (HEAD_DIM ** -0.5)).astype(BF16)
    k = _dot(hn, w_ref[:, C_Q_WIDTH:C_Q_WIDTH + kw]) + b_ref[:, C_Q_WIDTH:C_Q_WIDTH + kw]
    k_ref[...] = _rope(k, c, s1, s2).astype(BF16)
    o = C_Q_WIDTH + kw
    v_ref[...] = (_dot(hn, w_ref[:, o:o + kw]) + b_ref[:, o:o + kw]).astype(BF16)


def _swa_proj(h, g, w, b, tables, tm=TOK_TILE):
    bsz, seq, d = h.shape
    kw = 2 * C_KV_WIDTH
    tok = lambda width: pl.BlockSpec((None, tm, width), lambda bi, i: (bi, i, 0))
    tab = pl.BlockSpec((tm, LANES), lambda bi, i: (i, 0))
    return pl.pallas_call(
        _swa_proj_kernel,
        grid=(bsz, seq // tm),
        in_specs=[tok(d), _const_spec((1, d)), _const_spec(w.shape), _const_spec((1, w.shape[1])),
                  tab, tab, tab],
        out_specs=(tok(C_Q_WIDTH), tok(kw), tok(kw)),
        out_shape=(jax.ShapeDtypeStruct((bsz, seq, C_Q_WIDTH), BF16),
                   jax.ShapeDtypeStruct((bsz, seq, kw), BF16),
                   jax.ShapeDtypeStruct((bsz, seq, kw), BF16)),
        compiler_params=_params("arbitrary", "arbitrary"),
        name="swa_proj",
    )(h, g.reshape(1, -1), w, b.reshape(1, -1), *tables)


def _proj_out_kernel(h_ref, a_ref, w_ref, b_ref, out_ref):
    out_ref[...] = h_ref[...] + _dot(a_ref[...], w_ref[...]) + b_ref[...]


def _proj_out(h, a, w, b, tm=TOK_TILE):
    t, d = h.shape
    return pl.pallas_call(
        _proj_out_kernel,
        grid=(t // tm,),
        in_specs=[pl.BlockSpec((tm, d), lambda i: (i, 0)),
                  pl.BlockSpec((tm, a.shape[1]), lambda i: (i, 0)),
                  _const_spec(w.shape), _const_spec((1, d))],
        out_specs=pl.BlockSpec((tm, d), lambda i: (i, 0)),
        out_shape=jax.ShapeDtypeStruct((t, d), F32),
        compiler_params=_params("arbitrary"),
        name="proj_out",
    )(h, a, w, b.reshape(1, -1))


def _mem_kv_kernel(mem_ref, g_ref, w_ref, k_ref, v_ref):
    mn = _rms(mem_ref[...], g_ref[...]).astype(BF16)
    k_ref[...] = _dot(mn, w_ref[:, 0:XA_WIDTH]).astype(BF16)
    v_ref[...] = _dot(mn, w_ref[:, XA_WIDTH:2 * XA_WIDTH]).astype(BF16)


def _mem_kv(mem, g, wkv):
    bsz, m, d = mem.shape
    spec = pl.BlockSpec((None, m, XA_WIDTH), lambda b: (b, 0, 0))
    return pl.pallas_call(
        _mem_kv_kernel,
        grid=(bsz,),
        in_specs=[pl.BlockSpec((None, m, d), lambda b: (b, 0, 0)), _const_spec((1, d)),
                  _const_spec(wkv.shape)],
        out_specs=(spec, spec),
        out_shape=(jax.ShapeDtypeStruct((bsz, m, XA_WIDTH), BF16),) * 2,
        compiler_params=_params("arbitrary"),
        name="mem_kv",
    )(mem, g.reshape(1, -1), wkv)


def _xattn_kernel(h_ref, g_ref, wq_ref, k_ref, v_ref, wo_ref, out_ref):
    h = h_ref[...]
    q = _dot(_rms(h, g_ref[...]).astype(BF16), wq_ref[...]).astype(BF16)
    scale = XA_HEAD_DIM ** -0.5
    heads = []
    for hd in range(XA_HEADS):
        sl = slice(hd * XA_HEAD_DIM, (hd + 1) * XA_HEAD_DIM)
        s = _dot_nt(q[:, sl], k_ref[:, sl]) * scale
        e = jnp.exp(s - jnp.max(s, axis=-1, keepdims=True))
        den = jnp.sum(e, axis=-1, keepdims=True)
        heads.append((_dot(e.astype(BF16), v_ref[:, sl]) * (1.0 / den)).astype(BF16))
    out_ref[...] = h + _dot(jnp.concatenate(heads, axis=1), wo_ref[...])


def _xattn(h, g, wq, k, v, wo, tm=TOK_TILE):
    bsz, seq, d = h.shape
    tok = pl.BlockSpec((None, tm, d), lambda b, i: (b, i, 0))
    kv = pl.BlockSpec((None, N_MEM, XA_WIDTH), lambda b, i: (b, 0, 0))
    return pl.pallas_call(
        _xattn_kernel,
        grid=(bsz, seq // tm),
        in_specs=[tok, _const_spec((1, d)), _const_spec(wq.shape), kv, kv, _const_spec(wo.shape)],
        out_specs=tok,
        out_shape=jax.ShapeDtypeStruct((bsz, seq, d), F32),
        compiler_params=_params("arbitrary", "arbitrary"),
        name="xattn",
    )(h, g.reshape(1, -1), wq, k, v, wo)


def _ffn_kernel(h_ref, g_ref, wgu_ref, wd_ref, fg_ref, out_ref, *, d_ff, final_norm):
    h = h_ref[...]
    hn = _rms(h, g_ref[...]).astype(BF16)
    acc = h
    for c0 in range(0, d_ff, FF_CHUNK):
        gate = _dot(hn, wgu_ref[:, c0:c0 + FF_CHUNK])
        up = _dot(hn, wgu_ref[:, d_ff + c0:d_ff + c0 + FF_CHUNK])
        act = (gate * jax.nn.sigmoid(gate) * up).astype(BF16)
        acc = acc + _dot(act, wd_ref[c0:c0 + FF_CHUNK, :])
    out_ref[...] = _rms(acc, fg_ref[...]) if final_norm else acc


def _ffn(h, g, wgu, wd, final_g, final_norm, tm=TOK_TILE):
    t, d = h.shape
    d_ff = wd.shape[0]
    assert d_ff % FF_CHUNK == 0
    tok = pl.BlockSpec((tm, d), lambda i: (i, 0))
    return pl.pallas_call(
        functools.partial(_ffn_kernel, d_ff=d_ff, final_norm=final_norm),
        grid=(t // tm,),
        in_specs=[tok, _const_spec((1, d)), _const_spec(wgu.shape), _const_spec(wd.shape),
                  _const_spec((1, d))],
        out_specs=tok,
        out_shape=jax.ShapeDtypeStruct((t, d), F32),
        compiler_params=_params("arbitrary"),
        name="ffn",
    )(h, g.reshape(1, -1), wgu, wd, final_g.reshape(1, -1))


def _channel_block(h, mem, layer, bsz, seq, xa_norm, xa_mem_norm, xa_wq, xa_wkv, xa_wo,
                   ffn_norm, ffn_w_gate_up, ffn_w_down, final_norm, last):
    d = h.shape[-1]
    mk, mv = _mem_kv(mem, xa_mem_norm[layer], xa_wkv[layer].astype(BF16))
    h = _xattn(h.reshape(bsz, seq, d), xa_norm[layer], xa_wq[layer].astype(BF16), mk, mv,
               xa_wo[layer].astype(BF16)).reshape(bsz * seq, d)
    return _ffn(h, ffn_norm[layer], ffn_w_gate_up[layer].astype(BF16),
                ffn_w_down[layer].astype(BF16), final_norm, last)


def kernel(x, mem, mix_norm, ab_w_in, lru_conv_w, lru_conv_b, lru_wa, lru_ba, lru_wx, lru_bx, lru_lambda, ab_w_out, c_w_qkv, c_b_qkv, c_sinks, c_w_out, c_b_out, xa_norm, xa_mem_norm, xa_wq, xa_wkv, xa_wo, ffn_norm, ffn_w_gate_up, ffn_w_down, final_norm):
    bsz, seq, d = x.shape
    t = bsz * seq
    tables = _rope_tables(seq)
    channel = functools.partial(
        _channel_block, mem=mem, bsz=bsz, seq=seq, xa_norm=xa_norm, xa_mem_norm=xa_mem_norm,
        xa_wq=xa_wq, xa_wkv=xa_wkv, xa_wo=xa_wo, ffn_norm=ffn_norm,
        ffn_w_gate_up=ffn_w_gate_up, ffn_w_down=ffn_w_down, final_norm=final_norm)

    rec, q, k, v = _lru_proj(
        x, mix_norm[0], ab_w_in[0].astype(BF16), lru_conv_w[0], lru_conv_b[0],
        lru_wa[0].astype(BF16), lru_ba[0], lru_wx[0].astype(BF16), lru_bx[0], lru_lambda[0],
        tables)
    att0 = _dilated_attention(q, k, v)
    h = _ab_out(x.reshape(t, d), rec.reshape(t, LRU_WIDTH), att0.reshape(t, B_WIDTH),
                ab_w_out[0].astype(BF16))
    h = channel(h, layer=0, last=False)

    w_qkv, b_qkv = c_w_qkv[0], c_b_qkv[0]

    def dup_heads(a):
        parts = a.reshape(a.shape[0], C_KV_HEADS, 1, HEAD_DIM)
        return jnp.broadcast_to(parts, (a.shape[0], C_KV_HEADS, 2, HEAD_DIM)).reshape(
            a.shape[0], 2 * C_KV_WIDTH)

    def widen(a):
        qq = a[:, :C_Q_WIDTH]
        kk = a[:, C_Q_WIDTH:C_Q_WIDTH + C_KV_WIDTH]
        vv = a[:, C_Q_WIDTH + C_KV_WIDTH:]
        return jnp.concatenate([qq, dup_heads(kk), dup_heads(vv)], axis=1)

    q1, k1, v1 = _swa_proj(h.reshape(bsz, seq, d), mix_norm[1], widen(w_qkv).astype(BF16),
                           widen(b_qkv.reshape(1, -1))[0], tables)
    att = _swa_attention(q1, k1, v1, c_sinks[0])
    h = _proj_out(h, att.reshape(t, C_Q_WIDTH), c_w_out[0].astype(BF16), c_b_out[0])
    h = channel(h, layer=1, last=True)
    return h.reshape(bsz, seq, d)
```

```python
import functools
import math

import jax
import jax.numpy as jnp
from jax import lax
from jax.experimental import pallas as pl
from jax.experimental.pallas import tpu as pltpu

F32 = jnp.float32
BF16 = jnp.bfloat16

D_MODEL = 1024
N_MEM = 256
NORM_EPS = 1e-6
ROPE_THETA = 500000.0
HEAD_DIM = 64
ROT_DIM = HEAD_DIM // 4
ROT_HALF = ROT_DIM // 2
LANES = 128
SUBLANES = 8
BLOCK = 128
LRU_WIDTH = D_MODEL
LRU_HEADS = 4
LRU_HEAD_DIM = LRU_WIDTH // LRU_HEADS
CONV_WIDTH = 4
LRU_C = 8.0
B_HEADS = 8
B_WIDTH = B_HEADS * HEAD_DIM
DILATED_PATTERN = ((128, 1), (512, 4), (2048, 16))
C_HEADS = 16
C_KV_HEADS = 2
C_WINDOW = 128
C_Q_WIDTH = C_HEADS * HEAD_DIM
C_KV_WIDTH = C_KV_HEADS * HEAD_DIM
XA_HEADS = 4
XA_HEAD_DIM = 128
XA_WIDTH = XA_HEADS * XA_HEAD_DIM
NEG = -1e30
VMEM_LIMIT = 56 * 1024 * 1024

TOK_TILE = 512
FF_CHUNK = 1408


def _dot(a, b):
    return jnp.dot(a, b, preferred_element_type=F32)


def _dot_nt(a, b):
    return lax.dot_general(a, b, (((1,), (1,)), ((), ())), preferred_element_type=F32)


def _rms(x, g):
    return x * lax.rsqrt(jnp.mean(x * x, axis=-1, keepdims=True) + NORM_EPS) * g


def _params(*sem):
    return pltpu.CompilerParams(dimension_semantics=sem, vmem_limit_bytes=VMEM_LIMIT)


def _const_spec(shape):
    nd = len(shape)
    return pl.BlockSpec(shape, lambda *_: (0,) * nd, pipeline_mode=pl.Buffered(1))


def _layer_spec(shape, layer):
    nd = len(shape)
    return pl.BlockSpec((None,) + tuple(shape[1:]), lambda *_: (layer,) + (0,) * (nd - 1),
                        pipeline_mode=pl.Buffered(1))


def _rope(x, c, s1, s2):
    reps = x.shape[1] // LANES
    c, s1, s2 = (jnp.concatenate([t] * reps, axis=1) for t in (c, s1, s2))
    width = x.shape[1]
    return x * c + pltpu.roll(x, ROT_HALF, 1) * s1 + pltpu.roll(x, width - ROT_HALF, 1) * s2


def _rope_tables(seq):
    inv = ROPE_THETA ** (-jnp.arange(0, ROT_DIM, 2, dtype=F32) / ROT_DIM)
    ang = jnp.arange(seq, dtype=F32)[:, None] * inv[None, :]
    cos, sin = jnp.cos(ang), jnp.sin(ang)
    zeros = jnp.zeros((seq, HEAD_DIM - ROT_DIM), F32)
    zh = jnp.zeros((seq, ROT_HALF), F32)
    c = jnp.concatenate([cos, cos, zeros + 1.0], axis=1)
    s1 = jnp.concatenate([zh, sin, zeros], axis=1)
    s2 = jnp.concatenate([-sin, zh, zeros], axis=1)
    return tuple(jnp.concatenate([t, t], axis=1) for t in (c, s1, s2))


def _shift_rows(x, k, head):
    rolled = pltpu.roll(x, k, 0)
    sub = lax.broadcasted_iota(jnp.int32, head.shape, 0)
    first = jnp.where(sub < k, pltpu.roll(head, k, 0), rolled[0:SUBLANES])
    return jnp.concatenate([first, rolled[SUBLANES:]], axis=0)


def _lru_proj_kernel(h_ref, g_ref, w_ref, cw_ref, cb_ref, wa_ref, ba_ref, wx_ref, bx_ref,
                     lam_ref, rc_ref, rs1_ref, rs2_ref,
                     rec_ref, q_ref, k_ref, v_ref,
                     hn_ref, tail_ref, as_ref, us_ref, hs_ref, hprev_ref, *, tl):
    @pl.when(pl.program_id(1) == 0)
    def _():
        tail_ref[...] = jnp.zeros_like(tail_ref)
        hprev_ref[...] = jnp.zeros_like(hprev_ref)

    hn_ref[...] = _rms(h_ref[...], g_ref[...]).astype(BF16)
    hn = hn_ref[...]

    x = _dot(hn, w_ref[:, 0:LRU_WIDTH])
    tail = tail_ref[...]
    xc = cb_ref[...] + _shift_rows(x, 3, tail) * cw_ref[0:1, :]
    xc = xc + _shift_rows(x, 2, tail) * cw_ref[1:2, :]
    xc = xc + _shift_rows(x, 1, tail) * cw_ref[2:3, :]
    xc = xc + x * cw_ref[3:4, :]
    tail_ref[...] = x[tl - SUBLANES:tl, :]

    xcb = xc.astype(BF16)
    r_parts, i_parts = [], []
    for hd in range(LRU_HEADS):
        sl = slice(hd * LRU_HEAD_DIM, (hd + 1) * LRU_HEAD_DIM)
        r_parts.append(_dot(xcb[:, sl], wa_ref[hd]))
        i_parts.append(_dot(xcb[:, sl], wx_ref[hd]))
    r = jax.nn.sigmoid(jnp.concatenate(r_parts, axis=1) + ba_ref[...])
    ig = jax.nn.sigmoid(jnp.concatenate(i_parts, axis=1) + bx_ref[...])

    nlam = -lam_ref[...]
    softplus = jnp.maximum(nlam, 0.0) + jnp.log1p(jnp.exp(-jnp.abs(nlam)))
    log_a = (-LRU_C) * r * softplus
    a = jnp.exp(log_a)
    u = jnp.sqrt(jnp.tanh(-log_a) * (1.0 + a * a)) * (ig * xc)

    ones = jnp.ones((SUBLANES, LRU_WIDTH), F32)
    zeros = jnp.zeros((SUBLANES, LRU_WIDTH), F32)
    for s in (1, 2, 4):
        u = a * _shift_rows(u, s, zeros) + u
        a = a * _shift_rows(a, s, ones)
    as_ref[...] = a
    us_ref[...] = u

    def chain(g, hcar):
        rows = pl.ds(pl.multiple_of(g * SUBLANES, SUBLANES), SUBLANES)
        hcar = as_ref[rows, :] * hcar + us_ref[rows, :]
        hs_ref[rows, :] = hcar
        return hcar

    h0 = jnp.broadcast_to(hprev_ref[SUBLANES - 1:SUBLANES, :], (SUBLANES, LRU_WIDTH))
    hprev_ref[...] = lax.fori_loop(0, tl // SUBLANES, chain, h0, unroll=8)

    y = _dot(hn, w_ref[:, LRU_WIDTH:2 * LRU_WIDTH])
    gelu = 0.5 * y * (1.0 + jnp.tanh(math.sqrt(2.0 / math.pi) * (y + 0.044715 * (y * y * y))))
    rec_ref[...] = (hs_ref[...] * gelu).astype(BF16)

    c, s1, s2 = rc_ref[...], rs1_ref[...], rs2_ref[...]
    o = 2 * LRU_WIDTH
    q = _dot(hn, w_ref[:, o:o + B_WIDTH])
    q_ref[...] = (_rope(q, c, s1, s2) * (HEAD_DIM ** -0.5)).astype(BF16)
    k = _dot(hn, w_ref[:, o + B_WIDTH:o + 2 * B_WIDTH])
    k_ref[...] = _rope(k, c, s1, s2).astype(BF16)
    v_ref[...] = _dot(hn, w_ref[:, o + 2 * B_WIDTH:o + 3 * B_WIDTH]).astype(BF16)


def _lru_proj(h, g, w_in, conv_w, conv_b, wa, ba, wx, bx, lam, tables, tl=TOK_TILE):
    bsz, seq, d = h.shape
    tok = lambda w: pl.BlockSpec((None, tl, w), lambda b, i: (b, i, 0))
    tab = pl.BlockSpec((tl, LANES), lambda b, i: (i, 0))
    out_shape = (jax.ShapeDtypeStruct((bsz, seq, LRU_WIDTH), BF16),) + tuple(
        jax.ShapeDtypeStruct((bsz, seq, B_WIDTH), BF16) for _ in range(3))
    params = (g, w_in, conv_w, conv_b, wa, ba, wx, bx, lam)
    return pl.pallas_call(
        functools.partial(_lru_proj_kernel, tl=tl),
        grid=(bsz, seq // tl),
        in_specs=[tok(d)] + [_layer_spec(a.shape, 0) for a in params] + [tab, tab, tab],
        out_specs=(tok(LRU_WIDTH), tok(B_WIDTH), tok(B_WIDTH), tok(B_WIDTH)),
        out_shape=out_shape,
        scratch_shapes=[pltpu.VMEM((tl, d), BF16),
                        pltpu.VMEM((SUBLANES, LRU_WIDTH), F32),
                        pltpu.VMEM((tl, LRU_WIDTH), F32),
                        pltpu.VMEM((tl, LRU_WIDTH), F32),
                        pltpu.VMEM((tl, LRU_WIDTH), F32),
                        pltpu.VMEM((SUBLANES, LRU_WIDTH), F32)],
        compiler_params=_params("arbitrary", "arbitrary"),
        name="lru_proj",
    )(h, *params, *tables)


def _sink_column(first):
    return (2 * BLOCK - 1) * (1 - first)


def _fill_band_bias(bias_ref, max_dist, reps, sink_of_rep=None):
    rows = lax.broadcasted_iota(jnp.int32, (BLOCK, 2 * BLOCK), 0)
    cols = lax.broadcasted_iota(jnp.int32, (BLOCK, 2 * BLOCK), 1)
    for first, off in ((0, 0), (1, BLOCK)):
        dist = off + rows - cols
        valid = jnp.logical_and(dist >= 0, dist <= max_dist)
        tile = jnp.where(valid, 0.0, NEG).astype(F32)
        for rep in range(reps):
            rep_tile = tile
            if sink_of_rep is not None:
                assert max_dist < BLOCK
                rep_tile = jnp.where(cols == _sink_column(first), sink_of_rep(rep), tile)
            bias_ref[first, rep * BLOCK:(rep + 1) * BLOCK, :] = rep_tile


def _band_attend(qb, kb, vb, bias, want_lse):
    r = qb.shape[0]
    lane_q = lax.broadcasted_iota(jnp.int32, qb.shape, 1)
    zero = jnp.zeros((), qb.dtype)
    qz = jnp.concatenate([jnp.where(lane_q < HEAD_DIM, qb, zero),
                          jnp.where(lane_q >= HEAD_DIM, qb, zero)], axis=0)
    s = _dot_nt(qz, kb) + bias
    m = jnp.max(s, axis=-1, keepdims=True)
    e = jnp.exp(s - m).astype(vb.dtype)
    ob = _dot(e, jnp.concatenate([vb, jnp.ones_like(vb)], axis=1))
    first_head = lax.broadcasted_iota(jnp.int32, (r, LANES), 1) < HEAD_DIM
    num = jnp.where(first_head, ob[:r, :LANES], ob[r:, :LANES])
    den = jnp.where(first_head, ob[:r, LANES:], ob[r:, LANES:])
    o = num * (1.0 / den)
    if not want_lse:
        return o, None
    return o, jnp.where(first_head, m[:r], m[r:]) + jnp.log(den)


def _band_window(n, nblk):
    first = jnp.minimum(lax.rem(n, nblk), 1)
    q0 = pl.multiple_of(n * BLOCK, BLOCK)
    ks = pl.multiple_of(q0 - first * BLOCK, BLOCK)
    return q0, ks, first


def _dilated_kernel(q_ref, k_ref, v_ref, out_ref, stage_ref, p4_ref, perm_ref, o_ref, l_ref,
                    bias_ref, *, seq, max_dist):
    _fill_band_bias(bias_ref, max_dist, 2)

    l4, l16 = seq // 4, seq // 16
    for a, x_ref in enumerate((q_ref, k_ref, v_ref)):
        stage_ref[...] = x_ref[...].astype(F32)
        for r in range(4):
            blk = stage_ref[pl.ds(r, l4, stride=4), :]
            p4_ref[r * l4:(r + 1) * l4, :] = blk
            perm_ref[0, a, r * l4:(r + 1) * l4, :] = blk.astype(BF16)
        for r in range(4):
            for rb in range(4):
                blk = p4_ref[pl.ds(r * l4 + rb, l16, stride=4), :]
                res = r + 4 * rb
                perm_ref[1, a, res * l16:(res + 1) * l16, :] = blk.astype(BF16)

    branches = ((1, (q_ref, k_ref, v_ref)),
                (4, tuple(perm_ref.at[0, a] for a in range(3))),
                (16, tuple(perm_ref.at[1, a] for a in range(3))))
    for bi, (d, (qr, kr, vr)) in enumerate(branches):
        nblk = seq // d // BLOCK

        def body(n, carry, d=d, bi=bi, qr=qr, kr=kr, vr=vr, nblk=nblk):
            q0, ks, first = _band_window(n, nblk)
            o, lse = _band_attend(qr[pl.ds(q0, BLOCK), :], kr[pl.ds(ks, 2 * BLOCK), :],
                                  vr[pl.ds(ks, 2 * BLOCK), :], bias_ref[first], True)
            if d == 1:
                rows = pl.ds(q0, BLOCK)
            else:
                rows = pl.ds(lax.rem(n, nblk) * (BLOCK * d) + n // nblk, BLOCK, stride=d)
            o_ref.at[bi][rows, :] = o
            l_ref.at[bi][rows, :] = lse
            return carry

        lax.fori_loop(0, seq // BLOCK, body, 0, unroll=8)

    chunk = 4 * BLOCK

    def merge(c, carry):
        rows = pl.ds(pl.multiple_of(c * chunk, chunk), chunk)
        ls = [l_ref[bi, rows, :] for bi in range(3)]
        m = jnp.maximum(jnp.maximum(ls[0], ls[1]), ls[2])
        ws = [jnp.exp(l - m) for l in ls]
        acc = ws[0] * o_ref[0, rows, :] + ws[1] * o_ref[1, rows, :] + ws[2] * o_ref[2, rows, :]
        out_ref[rows, :] = (acc * (1.0 / (ws[0] + ws[1] + ws[2]))).astype(out_ref.dtype)
        return carry

    lax.fori_loop(0, seq // chunk, merge, 0)


def _dilated_attention(q, k, v):
    assert tuple(d for _, d in DILATED_PATTERN) == (1, 4, 16)
    assert len({w // d for w, d in DILATED_PATTERN}) == 1
    max_dist = DILATED_PATTERN[0][0] // DILATED_PATTERN[0][1]
    bsz, seq, w = q.shape
    spec = pl.BlockSpec((None, seq, LANES), lambda b, c: (b, 0, c))
    return pl.pallas_call(
        functools.partial(_dilated_kernel, seq=seq, max_dist=max_dist),
        grid=(bsz, w // LANES),
        in_specs=[spec, spec, spec],
        out_specs=spec,
        out_shape=jax.ShapeDtypeStruct((bsz, seq, w), BF16),
        scratch_shapes=[pltpu.VMEM((seq, LANES), F32), pltpu.VMEM((seq, LANES), F32),
                        pltpu.VMEM((2, 3, seq, LANES), BF16),
                        pltpu.VMEM((3, seq, LANES), F32), pltpu.VMEM((3, seq, LANES), F32),
                        pltpu.VMEM((2, 2 * BLOCK, 2 * BLOCK), F32)],
        compiler_params=_params("arbitrary", "arbitrary"),
        name="dilated",
    )(q, k, v)


def _swa_kernel(sink_ref, q_ref, k_ref, v_ref, o_ref, bias_ref, *, nblk, max_dist, pairs):
    head0 = pl.program_id(1) * (2 * pairs)
    _fill_band_bias(bias_ref, max_dist, 2 * pairs,
                    lambda rep: sink_ref[head0 + 2 * (rep % pairs) + rep // pairs])
    key_row = lax.broadcasted_iota(jnp.int32, (2 * BLOCK, LANES), 0)

    def body(i, carry):
        q0, ks, first = _band_window(i, nblk)
        qrow = q_ref[pl.ds(q0, BLOCK), :]
        qb = jnp.concatenate([qrow[:, p * LANES:(p + 1) * LANES] for p in range(pairs)], axis=0)
        unseen = key_row == _sink_column(first)
        zero = jnp.zeros((), k_ref.dtype)
        kb = jnp.where(unseen, zero, k_ref[pl.ds(ks, 2 * BLOCK), :])
        vb = jnp.where(unseen, zero, v_ref[pl.ds(ks, 2 * BLOCK), :])
        o, _ = _band_attend(qb, kb, vb, bias_ref[first], False)
        o_ref[pl.ds(q0, BLOCK), :] = jnp.concatenate(
            [o[p * BLOCK:(p + 1) * BLOCK, :] for p in range(pairs)], axis=1).astype(o_ref.dtype)
        return carry

    lax.fori_loop(0, nblk, body, 0, unroll=2)


def _swa_attention(q, kdup, vdup, sinks):
    bsz, seq, w = q.shape
    pairs = w // LANES // C_KV_HEADS
    qspec = pl.BlockSpec((None, seq, pairs * LANES), lambda b, g: (b, 0, g))
    kvspec = pl.BlockSpec((None, seq, LANES), lambda b, g: (b, 0, g))
    return pl.pallas_call(
        functools.partial(_swa_kernel, nblk=seq // BLOCK, max_dist=C_WINDOW - 1, pairs=pairs),
        grid=(bsz, C_KV_HEADS),
        in_specs=[pl.BlockSpec(memory_space=pltpu.SMEM), qspec, kvspec, kvspec],
        out_specs=qspec,
        out_shape=jax.ShapeDtypeStruct((bsz, seq, w), BF16),
        scratch_shapes=[pltpu.VMEM((2, 2 * pairs * BLOCK, 2 * BLOCK), F32)],
        compiler_params=_params("arbitrary", "arbitrary"),
        name="swa_sink",
    )(sinks, q, kdup, vdup)


def _dup_heads(a):
    swapped = pltpu.roll(a, HEAD_DIM, 1)
    low = lax.broadcasted_iota(jnp.int32, a.shape, 1) < HEAD_DIM
    return jnp.concatenate([jnp.where(low, a, swapped), jnp.where(low, swapped, a)], axis=1)


def _swa_proj_kernel(h_ref, g_ref, w_ref, b_ref, rc_ref, rs1_ref, rs2_ref, q_ref, k_ref, v_ref):
    hn = _rms(h_ref[...], g_ref[...]).astype(BF16)
    c, s1, s2 = rc_ref[...], rs1_ref[...], rs2_ref[...]
    ko, vo = C_Q_WIDTH, C_Q_WIDTH + C_KV_WIDTH
    q = _dot(hn, w_ref[:, 0:ko]) + b_ref[:, 0:ko]
    q_ref[...] = (_rope(q, c, s1, s2) * (HEAD_DIM ** -0.5)).astype(BF16)
    k = _dot(hn, w_ref[:, ko:vo]) + b_ref[:, ko:vo]
    k_ref[...] = _dup_heads(_rope(k, c, s1, s2)).astype(BF16)
    v = _dot(hn, w_ref[:, vo:vo + C_KV_WIDTH]) + b_ref[:, vo:vo + C_KV_WIDTH]
    v_ref[...] = _dup_heads(v).astype(BF16)


def _swa_proj(h, g, w, b, tables, tm=TOK_TILE):
    bsz, seq, d = h.shape
    kw = 2 * C_KV_WIDTH
    tok = lambda width: pl.BlockSpec((None, tm, width), lambda bi, i: (bi, i, 0))
    tab = pl.BlockSpec((tm, LANES), lambda bi, i: (i, 0))
    return pl.pallas_call(
        _swa_proj_kernel,
        grid=(bsz, seq // tm),
        in_specs=[tok(d), _layer_spec(g.shape, 0), _layer_spec(w.shape, 0),
                  _layer_spec(b.shape, 0), tab, tab, tab],
        out_specs=(tok(C_Q_WIDTH), tok(kw), tok(kw)),
        out_shape=(jax.ShapeDtypeStruct((bsz, seq, C_Q_WIDTH), BF16),
                   jax.ShapeDtypeStruct((bsz, seq, kw), BF16),
                   jax.ShapeDtypeStruct((bsz, seq, kw), BF16)),
        compiler_params=_params("arbitrary", "arbitrary"),
        name="swa_proj",
    )(h, g, w, b, *tables)


def _mem_kv_kernel(mem_ref, g_ref, w_ref, k_ref, v_ref):
    mn = _rms(mem_ref[...], g_ref[...]).astype(BF16)
    k_ref[...] = _dot(mn, w_ref[:, 0:XA_WIDTH]).astype(BF16)
    v_ref[...] = _dot(mn, w_ref[:, XA_WIDTH:2 * XA_WIDTH]).astype(BF16)


def _mem_kv(mem, g, wkv, layer):
    bsz, m, d = mem.shape
    spec = pl.BlockSpec((None, m, XA_WIDTH), lambda b: (b, 0, 0))
    return pl.pallas_call(
        _mem_kv_kernel,
        grid=(bsz,),
        in_specs=[pl.BlockSpec((None, m, d), lambda b: (b, 0, 0)), _layer_spec(g.shape, layer),
                  _layer_spec(wkv.shape, layer)],
        out_specs=(spec, spec),
        out_shape=(jax.ShapeDtypeStruct((bsz, m, XA_WIDTH), BF16),) * 2,
        compiler_params=_params("arbitrary"),
        name="mem_kv",
    )(mem, g, wkv)


def _layer_tail_kernel(*refs, n_act, d_ff, final_norm):
    h_ref, act_refs = refs[0], refs[1:1 + n_act]
    (wout_ref, bout_ref, xg_ref, wq_ref, mk_ref, mv_ref, wo_ref,
     fg_ref, wgu_ref, wd_ref, ng_ref, out_ref) = refs[1 + n_act:]

    h = h_ref[...] + bout_ref[...]
    row = 0
    for a_ref in act_refs:
        width = a_ref.shape[-1]
        h = h + _dot(a_ref[...], wout_ref[row:row + width, :])
        row += width

    q = _dot(_rms(h, xg_ref[...]).astype(BF16), wq_ref[...]).astype(BF16)
    scale = XA_HEAD_DIM ** -0.5
    heads = []
    for hd in range(XA_HEADS):
        sl = slice(hd * XA_HEAD_DIM, (hd + 1) * XA_HEAD_DIM)
        s = _dot_nt(q[:, sl], mk_ref[:, sl]) * scale
        e = jnp.exp(s - jnp.max(s, axis=-1, keepdims=True))
        den = jnp.sum(e, axis=-1, keepdims=True)
        heads.append((_dot(e.astype(BF16), mv_ref[:, sl]) * (1.0 / den)).astype(BF16))
    h = h + _dot(jnp.concatenate(heads, axis=1), wo_ref[...])

    hn = _rms(h, fg_ref[...]).astype(BF16)
    for c0 in range(0, d_ff, FF_CHUNK):
        gate = _dot(hn, wgu_ref[:, c0:c0 + FF_CHUNK])
        up = _dot(hn, wgu_ref[:, d_ff + c0:d_ff + c0 + FF_CHUNK])
        act = (gate * jax.nn.sigmoid(gate) * up).astype(BF16)
        h = h + _dot(act, wd_ref[c0:c0 + FF_CHUNK, :])
    out_ref[...] = _rms(h, ng_ref[...]) if final_norm else h


def _layer_tail(h, acts, w_out, b_out, mk, mv, p, layer, final_norm, tm=TOK_TILE):
    bsz, seq, d = h.shape
    d_ff = p["wd"].shape[1]
    assert d_ff % FF_CHUNK == 0 and sum(a.shape[-1] for a in acts) == w_out.shape[1]
    tok = lambda width: pl.BlockSpec((None, tm, width), lambda b, i: (b, i, 0))
    kv = pl.BlockSpec((None, N_MEM, XA_WIDTH), lambda b, i: (b, 0, 0))
    return pl.pallas_call(
        functools.partial(_layer_tail_kernel, n_act=len(acts), d_ff=d_ff, final_norm=final_norm),
        grid=(bsz, seq // tm),
        in_specs=[tok(d)] + [tok(a.shape[-1]) for a in acts] + [
            _layer_spec(w_out.shape, 0), _layer_spec(b_out.shape, 0),
            _layer_spec(p["xg"].shape, layer), _layer_spec(p["wq"].shape, layer), kv, kv,
            _layer_spec(p["wo"].shape, layer), _layer_spec(p["fg"].shape, layer),
            _layer_spec(p["wgu"].shape, layer), _layer_spec(p["wd"].shape, layer),
            _const_spec(p["ng"].shape)],
        out_specs=tok(d),
        out_shape=jax.ShapeDtypeStruct((bsz, seq, d), F32),
        compiler_params=_params("arbitrary", "arbitrary"),
        name="layer_tail",
    )(h, *acts, w_out, b_out, p["xg"], p["wq"], mk, mv, p["wo"], p["fg"], p["wgu"], p["wd"],
      p["ng"])


def kernel(x, mem, mix_norm, ab_w_in, lru_conv_w, lru_conv_b, lru_wa, lru_ba, lru_wx, lru_bx, lru_lambda, ab_w_out, c_w_qkv, c_b_qkv, c_sinks, c_w_out, c_b_out, xa_norm, xa_mem_norm, xa_wq, xa_wkv, xa_wo, ffn_norm, ffn_w_gate_up, ffn_w_down, final_norm):
    bsz, seq, d = x.shape
    tables = _rope_tables(seq)
    rows = lambda a: a.reshape(a.shape[0], 1, -1)
    bf = lambda a: a.astype(BF16)
    mix_g = rows(mix_norm)
    p = dict(xg=rows(xa_norm), wq=bf(xa_wq), wo=bf(xa_wo), fg=rows(ffn_norm),
             wgu=bf(ffn_w_gate_up), wd=bf(ffn_w_down), ng=final_norm.reshape(1, d))
    mem_g, wkv = rows(xa_mem_norm), bf(xa_wkv)

    rec, q, k, v = _lru_proj(x, mix_g, bf(ab_w_in), lru_conv_w, rows(lru_conv_b), bf(lru_wa),
                             rows(lru_ba), bf(lru_wx), rows(lru_bx), rows(lru_lambda), tables)
    att = _dilated_attention(q, k, v)
    mk, mv = _mem_kv(mem, mem_g, wkv, 0)
    h = _layer_tail(x, (rec, att), bf(ab_w_out), jnp.zeros((1, 1, d), F32), mk, mv, p,
                    layer=0, final_norm=False)

    q, k, v = _swa_proj(h, mix_g[1:2], bf(c_w_qkv), rows(c_b_qkv), tables)
    att = _swa_attention(q, k, v, c_sinks[0])
    mk, mv = _mem_kv(mem, mem_g, wkv, 1)
    return _layer_tail(h, (att,), bf(c_w_out), rows(c_b_out), mk, mv, p,
                       layer=1, final_norm=True)
```

```python
import functools
import math

import jax
import jax.numpy as jnp
from jax import lax
from jax.experimental import pallas as pl
from jax.experimental.pallas import tpu as pltpu

F32 = jnp.float32
BF16 = jnp.bfloat16

D_MODEL = 1024
N_MEM = 256
NORM_EPS = 1e-6
ROPE_THETA = 500000.0
HEAD_DIM = 64
ROT_DIM = HEAD_DIM // 4
ROT_HALF = ROT_DIM // 2
LANES = 128
SUBLANES = 8
BLOCK = 128
LRU_WIDTH = D_MODEL
LRU_HEADS = 4
LRU_HEAD_DIM = LRU_WIDTH // LRU_HEADS
CONV_WIDTH = 4
LRU_C = 8.0
B_HEADS = 8
B_WIDTH = B_HEADS * HEAD_DIM
DILATED_PATTERN = ((128, 1), (512, 4), (2048, 16))
C_HEADS = 16
C_KV_HEADS = 2
C_WINDOW = 128
C_Q_WIDTH = C_HEADS * HEAD_DIM
C_KV_WIDTH = C_KV_HEADS * HEAD_DIM
XA_HEADS = 4
XA_HEAD_DIM = 128
XA_WIDTH = XA_HEADS * XA_HEAD_DIM
NEG = -1e30
VMEM_LIMIT = 56 * 1024 * 1024

TOK_TILE = 512
MXU_TILE = 256
FF_CHUNK_TILES = 6


def _ff_chunks(d_ff):
    assert d_ff % MXU_TILE == 0
    step = FF_CHUNK_TILES * MXU_TILE
    return [(c0, min(step, d_ff - c0)) for c0 in range(0, d_ff, step)]


def _dot(a, b):
    return jnp.dot(a, b, preferred_element_type=F32)


def _dot_nt(a, b):
    return lax.dot_general(a, b, (((1,), (1,)), ((), ())), preferred_element_type=F32)


def _rms(x, g):
    return x * lax.rsqrt(jnp.mean(x * x, axis=-1, keepdims=True) + NORM_EPS) * g


def _params(*sem):
    return pltpu.CompilerParams(dimension_semantics=sem, vmem_limit_bytes=VMEM_LIMIT)


def _const_spec(shape):
    nd = len(shape)
    return pl.BlockSpec(shape, lambda *_: (0,) * nd, pipeline_mode=pl.Buffered(1))


def _layer_spec(shape, layer):
    nd = len(shape)
    return pl.BlockSpec((None,) + tuple(shape[1:]), lambda *_: (layer,) + (0,) * (nd - 1),
                        pipeline_mode=pl.Buffered(1))


def _rope(x, c, s1, s2):
    cols = []
    for j in range(x.shape[1] // LANES):
        xc = x[:, j * LANES:(j + 1) * LANES]
        cols.append(xc * c + pltpu.roll(xc, ROT_HALF, 1) * s1
                    + pltpu.roll(xc, LANES - ROT_HALF, 1) * s2)
    return cols[0] if len(cols) == 1 else jnp.concatenate(cols, axis=1)


def _rope_tables(seq):
    lane = jnp.arange(LANES) % HEAD_DIM
    inv = ROPE_THETA ** (-(2 * (lane % ROT_HALF)).astype(F32) / ROT_DIM)
    ang = jnp.arange(seq, dtype=F32)[:, None] * inv[None, :]
    cos, sin = jnp.cos(ang), jnp.sin(ang)
    c = jnp.where(lane < ROT_DIM, cos, 1.0)
    s1 = jnp.where(jnp.logical_and(lane >= ROT_HALF, lane < ROT_DIM), sin, 0.0)
    s2 = jnp.where(lane < ROT_HALF, -sin, 0.0)
    return jnp.stack([c, s1, s2])


def _table_specs(tm):
    return [pl.BlockSpec((None, tm, LANES), lambda b, i, j=j: (j, i, 0)) for j in range(3)]


def _sigmoid(x):
    return 0.5 * jnp.tanh(0.5 * x) + 0.5


def _shift_rows(x, k, head):
    rolled = pltpu.roll(x, k, 0)
    sub = lax.broadcasted_iota(jnp.int32, head.shape, 0)
    first = jnp.where(sub < k, pltpu.roll(head, k, 0), rolled[0:SUBLANES])
    return jnp.concatenate([first, rolled[SUBLANES:]], axis=0)


def _lru_proj_kernel(h_ref, g_ref, w_ref, cw_ref, cb_ref, wa_ref, ba_ref, wx_ref, bx_ref,
                     lam_ref, rc_ref, rs1_ref, rs2_ref,
                     rec_ref, q_ref, k_ref, v_ref,
                     hn_ref, tail_ref, as_ref, us_ref, hs_ref, hprev_ref, *, tl):
    @pl.when(pl.program_id(1) == 0)
    def _():
        tail_ref[...] = jnp.zeros_like(tail_ref)
        hprev_ref[...] = jnp.zeros_like(hprev_ref)

    hn_ref[...] = _rms(h_ref[...], g_ref[...]).astype(BF16)
    hn = hn_ref[...]

    x = _dot(hn, w_ref[:, 0:LRU_WIDTH])
    tail = tail_ref[...]
    xc = cb_ref[...] + _shift_rows(x, 3, tail) * cw_ref[0:1, :]
    xc = xc + _shift_rows(x, 2, tail) * cw_ref[1:2, :]
    xc = xc + _shift_rows(x, 1, tail) * cw_ref[2:3, :]
    xc = xc + x * cw_ref[3:4, :]
    tail_ref[...] = x[tl - SUBLANES:tl, :]

    xcb = xc.astype(BF16)
    r_parts, i_parts = [], []
    for hd in range(LRU_HEADS):
        sl = slice(hd * LRU_HEAD_DIM, (hd + 1) * LRU_HEAD_DIM)
        r_parts.append(_dot(xcb[:, sl], wa_ref[hd]))
        i_parts.append(_dot(xcb[:, sl], wx_ref[hd]))
    r = _sigmoid(jnp.concatenate(r_parts, axis=1) + ba_ref[...])
    ig = _sigmoid(jnp.concatenate(i_parts, axis=1) + bx_ref[...])

    nlam = -lam_ref[...]
    softplus = jnp.maximum(nlam, 0.0) + jnp.log1p(jnp.exp(-jnp.abs(nlam)))
    log_a = (-LRU_C) * r * softplus
    a = jnp.exp(log_a)
    u = jnp.sqrt(jnp.tanh(-log_a) * (1.0 + a * a)) * (ig * xc)

    ones = jnp.ones((SUBLANES, LRU_WIDTH), F32)
    zeros = jnp.zeros((SUBLANES, LRU_WIDTH), F32)
    for s in (1, 2, 4):
        u = a * _shift_rows(u, s, zeros) + u
        a = a * _shift_rows(a, s, ones)
    as_ref[...] = a
    us_ref[...] = u

    def chain(g, hcar):
        rows = pl.ds(pl.multiple_of(g * SUBLANES, SUBLANES), SUBLANES)
        hcar = as_ref[rows, :] * hcar + us_ref[rows, :]
        hs_ref[rows, :] = hcar
        return hcar

    h0 = jnp.broadcast_to(hprev_ref[SUBLANES - 1:SUBLANES, :], (SUBLANES, LRU_WIDTH))
    hprev_ref[...] = lax.fori_loop(0, tl // SUBLANES, chain, h0, unroll=8)

    y = _dot(hn, w_ref[:, LRU_WIDTH:2 * LRU_WIDTH])
    gelu = 0.5 * y * (1.0 + jnp.tanh(math.sqrt(2.0 / math.pi) * (y + 0.044715 * (y * y * y))))
    rec_ref[...] = (hs_ref[...] * gelu).astype(BF16)

    c, s1, s2 = rc_ref[...], rs1_ref[...], rs2_ref[...]
    o = 2 * LRU_WIDTH
    q = _dot(hn, w_ref[:, o:o + B_WIDTH])
    q_ref[...] = (_rope(q, c, s1, s2) * (HEAD_DIM ** -0.5)).astype(BF16)
    k = _dot(hn, w_ref[:, o + B_WIDTH:o + 2 * B_WIDTH])
    k_ref[...] = _rope(k, c, s1, s2).astype(BF16)
    v_ref[...] = _dot(hn, w_ref[:, o + 2 * B_WIDTH:o + 3 * B_WIDTH]).astype(BF16)


def _lru_proj(h, g, w_in, conv_w, conv_b, wa, ba, wx, bx, lam, tables, tl=TOK_TILE):
    bsz, seq, d = h.shape
    tok = lambda w: pl.BlockSpec((None, tl, w), lambda b, i: (b, i, 0))
    out_shape = (jax.ShapeDtypeStruct((bsz, seq, LRU_WIDTH), BF16),) + tuple(
        jax.ShapeDtypeStruct((bsz, seq, B_WIDTH), BF16) for _ in range(3))
    params = (g, w_in, conv_w, conv_b, wa, ba, wx, bx, lam)
    return pl.pallas_call(
        functools.partial(_lru_proj_kernel, tl=tl),
        grid=(bsz, seq // tl),
        in_specs=[tok(d)] + [_layer_spec(a.shape, 0) for a in params] + _table_specs(tl),
        out_specs=(tok(LRU_WIDTH), tok(B_WIDTH), tok(B_WIDTH), tok(B_WIDTH)),
        out_shape=out_shape,
        scratch_shapes=[pltpu.VMEM((tl, d), BF16),
                        pltpu.VMEM((SUBLANES, LRU_WIDTH), F32),
                        pltpu.VMEM((tl, LRU_WIDTH), F32),
                        pltpu.VMEM((tl, LRU_WIDTH), F32),
                        pltpu.VMEM((tl, LRU_WIDTH), F32),
                        pltpu.VMEM((SUBLANES, LRU_WIDTH), F32)],
        compiler_params=_params("arbitrary", "arbitrary"),
        name="lru_proj",
    )(h, *params, tables, tables, tables)


def _sink_column(first):
    return (2 * BLOCK - 1) * (1 - first)


def _fill_band_bias(bias_ref, max_dist, reps, sink_of_rep=None):
    rows = lax.broadcasted_iota(jnp.int32, (BLOCK, 2 * BLOCK), 0)
    cols = lax.broadcasted_iota(jnp.int32, (BLOCK, 2 * BLOCK), 1)
    for first, off in ((0, 0), (1, BLOCK)):
        dist = off + rows - cols
        valid = jnp.logical_and(dist >= 0, dist <= max_dist)
        tile = jnp.where(valid, 0.0, NEG).astype(F32)
        for rep in range(reps):
            rep_tile = tile
            if sink_of_rep is not None:
                assert max_dist < BLOCK
                rep_tile = jnp.where(cols == _sink_column(first), sink_of_rep(rep), tile)
            bias_ref[first, rep * BLOCK:(rep + 1) * BLOCK, :] = rep_tile


def _band_attend(qb, kb, vb, bias, want_lse):
    r = qb.shape[0]
    lane_q = lax.broadcasted_iota(jnp.int32, qb.shape, 1)
    zero = jnp.zeros((), qb.dtype)
    qz = jnp.concatenate([jnp.where(lane_q < HEAD_DIM, qb, zero),
                          jnp.where(lane_q >= HEAD_DIM, qb, zero)], axis=0)
    s = _dot_nt(qz, kb) + bias
    m = jnp.max(s, axis=-1, keepdims=True)
    e = jnp.exp(s - m).astype(vb.dtype)
    ob = _dot(e, jnp.concatenate([vb, jnp.ones_like(vb)], axis=1))
    first_head = lax.broadcasted_iota(jnp.int32, (r, LANES), 1) < HEAD_DIM
    num = jnp.where(first_head, ob[:r, :LANES], ob[r:, :LANES])
    den = jnp.where(first_head, ob[:r, LANES:], ob[r:, LANES:])
    o = num * (1.0 / den)
    if not want_lse:
        return o, None
    return o, jnp.where(first_head, m[:r], m[r:]) + jnp.log(den)


def _band_window(n, nblk):
    first = jnp.minimum(lax.rem(n, nblk), 1)
    q0 = pl.multiple_of(n * BLOCK, BLOCK)
    ks = pl.multiple_of(q0 - first * BLOCK, BLOCK)
    return q0, ks, first


def _dilated_kernel(q_ref, k_ref, v_ref, out_ref, stage_ref, p4_ref, perm_ref, o_ref, l_ref,
                    bias_ref, *, seq, max_dist):
    _fill_band_bias(bias_ref, max_dist, 2)

    l4, l16 = seq // 4, seq // 16
    for a, x_ref in enumerate((q_ref, k_ref, v_ref)):
        stage_ref[...] = x_ref[...].astype(F32)
        for r in range(4):
            blk = stage_ref[pl.ds(r, l4, stride=4), :]
            p4_ref[r * l4:(r + 1) * l4, :] = blk
            perm_ref[0, a, r * l4:(r + 1) * l4, :] = blk.astype(BF16)
        for r in range(4):
            for rb in range(4):
                blk = p4_ref[pl.ds(r * l4 + rb, l16, stride=4), :]
                res = r + 4 * rb
                perm_ref[1, a, res * l16:(res + 1) * l16, :] = blk.astype(BF16)

    branches = ((1, (q_ref, k_ref, v_ref)),
                (4, tuple(perm_ref.at[0, a] for a in range(3))),
                (16, tuple(perm_ref.at[1, a] for a in range(3))))
    for bi, (d, (qr, kr, vr)) in enumerate(branches):
        nblk = seq // d // BLOCK

        def body(n, carry, d=d, bi=bi, qr=qr, kr=kr, vr=vr, nblk=nblk):
            q0, ks, first = _band_window(n, nblk)
            o, lse = _band_attend(qr[pl.ds(q0, BLOCK), :], kr[pl.ds(ks, 2 * BLOCK), :],
                                  vr[pl.ds(ks, 2 * BLOCK), :], bias_ref[first], True)
            if d == 1:
                rows = pl.ds(q0, BLOCK)
            else:
                rows = pl.ds(lax.rem(n, nblk) * (BLOCK * d) + n // nblk, BLOCK, stride=d)
            o_ref.at[bi][rows, :] = o
            l_ref.at[bi][rows, :] = lse
            return carry

        lax.fori_loop(0, seq // BLOCK, body, 0, unroll=8)

    chunk = 4 * BLOCK

    def merge(c, carry):
        rows = pl.ds(pl.multiple_of(c * chunk, chunk), chunk)
        ls = [l_ref[bi, rows, :] for bi in range(3)]
        m = jnp.maximum(jnp.maximum(ls[0], ls[1]), ls[2])
        ws = [jnp.exp(l - m) for l in ls]
        acc = ws[0] * o_ref[0, rows, :] + ws[1] * o_ref[1, rows, :] + ws[2] * o_ref[2, rows, :]
        out_ref[rows, :] = (acc * (1.0 / (ws[0] + ws[1] + ws[2]))).astype(out_ref.dtype)
        return carry

    lax.fori_loop(0, seq // chunk, merge, 0)


def _dilated_attention(q, k, v):
    assert tuple(d for _, d in DILATED_PATTERN) == (1, 4, 16)
    assert len({w // d for w, d in DILATED_PATTERN}) == 1
    max_dist = DILATED_PATTERN[0][0] // DILATED_PATTERN[0][1]
    bsz, seq, w = q.shape
    spec = pl.BlockSpec((None, seq, LANES), lambda b, c: (b, 0, c))
    return pl.pallas_call(
        functools.partial(_dilated_kernel, seq=seq, max_dist=max_dist),
        grid=(bsz, w // LANES),
        in_specs=[spec, spec, spec],
        out_specs=spec,
        out_shape=jax.ShapeDtypeStruct((bsz, seq, w), BF16),
        scratch_shapes=[pltpu.VMEM((seq, LANES), F32), pltpu.VMEM((seq, LANES), F32),
                        pltpu.VMEM((2, 3, seq, LANES), BF16),
                        pltpu.VMEM((3, seq, LANES), F32), pltpu.VMEM((3, seq, LANES), F32),
                        pltpu.VMEM((2, 2 * BLOCK, 2 * BLOCK), F32)],
        compiler_params=_params("arbitrary", "arbitrary"),
        name="dilated",
    )(q, k, v)


def _swa_kernel(sink_ref, q_ref, k_ref, v_ref, o_ref, bias_ref, *, nblk, max_dist, pairs):
    head0 = pl.program_id(1) * (2 * pairs)
    _fill_band_bias(bias_ref, max_dist, 2 * pairs,
                    lambda rep: sink_ref[head0 + 2 * (rep % pairs) + rep // pairs])
    key_row = lax.broadcasted_iota(jnp.int32, (2 * BLOCK, LANES), 0)

    def body(i, carry):
        q0, ks, first = _band_window(i, nblk)
        qrow = q_ref[pl.ds(q0, BLOCK), :]
        qb = jnp.concatenate([qrow[:, p * LANES:(p + 1) * LANES] for p in range(pairs)], axis=0)
        unseen = key_row == _sink_column(first)
        zero = jnp.zeros((), k_ref.dtype)
        kb = jnp.where(unseen, zero, k_ref[pl.ds(ks, 2 * BLOCK), :])
        vb = jnp.where(unseen, zero, v_ref[pl.ds(ks, 2 * BLOCK), :])
        o, _ = _band_attend(qb, kb, vb, bias_ref[first], False)
        o_ref[pl.ds(q0, BLOCK), :] = jnp.concatenate(
            [o[p * BLOCK:(p + 1) * BLOCK, :] for p in range(pairs)], axis=1).astype(o_ref.dtype)
        return carry

    lax.fori_loop(0, nblk, body, 0, unroll=2)


def _swa_attention(q, kdup, vdup, sinks):
    bsz, seq, w = q.shape
    pairs = w // LANES // C_KV_HEADS
    qspec = pl.BlockSpec((None, seq, pairs * LANES), lambda b, g: (b, 0, g))
    kvspec = pl.BlockSpec((None, seq, LANES), lambda b, g: (b, 0, g))
    return pl.pallas_call(
        functools.partial(_swa_kernel, nblk=seq // BLOCK, max_dist=C_WINDOW - 1, pairs=pairs),
        grid=(bsz, C_KV_HEADS),
        in_specs=[pl.BlockSpec(memory_space=pltpu.SMEM), qspec, kvspec, kvspec],
        out_specs=qspec,
        out_shape=jax.ShapeDtypeStruct((bsz, seq, w), BF16),
        scratch_shapes=[pltpu.VMEM((2, 2 * pairs * BLOCK, 2 * BLOCK), F32)],
        compiler_params=_params("arbitrary", "arbitrary"),
        name="swa_sink",
    )(sinks, q, kdup, vdup)


def _dup_heads(a):
    swapped = pltpu.roll(a, HEAD_DIM, 1)
    low = lax.broadcasted_iota(jnp.int32, a.shape, 1) < HEAD_DIM
    return jnp.concatenate([jnp.where(low, a, swapped), jnp.where(low, swapped, a)], axis=1)


def _swa_qkv(h, g_ref, w_ref, b_ref, rc_ref, rs1_ref, rs2_ref, q_ref, k_ref, v_ref):
    hn = _rms(h, g_ref[...]).astype(BF16)
    c, s1, s2 = rc_ref[...], rs1_ref[...], rs2_ref[...]
    ko, vo = C_Q_WIDTH, C_Q_WIDTH + C_KV_WIDTH
    q = _dot(hn, w_ref[:, 0:ko]) + b_ref[:, 0:ko]
    q_ref[...] = (_rope(q, c, s1, s2) * (HEAD_DIM ** -0.5)).astype(BF16)
    kv = _dot(hn, w_ref[:, ko:vo + C_KV_WIDTH]) + b_ref[:, ko:vo + C_KV_WIDTH]
    k_ref[...] = _dup_heads(_rope(kv[:, :C_KV_WIDTH], c, s1, s2)).astype(BF16)
    v_ref[...] = _dup_heads(kv[:, C_KV_WIDTH:]).astype(BF16)


def _mem_kv_kernel(mem_ref, g_ref, w_ref, k_ref, v_ref):
    mn = _rms(mem_ref[...], g_ref[...]).astype(BF16)
    k_ref[...] = _dot(mn, w_ref[:, 0:XA_WIDTH]).astype(BF16)
    v_ref[...] = _dot(mn, w_ref[:, XA_WIDTH:2 * XA_WIDTH]).astype(BF16)


def _mem_kv(mem, g, wkv, layer):
    bsz, m, d = mem.shape
    spec = pl.BlockSpec((None, m, XA_WIDTH), lambda b: (b, 0, 0))
    return pl.pallas_call(
        _mem_kv_kernel,
        grid=(bsz,),
        in_specs=[pl.BlockSpec((None, m, d), lambda b: (b, 0, 0)), _layer_spec(g.shape, layer),
                  _layer_spec(wkv.shape, layer)],
        out_specs=(spec, spec),
        out_shape=(jax.ShapeDtypeStruct((bsz, m, XA_WIDTH), BF16),) * 2,
        compiler_params=_params("arbitrary"),
        name="mem_kv",
    )(mem, g, wkv)


def _layer_tail_kernel(*refs, n_act, d_ff, last):
    h_ref, act_refs = refs[0], refs[1:1 + n_act]
    (wout_ref, bout_ref, xg_ref, wq_ref, mk_ref, mv_ref, wo_ref,
     fg_ref, wgu_ref, wd_ref) = refs[1 + n_act:11 + n_act]
    epilogue = refs[11 + n_act:]

    h = h_ref[...] + bout_ref[...]
    row = 0
    for a_ref in act_refs:
        width = a_ref.shape[-1]
        h = h + _dot(a_ref[...], wout_ref[row:row + width, :])
        row += width

    q = _dot(_rms(h, xg_ref[...]).astype(BF16), wq_ref[...]).astype(BF16)
    scale = XA_HEAD_DIM ** -0.5
    heads = []
    for hd in range(XA_HEADS):
        sl = slice(hd * XA_HEAD_DIM, (hd + 1) * XA_HEAD_DIM)
        s = _dot_nt(q[:, sl], mk_ref[:, sl]) * scale
        e = jnp.exp(s - jnp.max(s, axis=-1, keepdims=True))
        den = jnp.sum(e, axis=-1, keepdims=True)
        heads.append((_dot(e.astype(BF16), mv_ref[:, sl]) * (1.0 / den)).astype(BF16))
    h = h + _dot(jnp.concatenate(heads, axis=1), wo_ref[...])

    hn = _rms(h, fg_ref[...]).astype(BF16)
    for c0, width in _ff_chunks(d_ff):
        gate = _dot(hn, wgu_ref[:, c0:c0 + width])
        up = _dot(hn, wgu_ref[:, d_ff + c0:d_ff + c0 + width])
        act = (gate * jax.nn.sigmoid(gate) * up).astype(BF16)
        h = h + _dot(act, wd_ref[c0:c0 + width, :])
    if last:
        ng_ref, out_ref = epilogue
        out_ref[...] = _rms(h, ng_ref[...])
    else:
        out_ref = epilogue[6]
        out_ref[...] = h
        _swa_qkv(h, *epilogue[:6], *epilogue[7:])


def _layer_tail(h, acts, w_out, b_out, mk, mv, p, layer, nxt=None, tm=TOK_TILE):
    bsz, seq, d = h.shape
    d_ff = p["wd"].shape[1]
    assert sum(a.shape[-1] for a in acts) == w_out.shape[1]
    tok = lambda width: pl.BlockSpec((None, tm, width), lambda b, i: (b, i, 0))
    kv = pl.BlockSpec((None, N_MEM, XA_WIDTH), lambda b, i: (b, 0, 0))
    res = jax.ShapeDtypeStruct((bsz, seq, d), F32)
    if nxt is None:
        extra, extra_specs = (p["ng"],), [_const_spec(p["ng"].shape)]
        out_specs, out_shape = tok(d), res
    else:
        g2, w2, b2, tables = nxt
        extra = (g2, w2, b2, tables, tables, tables)
        extra_specs = [_layer_spec(a.shape, 0) for a in (g2, w2, b2)] + _table_specs(tm)
        widths = (C_Q_WIDTH, 2 * C_KV_WIDTH, 2 * C_KV_WIDTH)
        out_specs = (tok(d),) + tuple(tok(w) for w in widths)
        out_shape = (res,) + tuple(jax.ShapeDtypeStruct((bsz, seq, w), BF16) for w in widths)
    return pl.pallas_call(
        functools.partial(_layer_tail_kernel, n_act=len(acts), d_ff=d_ff, last=nxt is None),
        grid=(bsz, seq // tm),
        in_specs=[tok(d)] + [tok(a.shape[-1]) for a in acts] + [
            _layer_spec(w_out.shape, 0), _layer_spec(b_out.shape, 0),
            _layer_spec(p["xg"].shape, layer), _layer_spec(p["wq"].shape, layer), kv, kv,
            _layer_spec(p["wo"].shape, layer), _layer_spec(p["fg"].shape, layer),
            _layer_spec(p["wgu"].shape, layer), _layer_spec(p["wd"].shape, layer)] + extra_specs,
        out_specs=out_specs,
        out_shape=out_shape,
        compiler_params=_params("arbitrary", "arbitrary"),
        name="layer_tail",
    )(h, *acts, w_out, b_out, p["xg"], p["wq"], mk, mv, p["wo"], p["fg"], p["wgu"], p["wd"],
      *extra)


def kernel(x, mem, mix_norm, ab_w_in, lru_conv_w, lru_conv_b, lru_wa, lru_ba, lru_wx, lru_bx, lru_lambda, ab_w_out, c_w_qkv, c_b_qkv, c_sinks, c_w_out, c_b_out, xa_norm, xa_mem_norm, xa_wq, xa_wkv, xa_wo, ffn_norm, ffn_w_gate_up, ffn_w_down, final_norm):
    bsz, seq, d = x.shape
    tables = _rope_tables(seq)
    rows = lambda a: a.reshape(a.shape[0], 1, -1)
    bf = lambda a: a.astype(BF16)
    mix_g = rows(mix_norm)
    p = dict(xg=rows(xa_norm), wq=bf(xa_wq), wo=bf(xa_wo), fg=rows(ffn_norm),
             wgu=bf(ffn_w_gate_up), wd=bf(ffn_w_down), ng=final_norm.reshape(1, d))
    mem_g, wkv = rows(xa_mem_norm), bf(xa_wkv)

    rec, q, k, v = _lru_proj(x, mix_g, bf(ab_w_in), lru_conv_w, rows(lru_conv_b), bf(lru_wa),
                             rows(lru_ba), bf(lru_wx), rows(lru_bx), rows(lru_lambda), tables)
    att = _dilated_attention(q, k, v)
    mk, mv = _mem_kv(mem, mem_g, wkv, 0)
    h, q, k, v = _layer_tail(x, (rec, att), bf(ab_w_out), jnp.zeros((1, 1, d), F32), mk, mv, p,
                             layer=0, nxt=(mix_g[1:2], bf(c_w_qkv), rows(c_b_qkv), tables))

    att = _swa_attention(q, k, v, c_sinks[0])
    mk, mv = _mem_kv(mem, mem_g, wkv, 1)
    return _layer_tail(h, (att,), bf(c_w_out), rows(c_b_out), mk, mv, p, layer=1)
```

```python
import functools
import math

import jax
import jax.numpy as jnp
from jax import lax
from jax.experimental import pallas as pl
from jax.experimental.pallas import tpu as pltpu

F32 = jnp.float32
BF16 = jnp.bfloat16

D_MODEL = 1024
N_MEM = 256
NORM_EPS = 1e-6
ROPE_THETA = 500000.0
HEAD_DIM = 64
ROT_DIM = HEAD_DIM // 4
ROT_HALF = ROT_DIM // 2
LANES = 128
SUBLANES = 8
BLOCK = 128
LRU_WIDTH = D_MODEL
LRU_HEADS = 4
LRU_HEAD_DIM = LRU_WIDTH // LRU_HEADS
CONV_WIDTH = 4
LRU_C = 8.0
B_HEADS = 8
B_WIDTH = B_HEADS * HEAD_DIM
DILATED_PATTERN = ((128, 1), (512, 4), (2048, 16))
C_HEADS = 16
C_KV_HEADS = 2
C_WINDOW = 128
C_Q_WIDTH = C_HEADS * HEAD_DIM
C_KV_WIDTH = C_KV_HEADS * HEAD_DIM
XA_HEADS = 4
XA_HEAD_DIM = 128
XA_WIDTH = XA_HEADS * XA_HEAD_DIM
NEG = -1e30
VMEM_LIMIT = 56 * 1024 * 1024

TOK_TILE = 512
MXU_TILE = 256
FF_CHUNK_TILES = 6


def _ff_chunks(d_ff):
    assert d_ff % MXU_TILE == 0
    step = FF_CHUNK_TILES * MXU_TILE
    return [(c0, min(step, d_ff - c0)) for c0 in range(0, d_ff, step)]


def _dot(a, b):
    return jnp.dot(a, b, preferred_element_type=F32)


def _dot_nt(a, b):
    return lax.dot_general(a, b, (((1,), (1,)), ((), ())), preferred_element_type=F32)


def _rms(x, g):
    return x * lax.rsqrt(jnp.mean(x * x, axis=-1, keepdims=True) + NORM_EPS) * g


def _params(*sem):
    return pltpu.CompilerParams(dimension_semantics=sem, vmem_limit_bytes=VMEM_LIMIT)


def _const_spec(shape):
    nd = len(shape)
    return pl.BlockSpec(shape, lambda *_: (0,) * nd, pipeline_mode=pl.Buffered(1))


def _layer_spec(shape, layer):
    nd = len(shape)
    return pl.BlockSpec((None,) + tuple(shape[1:]), lambda *_: (layer,) + (0,) * (nd - 1),
                        pipeline_mode=pl.Buffered(1))


def _rope(x, c, s1, s2):
    cols = []
    for j in range(x.shape[1] // LANES):
        xc = x[:, j * LANES:(j + 1) * LANES]
        cols.append(xc * c + pltpu.roll(xc, ROT_HALF, 1) * s1
                    + pltpu.roll(xc, LANES - ROT_HALF, 1) * s2)
    return cols[0] if len(cols) == 1 else jnp.concatenate(cols, axis=1)


def _rope_tables(seq):
    lane = jnp.arange(LANES) % HEAD_DIM
    inv = ROPE_THETA ** (-(2 * (lane % ROT_HALF)).astype(F32) / ROT_DIM)
    ang = jnp.arange(seq, dtype=F32)[:, None] * inv[None, :]
    cos, sin = jnp.cos(ang), jnp.sin(ang)
    c = jnp.where(lane < ROT_DIM, cos, 1.0)
    s1 = jnp.where(jnp.logical_and(lane >= ROT_HALF, lane < ROT_DIM), sin, 0.0)
    s2 = jnp.where(lane < ROT_HALF, -sin, 0.0)
    return jnp.stack([c, s1, s2])


def _table_specs(tm):
    return [pl.BlockSpec((None, tm, LANES), lambda b, i, j=j: (j, i, 0)) for j in range(3)]


def _sigmoid(x):
    return 0.5 * jnp.tanh(0.5 * x) + 0.5


def _shift_rows(x, k, head):
    rolled = pltpu.roll(x, k, 0)
    sub = lax.broadcasted_iota(jnp.int32, head.shape, 0)
    first = jnp.where(sub < k, pltpu.roll(head, k, 0), rolled[0:SUBLANES])
    return jnp.concatenate([first, rolled[SUBLANES:]], axis=0)


def _lru_proj_kernel(h_ref, g_ref, w_ref, cw_ref, cb_ref, wa_ref, ba_ref, wx_ref, bx_ref,
                     lam_ref, rc_ref, rs1_ref, rs2_ref,
                     rec_ref, q_ref, k_ref, v_ref,
                     hn_ref, tail_ref, as_ref, us_ref, hs_ref, hprev_ref, *, tl):
    @pl.when(pl.program_id(1) == 0)
    def _():
        tail_ref[...] = jnp.zeros_like(tail_ref)
        hprev_ref[...] = jnp.zeros_like(hprev_ref)

    hn_ref[...] = _rms(h_ref[...], g_ref[...]).astype(BF16)
    hn = hn_ref[...]

    x = _dot(hn, w_ref[:, 0:LRU_WIDTH])
    tail = tail_ref[...]
    xc = cb_ref[...] + _shift_rows(x, 3, tail) * cw_ref[0:1, :]
    xc = xc + _shift_rows(x, 2, tail) * cw_ref[1:2, :]
    xc = xc + _shift_rows(x, 1, tail) * cw_ref[2:3, :]
    xc = xc + x * cw_ref[3:4, :]
    tail_ref[...] = x[tl - SUBLANES:tl, :]

    xcb = xc.astype(BF16)
    r_parts, i_parts = [], []
    for hd in range(LRU_HEADS):
        sl = slice(hd * LRU_HEAD_DIM, (hd + 1) * LRU_HEAD_DIM)
        r_parts.append(_dot(xcb[:, sl], wa_ref[hd]))
        i_parts.append(_dot(xcb[:, sl], wx_ref[hd]))
    r = _sigmoid(jnp.concatenate(r_parts, axis=1) + ba_ref[...])
    ig = _sigmoid(jnp.concatenate(i_parts, axis=1) + bx_ref[...])

    nlam = -lam_ref[...]
    softplus = jnp.maximum(nlam, 0.0) + jnp.log1p(jnp.exp(-jnp.abs(nlam)))
    log_a = (-LRU_C) * r * softplus
    a = jnp.exp(log_a)
    u = jnp.sqrt(jnp.tanh(-log_a) * (1.0 + a * a)) * (ig * xc)

    ones = jnp.ones((SUBLANES, LRU_WIDTH), F32)
    zeros = jnp.zeros((SUBLANES, LRU_WIDTH), F32)
    for s in (1, 2, 4):
        u = a * _shift_rows(u, s, zeros) + u
        a = a * _shift_rows(a, s, ones)
    as_ref[...] = a
    us_ref[...] = u

    def chain(g, hcar):
        rows = pl.ds(pl.multiple_of(g * SUBLANES, SUBLANES), SUBLANES)
        hcar = as_ref[rows, :] * hcar + us_ref[rows, :]
        hs_ref[rows, :] = hcar
        return hcar

    h0 = jnp.broadcast_to(hprev_ref[SUBLANES - 1:SUBLANES, :], (SUBLANES, LRU_WIDTH))
    hprev_ref[...] = lax.fori_loop(0, tl // SUBLANES, chain, h0, unroll=8)

    y = _dot(hn, w_ref[:, LRU_WIDTH:2 * LRU_WIDTH])
    gelu = 0.5 * y * (1.0 + jnp.tanh(math.sqrt(2.0 / math.pi) * (y + 0.044715 * (y * y * y))))
    rec_ref[...] = (hs_ref[...] * gelu).astype(BF16)

    c, s1, s2 = rc_ref[...], rs1_ref[...], rs2_ref[...]
    o = 2 * LRU_WIDTH
    q = _dot(hn, w_ref[:, o:o + B_WIDTH])
    q_ref[...] = (_rope(q, c, s1, s2) * (HEAD_DIM ** -0.5)).astype(BF16)
    k = _dot(hn, w_ref[:, o + B_WIDTH:o + 2 * B_WIDTH])
    k_ref[...] = _rope(k, c, s1, s2).astype(BF16)
    v_ref[...] = _dot(hn, w_ref[:, o + 2 * B_WIDTH:o + 3 * B_WIDTH]).astype(BF16)


def _lru_proj(h, g, w_in, conv_w, conv_b, wa, ba, wx, bx, lam, tables, tl=TOK_TILE):
    bsz, seq, d = h.shape
    tok = lambda w: pl.BlockSpec((None, tl, w), lambda b, i: (b, i, 0))
    out_shape = (jax.ShapeDtypeStruct((bsz, seq, LRU_WIDTH), BF16),) + tuple(
        jax.ShapeDtypeStruct((bsz, seq, B_WIDTH), BF16) for _ in range(3))
    params = (g, w_in, conv_w, conv_b, wa, ba, wx, bx, lam)
    return pl.pallas_call(
        functools.partial(_lru_proj_kernel, tl=tl),
        grid=(bsz, seq // tl),
        in_specs=[tok(d)] + [_layer_spec(a.shape, 0) for a in params] + _table_specs(tl),
        out_specs=(tok(LRU_WIDTH), tok(B_WIDTH), tok(B_WIDTH), tok(B_WIDTH)),
        out_shape=out_shape,
        scratch_shapes=[pltpu.VMEM((tl, d), BF16),
                        pltpu.VMEM((SUBLANES, LRU_WIDTH), F32),
                        pltpu.VMEM((tl, LRU_WIDTH), F32),
                        pltpu.VMEM((tl, LRU_WIDTH), F32),
                        pltpu.VMEM((tl, LRU_WIDTH), F32),
                        pltpu.VMEM((SUBLANES, LRU_WIDTH), F32)],
        compiler_params=_params("arbitrary", "arbitrary"),
        name="lru_proj",
    )(h, *params, tables, tables, tables)


def _sink_column(first):
    return (2 * BLOCK - 1) * (1 - first)


def _fill_band_bias(bias_ref, max_dist, reps, sink_of_rep=None):
    rows = lax.broadcasted_iota(jnp.int32, (BLOCK, 2 * BLOCK), 0)
    cols = lax.broadcasted_iota(jnp.int32, (BLOCK, 2 * BLOCK), 1)
    for first, off in ((0, 0), (1, BLOCK)):
        dist = off + rows - cols
        valid = jnp.logical_and(dist >= 0, dist <= max_dist)
        tile = jnp.where(valid, 0.0, NEG).astype(F32)
        for rep in range(reps):
            rep_tile = tile
            if sink_of_rep is not None:
                assert max_dist < BLOCK
                rep_tile = jnp.where(cols == _sink_column(first), sink_of_rep(rep), tile)
            bias_ref[first, rep * BLOCK:(rep + 1) * BLOCK, :] = rep_tile


def _band_attend(qb, kb, vb, bias, want_lse):
    r = qb.shape[0]
    lane_q = lax.broadcasted_iota(jnp.int32, qb.shape, 1)
    zero = jnp.zeros((), qb.dtype)
    qz = jnp.concatenate([jnp.where(lane_q < HEAD_DIM, qb, zero),
                          jnp.where(lane_q >= HEAD_DIM, qb, zero)], axis=0)
    s = _dot_nt(qz, kb) + bias
    m = jnp.max(s, axis=-1, keepdims=True)
    e = jnp.exp(s - m).astype(vb.dtype)
    ob = _dot(e, jnp.concatenate([vb, jnp.ones_like(vb)], axis=1))
    first_head = lax.broadcasted_iota(jnp.int32, (r, LANES), 1) < HEAD_DIM
    num = jnp.where(first_head, ob[:r, :LANES], ob[r:, :LANES])
    den = jnp.where(first_head, ob[:r, LANES:], ob[r:, LANES:])
    o = num * (1.0 / den)
    if not want_lse:
        return o, None
    return o, jnp.where(first_head, m[:r], m[r:]) + jnp.log(den)


def _band_window(n, nblk):
    first = jnp.minimum(lax.rem(n, nblk), 1)
    q0 = pl.multiple_of(n * BLOCK, BLOCK)
    ks = pl.multiple_of(q0 - first * BLOCK, BLOCK)
    return q0, ks, first


def _dilated_kernel(q_ref, k_ref, v_ref, out_ref, stage_ref, p4_ref, perm_ref, o_ref, l_ref,
                    bias_ref, *, seq, max_dist):
    _fill_band_bias(bias_ref, max_dist, 2)

    l4, l16 = seq // 4, seq // 16
    for a, x_ref in enumerate((q_ref, k_ref, v_ref)):
        stage_ref[...] = x_ref[...].astype(F32)
        for r in range(4):
            blk = stage_ref[pl.ds(r, l4, stride=4), :]
            p4_ref[r * l4:(r + 1) * l4, :] = blk
            perm_ref[0, a, r * l4:(r + 1) * l4, :] = blk.astype(BF16)
        for r in range(4):
            for rb in range(4):
                blk = p4_ref[pl.ds(r * l4 + rb, l16, stride=4), :]
                res = r + 4 * rb
                perm_ref[1, a, res * l16:(res + 1) * l16, :] = blk.astype(BF16)

    branches = ((1, (q_ref, k_ref, v_ref)),
                (4, tuple(perm_ref.at[0, a] for a in range(3))),
                (16, tuple(perm_ref.at[1, a] for a in range(3))))
    for bi, (d, (qr, kr, vr)) in enumerate(branches):
        nblk = seq // d // BLOCK

        def body(n, carry, d=d, bi=bi, qr=qr, kr=kr, vr=vr, nblk=nblk):
            q0, ks, first = _band_window(n, nblk)
            o, lse = _band_attend(qr[pl.ds(q0, BLOCK), :], kr[pl.ds(ks, 2 * BLOCK), :],
                                  vr[pl.ds(ks, 2 * BLOCK), :], bias_ref[first], True)
            if d == 1:
                rows = pl.ds(q0, BLOCK)
            else:
                rows = pl.ds(lax.rem(n, nblk) * (BLOCK * d) + n // nblk, BLOCK, stride=d)
            o_ref.at[bi][rows, :] = o
            l_ref.at[bi][rows, :] = lse
            return carry

        lax.fori_loop(0, seq // BLOCK, body, 0, unroll=16)

    chunk = 4 * BLOCK

    def merge(c, carry):
        rows = pl.ds(pl.multiple_of(c * chunk, chunk), chunk)
        ls = [l_ref[bi, rows, :] for bi in range(3)]
        m = jnp.maximum(jnp.maximum(ls[0], ls[1]), ls[2])
        ws = [jnp.exp(l - m) for l in ls]
        acc = ws[0] * o_ref[0, rows, :] + ws[1] * o_ref[1, rows, :] + ws[2] * o_ref[2, rows, :]
        out_ref[rows, :] = (acc * (1.0 / (ws[0] + ws[1] + ws[2]))).astype(out_ref.dtype)
        return carry

    lax.fori_loop(0, seq // chunk, merge, 0)


def _dilated_attention(q, k, v):
    assert tuple(d for _, d in DILATED_PATTERN) == (1, 4, 16)
    assert len({w // d for w, d in DILATED_PATTERN}) == 1
    max_dist = DILATED_PATTERN[0][0] // DILATED_PATTERN[0][1]
    bsz, seq, w = q.shape
    spec = pl.BlockSpec((None, seq, LANES), lambda b, c: (b, 0, c))
    return pl.pallas_call(
        functools.partial(_dilated_kernel, seq=seq, max_dist=max_dist),
        grid=(bsz, w // LANES),
        in_specs=[spec, spec, spec],
        out_specs=spec,
        out_shape=jax.ShapeDtypeStruct((bsz, seq, w), BF16),
        scratch_shapes=[pltpu.VMEM((seq, LANES), F32), pltpu.VMEM((seq, LANES), F32),
                        pltpu.VMEM((2, 3, seq, LANES), BF16),
                        pltpu.VMEM((3, seq, LANES), F32), pltpu.VMEM((3, seq, LANES), F32),
                        pltpu.VMEM((2, 2 * BLOCK, 2 * BLOCK), F32)],
        compiler_params=_params("arbitrary", "arbitrary"),
        name="dilated",
    )(q, k, v)


def _swa_kernel(sink_ref, q_ref, k_ref, v_ref, o_ref, bias_ref, *, nblk, max_dist, pairs):
    head0 = pl.program_id(1) * (2 * pairs)
    _fill_band_bias(bias_ref, max_dist, 2 * pairs,
                    lambda rep: sink_ref[head0 + 2 * (rep % pairs) + rep // pairs])
    key_row = lax.broadcasted_iota(jnp.int32, (2 * BLOCK, LANES), 0)

    def body(i, carry):
        q0, ks, first = _band_window(i, nblk)
        qrow = q_ref[pl.ds(q0, BLOCK), :]
        qb = jnp.concatenate([qrow[:, p * LANES:(p + 1) * LANES] for p in range(pairs)], axis=0)
        unseen = key_row == _sink_column(first)
        zero = jnp.zeros((), k_ref.dtype)
        kb = jnp.where(unseen, zero, k_ref[pl.ds(ks, 2 * BLOCK), :])
        vb = jnp.where(unseen, zero, v_ref[pl.ds(ks, 2 * BLOCK), :])
        o, _ = _band_attend(qb, kb, vb, bias_ref[first], False)
        o_ref[pl.ds(q0, BLOCK), :] = jnp.concatenate(
            [o[p * BLOCK:(p + 1) * BLOCK, :] for p in range(pairs)], axis=1).astype(o_ref.dtype)
        return carry

    lax.fori_loop(0, nblk, body, 0, unroll=8)


def _swa_attention(q, kdup, vdup, sinks):
    bsz, seq, w = q.shape
    pairs = w // LANES // C_KV_HEADS
    qspec = pl.BlockSpec((None, seq, pairs * LANES), lambda b, g: (b, 0, g))
    kvspec = pl.BlockSpec((None, seq, LANES), lambda b, g: (b, 0, g))
    return pl.pallas_call(
        functools.partial(_swa_kernel, nblk=seq // BLOCK, max_dist=C_WINDOW - 1, pairs=pairs),
        grid=(bsz, C_KV_HEADS),
        in_specs=[pl.BlockSpec(memory_space=pltpu.SMEM), qspec, kvspec, kvspec],
        out_specs=qspec,
        out_shape=jax.ShapeDtypeStruct((bsz, seq, w), BF16),
        scratch_shapes=[pltpu.VMEM((2, 2 * pairs * BLOCK, 2 * BLOCK), F32)],
        compiler_params=_params("arbitrary", "arbitrary"),
        name="swa_sink",
    )(sinks, q, kdup, vdup)


def _dup_heads(a):
    swapped = pltpu.roll(a, HEAD_DIM, 1)
    low = lax.broadcasted_iota(jnp.int32, a.shape, 1) < HEAD_DIM
    return jnp.concatenate([jnp.where(low, a, swapped), jnp.where(low, swapped, a)], axis=1)


def _swa_qkv(h, g_ref, w_ref, b_ref, rc_ref, rs1_ref, rs2_ref, q_ref, k_ref, v_ref):
    hn = _rms(h, g_ref[...]).astype(BF16)
    c, s1, s2 = rc_ref[...], rs1_ref[...], rs2_ref[...]
    ko, vo = C_Q_WIDTH, C_Q_WIDTH + C_KV_WIDTH
    q = _dot(hn, w_ref[:, 0:ko]) + b_ref[:, 0:ko]
    q_ref[...] = (_rope(q, c, s1, s2) * (HEAD_DIM ** -0.5)).astype(BF16)
    kv = _dot(hn, w_ref[:, ko:vo + C_KV_WIDTH]) + b_ref[:, ko:vo + C_KV_WIDTH]
    k_ref[...] = _dup_heads(_rope(kv[:, :C_KV_WIDTH], c, s1, s2)).astype(BF16)
    v_ref[...] = _dup_heads(kv[:, C_KV_WIDTH:]).astype(BF16)


def _mem_kv_kernel(mem_ref, g_ref, w_ref, k_ref, v_ref):
    mn = _rms(mem_ref[...], g_ref[...]).astype(BF16)
    k_ref[...] = _dot(mn, w_ref[:, 0:XA_WIDTH]).astype(BF16)
    v_ref[...] = _dot(mn, w_ref[:, XA_WIDTH:2 * XA_WIDTH]).astype(BF16)


def _mem_kv(mem, g, wkv):
    bsz, m, d = mem.shape
    layers = wkv.shape[0]
    spec = pl.BlockSpec((None, None, m, XA_WIDTH), lambda l, b: (l, b, 0, 0))
    per_layer = lambda a: pl.BlockSpec((None,) + a.shape[1:], lambda l, b: (l, 0, 0))
    return pl.pallas_call(
        _mem_kv_kernel,
        grid=(layers, bsz),
        in_specs=[pl.BlockSpec((None, m, d), lambda l, b: (b, 0, 0)), per_layer(g), per_layer(wkv)],
        out_specs=(spec, spec),
        out_shape=(jax.ShapeDtypeStruct((layers, bsz, m, XA_WIDTH), BF16),) * 2,
        compiler_params=_params("arbitrary", "arbitrary"),
        name="mem_kv",
    )(mem, g, wkv)


def _layer_tail_kernel(*refs, n_act, d_ff, last):
    h_ref, act_refs = refs[0], refs[1:1 + n_act]
    (wout_ref, bout_ref, xg_ref, wq_ref, mk_ref, mv_ref, wo_ref,
     fg_ref, wgu_ref, wd_ref) = refs[1 + n_act:11 + n_act]
    epilogue = refs[11 + n_act:]

    h = h_ref[...] + bout_ref[...]
    row = 0
    for a_ref in act_refs:
        width = a_ref.shape[-1]
        h = h + _dot(a_ref[...], wout_ref[row:row + width, :])
        row += width

    q = _dot(_rms(h, xg_ref[...]).astype(BF16), wq_ref[...]).astype(BF16)
    scale = XA_HEAD_DIM ** -0.5
    heads = []
    for hd in range(XA_HEADS):
        sl = slice(hd * XA_HEAD_DIM, (hd + 1) * XA_HEAD_DIM)
        s = _dot_nt(q[:, sl], mk_ref[:, sl]) * scale
        e = jnp.exp(s - jnp.max(s, axis=-1, keepdims=True))
        den = jnp.sum(e, axis=-1, keepdims=True)
        heads.append((_dot(e.astype(BF16), mv_ref[:, sl]) * (1.0 / den)).astype(BF16))
    h = h + _dot(jnp.concatenate(heads, axis=1), wo_ref[...])

    hn = _rms(h, fg_ref[...]).astype(BF16)
    for c0, width in _ff_chunks(d_ff):
        gate = _dot(hn, wgu_ref[:, c0:c0 + width])
        up = _dot(hn, wgu_ref[:, d_ff + c0:d_ff + c0 + width])
        act = (gate * jax.nn.sigmoid(gate) * up).astype(BF16)
        h = h + _dot(act, wd_ref[c0:c0 + width, :])
    if last:
        ng_ref, out_ref = epilogue
        out_ref[...] = _rms(h, ng_ref[...])
    else:
        out_ref = epilogue[6]
        out_ref[...] = h
        _swa_qkv(h, *epilogue[:6], *epilogue[7:])


def _layer_tail(h, acts, w_out, b_out, mk, mv, p, layer, nxt=None, tm=TOK_TILE):
    bsz, seq, d = h.shape
    d_ff = p["wd"].shape[1]
    assert sum(a.shape[-1] for a in acts) == w_out.shape[1]
    tok = lambda width: pl.BlockSpec((None, tm, width), lambda b, i: (b, i, 0))
    kv = pl.BlockSpec((None, None, N_MEM, XA_WIDTH), lambda b, i: (layer, b, 0, 0))
    res = jax.ShapeDtypeStruct((bsz, seq, d), F32)
    if nxt is None:
        extra, extra_specs = (p["ng"],), [_const_spec(p["ng"].shape)]
        out_specs, out_shape = tok(d), res
    else:
        g2, w2, b2, tables = nxt
        extra = (g2, w2, b2, tables, tables, tables)
        extra_specs = [_layer_spec(a.shape, 0) for a in (g2, w2, b2)] + _table_specs(tm)
        widths = (C_Q_WIDTH, 2 * C_KV_WIDTH, 2 * C_KV_WIDTH)
        out_specs = (tok(d),) + tuple(tok(w) for w in widths)
        out_shape = (res,) + tuple(jax.ShapeDtypeStruct((bsz, seq, w), BF16) for w in widths)
    return pl.pallas_call(
        functools.partial(_layer_tail_kernel, n_act=len(acts), d_ff=d_ff, last=nxt is None),
        grid=(bsz, seq // tm),
        in_specs=[tok(d)] + [tok(a.shape[-1]) for a in acts] + [
            _layer_spec(w_out.shape, 0), _layer_spec(b_out.shape, 0),
            _layer_spec(p["xg"].shape, layer), _layer_spec(p["wq"].shape, layer), kv, kv,
            _layer_spec(p["wo"].shape, layer), _layer_spec(p["fg"].shape, layer),
            _layer_spec(p["wgu"].shape, layer), _layer_spec(p["wd"].shape, layer)] + extra_specs,
        out_specs=out_specs,
        out_shape=out_shape,
        compiler_params=_params("arbitrary", "arbitrary"),
        name="layer_tail",
    )(h, *acts, w_out, b_out, p["xg"], p["wq"], mk, mv, p["wo"], p["fg"], p["wgu"], p["wd"],
      *extra)


def kernel(x, mem, mix_norm, ab_w_in, lru_conv_w, lru_conv_b, lru_wa, lru_ba, lru_wx, lru_bx, lru_lambda, ab_w_out, c_w_qkv, c_b_qkv, c_sinks, c_w_out, c_b_out, xa_norm, xa_mem_norm, xa_wq, xa_wkv, xa_wo, ffn_norm, ffn_w_gate_up, ffn_w_down, final_norm):
    bsz, seq, d = x.shape
    tables = _rope_tables(seq)
    rows = lambda a: a.reshape(a.shape[0], 1, -1)
    bf = lambda a: a.astype(BF16)
    mix_g = rows(mix_norm)
    p = dict(xg=rows(xa_norm), wq=bf(xa_wq), wo=bf(xa_wo), fg=rows(ffn_norm),
             wgu=bf(ffn_w_gate_up), wd=bf(ffn_w_down), ng=final_norm.reshape(1, d))
    mk, mv = _mem_kv(mem, rows(xa_mem_norm), bf(xa_wkv))

    rec, q, k, v = _lru_proj(x, mix_g, bf(ab_w_in), lru_conv_w, rows(lru_conv_b), bf(lru_wa),
                             rows(lru_ba), bf(lru_wx), rows(lru_bx), rows(lru_lambda), tables)
    att = _dilated_attention(q, k, v)
    h, q, k, v = _layer_tail(x, (rec, att), bf(ab_w_out), jnp.zeros((1, 1, d), F32), mk, mv, p,
                             layer=0, nxt=(mix_g[1:2], bf(c_w_qkv), rows(c_b_qkv), tables))

    att = _swa_attention(q, k, v, c_sinks[0])
    return _layer_tail(h, (att,), bf(c_w_out), rows(c_b_out), mk, mv, p, layer=1)
```

```python
import functools
import math

import jax
import jax.numpy as jnp
from jax import lax
from jax.experimental import pallas as pl
from jax.experimental.pallas import tpu as pltpu

F32 = jnp.float32
BF16 = jnp.bfloat16

D_MODEL = 1024
N_MEM = 256
NORM_EPS = 1e-6
ROPE_THETA = 500000.0
HEAD_DIM = 64
ROT_DIM = HEAD_DIM // 4
ROT_HALF = ROT_DIM // 2
LANES = 128
SUBLANES = 8
BLOCK = 128
LRU_WIDTH = D_MODEL
LRU_HEADS = 4
LRU_HEAD_DIM = LRU_WIDTH // LRU_HEADS
CONV_WIDTH = 4
LRU_C = 8.0
B_HEADS = 8
B_WIDTH = B_HEADS * HEAD_DIM
DILATED_PATTERN = ((128, 1), (512, 4), (2048, 16))
C_HEADS = 16
C_KV_HEADS = 2
C_WINDOW = 128
C_Q_WIDTH = C_HEADS * HEAD_DIM
C_KV_WIDTH = C_KV_HEADS * HEAD_DIM
XA_HEADS = 4
XA_HEAD_DIM = 128
XA_WIDTH = XA_HEADS * XA_HEAD_DIM
NEG = -1e30
VMEM_LIMIT = 56 * 1024 * 1024

TOK_TILE = 512
MXU_TILE = 256
FF_CHUNK_TILES = 3
LAST_TAIL_TILE = 1024


def _ff_chunks(d_ff):
    assert d_ff % MXU_TILE == 0
    step = FF_CHUNK_TILES * MXU_TILE
    return [(c0, min(step, d_ff - c0)) for c0 in range(0, d_ff, step)]


def _dot(a, b):
    return jnp.dot(a, b, preferred_element_type=F32)


def _dot_nt(a, b):
    return lax.dot_general(a, b, (((1,), (1,)), ((), ())), preferred_element_type=F32)


def _rms(x, g):
    return x * lax.rsqrt(jnp.mean(x * x, axis=-1, keepdims=True) + NORM_EPS) * g


def _params(*sem):
    return pltpu.CompilerParams(dimension_semantics=sem, vmem_limit_bytes=VMEM_LIMIT)


def _const_spec(shape):
    nd = len(shape)
    return pl.BlockSpec(shape, lambda *_: (0,) * nd, pipeline_mode=pl.Buffered(1))


def _layer_spec(shape, layer):
    nd = len(shape)
    return pl.BlockSpec((None,) + tuple(shape[1:]), lambda *_: (layer,) + (0,) * (nd - 1),
                        pipeline_mode=pl.Buffered(1))


def _rope(x, c, s1, s2):
    cols = []
    for j in range(x.shape[1] // LANES):
        xc = x[:, j * LANES:(j + 1) * LANES]
        cols.append(xc * c + pltpu.roll(xc, ROT_HALF, 1) * s1
                    + pltpu.roll(xc, LANES - ROT_HALF, 1) * s2)
    return cols[0] if len(cols) == 1 else jnp.concatenate(cols, axis=1)


def _rope_tables(seq):
    lane = jnp.arange(LANES) % HEAD_DIM
    inv = ROPE_THETA ** (-(2 * (lane % ROT_HALF)).astype(F32) / ROT_DIM)
    ang = jnp.arange(seq, dtype=F32)[:, None] * inv[None, :]
    cos, sin = jnp.cos(ang), jnp.sin(ang)
    c = jnp.where(lane < ROT_DIM, cos, 1.0)
    s1 = jnp.where(jnp.logical_and(lane >= ROT_HALF, lane < ROT_DIM), sin, 0.0)
    s2 = jnp.where(lane < ROT_HALF, -sin, 0.0)
    return jnp.stack([c, s1, s2])


def _table_specs(tm):
    return [pl.BlockSpec((None, tm, LANES), lambda b, i, j=j: (j, i, 0)) for j in range(3)]


def _sigmoid(x):
    return 0.5 * jnp.tanh(0.5 * x) + 0.5


def _shift_rows(x, k, head):
    rolled = pltpu.roll(x, k, 0)
    sub = lax.broadcasted_iota(jnp.int32, head.shape, 0)
    first = jnp.where(sub < k, pltpu.roll(head, k, 0), rolled[0:SUBLANES])
    return jnp.concatenate([first, rolled[SUBLANES:]], axis=0)


def _lru_proj_kernel(h_ref, g_ref, w_ref, cw_ref, cb_ref, wa_ref, ba_ref, wx_ref, bx_ref,
                     lam_ref, rc_ref, rs1_ref, rs2_ref,
                     rec_ref, q_ref, k_ref, v_ref,
                     hn_ref, tail_ref, as_ref, us_ref, gl_ref, hprev_ref, *, tl):
    @pl.when(pl.program_id(1) == 0)
    def _():
        tail_ref[...] = jnp.zeros_like(tail_ref)
        hprev_ref[...] = jnp.zeros_like(hprev_ref)

    hn_ref[...] = _rms(h_ref[...], g_ref[...]).astype(BF16)
    hn = hn_ref[...]
    c, s1, s2 = rc_ref[...], rs1_ref[...], rs2_ref[...]
    ones = jnp.ones((SUBLANES, LRU_HEAD_DIM), F32)
    zeros = jnp.zeros((SUBLANES, LRU_HEAD_DIM), F32)
    attn_out = (q_ref, k_ref, v_ref)

    group_cols = lambda hd: slice(hd * LRU_HEAD_DIM, (hd + 1) * LRU_HEAD_DIM)
    half = B_WIDTH // 2

    def attn_piece(hd, j):
        o = 2 * LRU_WIDTH + hd * B_WIDTH + j * half
        t = _dot(hn, w_ref[:, o:o + half])
        if hd < 2:
            t = _rope(t, c, s1, s2)
        if hd == 0:
            t = t * (HEAD_DIM ** -0.5)
        attn_out[hd][:, j * half:(j + 1) * half] = t.astype(BF16)

    x_next = _dot(hn, w_ref[:, group_cols(0)])
    for hd in range(LRU_HEADS):
        cols = group_cols(hd)
        x = x_next
        if hd + 1 < LRU_HEADS:
            x_next = _dot(hn, w_ref[:, group_cols(hd + 1)])

        tail = tail_ref[:, cols]
        xc = cb_ref[:, cols] + _shift_rows(x, 3, tail) * cw_ref[0:1, cols]
        xc = xc + _shift_rows(x, 2, tail) * cw_ref[1:2, cols]
        xc = xc + _shift_rows(x, 1, tail) * cw_ref[2:3, cols]
        xc = xc + x * cw_ref[3:4, cols]
        tail_ref[:, cols] = x[tl - SUBLANES:tl, :]

        y = _dot(hn, w_ref[:, LRU_WIDTH + hd * LRU_HEAD_DIM:LRU_WIDTH + (hd + 1) * LRU_HEAD_DIM])

        xcb = xc.astype(BF16)
        r = _sigmoid(_dot(xcb, wa_ref[hd]) + ba_ref[:, cols])
        ig = _sigmoid(_dot(xcb, wx_ref[hd]) + bx_ref[:, cols])
        nlam = -lam_ref[:, cols]
        softplus = jnp.maximum(nlam, 0.0) + jnp.log1p(jnp.exp(-jnp.abs(nlam)))
        log_a = (-LRU_C) * r * softplus
        a = jnp.exp(log_a)
        u = jnp.sqrt(jnp.tanh(-log_a) * (1.0 + a * a)) * (ig * xc)

        if hd < len(attn_out):
            attn_piece(hd, 0)

        for s in (1, 2, 4):
            u = a * _shift_rows(u, s, zeros) + u
            a = a * _shift_rows(a, s, ones)
        as_ref[:, cols] = a
        us_ref[:, cols] = u

        if hd < len(attn_out):
            attn_piece(hd, 1)

        gl_ref[:, cols] = 0.5 * y * (1.0 + jnp.tanh(
            math.sqrt(2.0 / math.pi) * (y + 0.044715 * (y * y * y))))

    def chain(g, hcar):
        rows = pl.ds(pl.multiple_of(g * SUBLANES, SUBLANES), SUBLANES)
        hcar = as_ref[rows, :] * hcar + us_ref[rows, :]
        us_ref[rows, :] = hcar
        return hcar

    h0 = jnp.broadcast_to(hprev_ref[SUBLANES - 1:SUBLANES, :], (SUBLANES, LRU_WIDTH))
    hprev_ref[...] = lax.fori_loop(0, tl // SUBLANES, chain, h0, unroll=8)
    rec_ref[...] = (us_ref[...] * gl_ref[...]).astype(BF16)


def _lru_proj(h, g, w_in, conv_w, conv_b, wa, ba, wx, bx, lam, tables, tl=TOK_TILE):
    bsz, seq, d = h.shape
    tok = lambda w: pl.BlockSpec((None, tl, w), lambda b, i: (b, i, 0))
    out_shape = (jax.ShapeDtypeStruct((bsz, seq, LRU_WIDTH), BF16),) + tuple(
        jax.ShapeDtypeStruct((bsz, seq, B_WIDTH), BF16) for _ in range(3))
    params = (g, w_in, conv_w, conv_b, wa, ba, wx, bx, lam)
    return pl.pallas_call(
        functools.partial(_lru_proj_kernel, tl=tl),
        grid=(bsz, seq // tl),
        in_specs=[tok(d)] + [_layer_spec(a.shape, 0) for a in params] + _table_specs(tl),
        out_specs=(tok(LRU_WIDTH), tok(B_WIDTH), tok(B_WIDTH), tok(B_WIDTH)),
        out_shape=out_shape,
        scratch_shapes=[pltpu.VMEM((tl, d), BF16),
                        pltpu.VMEM((SUBLANES, LRU_WIDTH), F32),
                        pltpu.VMEM((tl, LRU_WIDTH), F32),
                        pltpu.VMEM((tl, LRU_WIDTH), F32),
                        pltpu.VMEM((tl, LRU_WIDTH), F32),
                        pltpu.VMEM((SUBLANES, LRU_WIDTH), F32)],
        compiler_params=_params("arbitrary", "arbitrary"),
        name="lru_proj",
    )(h, *params, tables, tables, tables)


def _sink_column(first):
    return (2 * BLOCK - 1) * (1 - first)


def _fill_band_bias(bias_ref, max_dist, reps, sink_of_rep=None):
    rows = lax.broadcasted_iota(jnp.int32, (BLOCK, 2 * BLOCK), 0)
    cols = lax.broadcasted_iota(jnp.int32, (BLOCK, 2 * BLOCK), 1)
    for first, off in ((0, 0), (1, BLOCK)):
        dist = off + rows - cols
        valid = jnp.logical_and(dist >= 0, dist <= max_dist)
        tile = jnp.where(valid, 0.0, NEG).astype(F32)
        for rep in range(reps):
            rep_tile = tile
            if sink_of_rep is not None:
                assert max_dist < BLOCK
                rep_tile = jnp.where(cols == _sink_column(first), sink_of_rep(rep), tile)
            bias_ref[first, rep * BLOCK:(rep + 1) * BLOCK, :] = rep_tile


def _band_attend(qb, kb, vb, bias, want_lse):
    r = qb.shape[0]
    lane_q = lax.broadcasted_iota(jnp.int32, qb.shape, 1)
    zero = jnp.zeros((), qb.dtype)
    qz = jnp.concatenate([jnp.where(lane_q < HEAD_DIM, qb, zero),
                          jnp.where(lane_q >= HEAD_DIM, qb, zero)], axis=0)
    s = _dot_nt(qz, kb) + bias
    m = jnp.max(s, axis=-1, keepdims=True)
    e = jnp.exp(s - m).astype(vb.dtype)
    ob = _dot(e, jnp.concatenate([vb, jnp.ones_like(vb)], axis=1))
    first_head = lax.broadcasted_iota(jnp.int32, (r, LANES), 1) < HEAD_DIM
    num = jnp.where(first_head, ob[:r, :LANES], ob[r:, :LANES])
    den = jnp.where(first_head, ob[:r, LANES:], ob[r:, LANES:])
    o = num * (1.0 / den)
    if not want_lse:
        return o, None
    return o, jnp.where(first_head, m[:r], m[r:]) + jnp.log(den)


def _band_window(n, nblk):
    first = jnp.minimum(lax.rem(n, nblk), 1)
    q0 = pl.multiple_of(n * BLOCK, BLOCK)
    ks = pl.multiple_of(q0 - first * BLOCK, BLOCK)
    return q0, ks, first


def _dilated_kernel(q_ref, k_ref, v_ref, out_ref, stage_ref, p4_ref, perm_ref, o_ref, l_ref,
                    bias_ref, *, seq, max_dist):
    _fill_band_bias(bias_ref, max_dist, 2)

    l4, l16 = seq // 4, seq // 16
    for a, x_ref in enumerate((q_ref, k_ref, v_ref)):
        stage_ref[...] = x_ref[...].astype(F32)
        for r in range(4):
            blk = stage_ref[pl.ds(r, l4, stride=4), :]
            p4_ref[r * l4:(r + 1) * l4, :] = blk
            perm_ref[0, a, r * l4:(r + 1) * l4, :] = blk.astype(BF16)
        for r in range(4):
            for rb in range(4):
                blk = p4_ref[pl.ds(r * l4 + rb, l16, stride=4), :]
                res = r + 4 * rb
                perm_ref[1, a, res * l16:(res + 1) * l16, :] = blk.astype(BF16)

    branches = ((1, (q_ref, k_ref, v_ref)),
                (4, tuple(perm_ref.at[0, a] for a in range(3))),
                (16, tuple(perm_ref.at[1, a] for a in range(3))))
    for bi, (d, (qr, kr, vr)) in enumerate(branches):
        nblk = seq // d // BLOCK

        def body(n, carry, d=d, bi=bi, qr=qr, kr=kr, vr=vr, nblk=nblk):
            q0, ks, first = _band_window(n, nblk)
            o, lse = _band_attend(qr[pl.ds(q0, BLOCK), :], kr[pl.ds(ks, 2 * BLOCK), :],
                                  vr[pl.ds(ks, 2 * BLOCK), :], bias_ref[first], True)
            if d == 1:
                rows = pl.ds(q0, BLOCK)
            else:
                rows = pl.ds(lax.rem(n, nblk) * (BLOCK * d) + n // nblk, BLOCK, stride=d)
            o_ref.at[bi][rows, :] = o
            l_ref.at[bi][rows, :] = lse
            return carry

        lax.fori_loop(0, seq // BLOCK, body, 0, unroll=16)

    chunk = 4 * BLOCK

    def merge(c, carry):
        rows = pl.ds(pl.multiple_of(c * chunk, chunk), chunk)
        ls = [l_ref[bi, rows, :] for bi in range(3)]
        m = jnp.maximum(jnp.maximum(ls[0], ls[1]), ls[2])
        ws = [jnp.exp(l - m) for l in ls]
        acc = ws[0] * o_ref[0, rows, :] + ws[1] * o_ref[1, rows, :] + ws[2] * o_ref[2, rows, :]
        out_ref[rows, :] = (acc * (1.0 / (ws[0] + ws[1] + ws[2]))).astype(out_ref.dtype)
        return carry

    lax.fori_loop(0, seq // chunk, merge, 0)


def _dilated_attention(q, k, v):
    assert tuple(d for _, d in DILATED_PATTERN) == (1, 4, 16)
    assert len({w // d for w, d in DILATED_PATTERN}) == 1
    max_dist = DILATED_PATTERN[0][0] // DILATED_PATTERN[0][1]
    bsz, seq, w = q.shape
    spec = pl.BlockSpec((None, seq, LANES), lambda b, c: (b, 0, c))
    return pl.pallas_call(
        functools.partial(_dilated_kernel, seq=seq, max_dist=max_dist),
        grid=(bsz, w // LANES),
        in_specs=[spec, spec, spec],
        out_specs=spec,
        out_shape=jax.ShapeDtypeStruct((bsz, seq, w), BF16),
        scratch_shapes=[pltpu.VMEM((seq, LANES), F32), pltpu.VMEM((seq, LANES), F32),
                        pltpu.VMEM((2, 3, seq, LANES), BF16),
                        pltpu.VMEM((3, seq, LANES), F32), pltpu.VMEM((3, seq, LANES), F32),
                        pltpu.VMEM((2, 2 * BLOCK, 2 * BLOCK), F32)],
        compiler_params=_params("arbitrary", "arbitrary"),
        name="dilated",
    )(q, k, v)


def _swa_kernel(sink_ref, q_ref, k_ref, v_ref, o_ref, bias_ref, *, nblk, max_dist, pairs):
    head0 = pl.program_id(1) * (2 * pairs)
    _fill_band_bias(bias_ref, max_dist, 2 * pairs,
                    lambda rep: sink_ref[head0 + 2 * (rep % pairs) + rep // pairs])
    key_row = lax.broadcasted_iota(jnp.int32, (2 * BLOCK, LANES), 0)

    def body(i, carry):
        q0, ks, first = _band_window(i, nblk)
        qrow = q_ref[pl.ds(q0, BLOCK), :]
        qb = jnp.concatenate([qrow[:, p * LANES:(p + 1) * LANES] for p in range(pairs)], axis=0)
        unseen = key_row == _sink_column(first)
        zero = jnp.zeros((), k_ref.dtype)
        kb = jnp.where(unseen, zero, k_ref[pl.ds(ks, 2 * BLOCK), :])
        vb = jnp.where(unseen, zero, v_ref[pl.ds(ks, 2 * BLOCK), :])
        o, _ = _band_attend(qb, kb, vb, bias_ref[first], False)
        o_ref[pl.ds(q0, BLOCK), :] = jnp.concatenate(
            [o[p * BLOCK:(p + 1) * BLOCK, :] for p in range(pairs)], axis=1).astype(o_ref.dtype)
        return carry

    lax.fori_loop(0, nblk, body, 0, unroll=8)


def _swa_attention(q, kdup, vdup, sinks):
    bsz, seq, w = q.shape
    pairs = w // LANES // C_KV_HEADS
    qspec = pl.BlockSpec((None, seq, pairs * LANES), lambda b, g: (b, 0, g))
    kvspec = pl.BlockSpec((None, seq, LANES), lambda b, g: (b, 0, g))
    return pl.pallas_call(
        functools.partial(_swa_kernel, nblk=seq // BLOCK, max_dist=C_WINDOW - 1, pairs=pairs),
        grid=(bsz, C_KV_HEADS),
        in_specs=[pl.BlockSpec(memory_space=pltpu.SMEM), qspec, kvspec, kvspec],
        out_specs=qspec,
        out_shape=jax.ShapeDtypeStruct((bsz, seq, w), BF16),
        scratch_shapes=[pltpu.VMEM((2, 2 * pairs * BLOCK, 2 * BLOCK), F32)],
        compiler_params=_params("arbitrary", "arbitrary"),
        name="swa_sink",
    )(sinks, q, kdup, vdup)


def _dup_heads(a):
    swapped = pltpu.roll(a, HEAD_DIM, 1)
    low = lax.broadcasted_iota(jnp.int32, a.shape, 1) < HEAD_DIM
    return jnp.concatenate([jnp.where(low, a, swapped), jnp.where(low, swapped, a)], axis=1)


def _swa_qkv(h, g_ref, w_ref, b_ref, rc_ref, rs1_ref, rs2_ref, q_ref, k_ref, v_ref):
    hn = _rms(h, g_ref[...]).astype(BF16)
    c, s1, s2 = rc_ref[...], rs1_ref[...], rs2_ref[...]
    ko, vo = C_Q_WIDTH, C_Q_WIDTH + C_KV_WIDTH
    q = _dot(hn, w_ref[:, 0:ko]) + b_ref[:, 0:ko]
    q_ref[...] = (_rope(q, c, s1, s2) * (HEAD_DIM ** -0.5)).astype(BF16)
    kv = _dot(hn, w_ref[:, ko:vo + C_KV_WIDTH]) + b_ref[:, ko:vo + C_KV_WIDTH]
    k_ref[...] = _dup_heads(_rope(kv[:, :C_KV_WIDTH], c, s1, s2)).astype(BF16)
    v_ref[...] = _dup_heads(kv[:, C_KV_WIDTH:]).astype(BF16)


def _mem_kv_kernel(mem_ref, g_ref, w_ref, k_ref, v_ref):
    mn = _rms(mem_ref[...], g_ref[...]).astype(BF16)
    k_ref[...] = _dot(mn, w_ref[:, 0:XA_WIDTH]).astype(BF16)
    v_ref[...] = _dot(mn, w_ref[:, XA_WIDTH:2 * XA_WIDTH]).astype(BF16)


def _mem_kv(mem, g, wkv):
    bsz, m, d = mem.shape
    layers = wkv.shape[0]
    spec = pl.BlockSpec((None, None, m, XA_WIDTH), lambda l, b: (l, b, 0, 0))
    per_layer = lambda a: pl.BlockSpec((None,) + a.shape[1:], lambda l, b: (l, 0, 0))
    return pl.pallas_call(
        _mem_kv_kernel,
        grid=(layers, bsz),
        in_specs=[pl.BlockSpec((None, m, d), lambda l, b: (b, 0, 0)), per_layer(g), per_layer(wkv)],
        out_specs=(spec, spec),
        out_shape=(jax.ShapeDtypeStruct((layers, bsz, m, XA_WIDTH), BF16),) * 2,
        compiler_params=_params("arbitrary", "arbitrary"),
        name="mem_kv",
    )(mem, g, wkv)


def _layer_tail_kernel(*refs, n_act, d_ff, last):
    h_ref, act_refs = refs[0], refs[1:1 + n_act]
    (wout_ref, bout_ref, xg_ref, wq_ref, mk_ref, mv_ref, wo_ref,
     fg_ref, wgu_ref, wd_ref) = refs[1 + n_act:11 + n_act]
    epilogue = refs[11 + n_act:]

    h = h_ref[...] + bout_ref[...]
    row = 0
    for a_ref in act_refs:
        width = a_ref.shape[-1]
        h = h + _dot(a_ref[...], wout_ref[row:row + width, :])
        row += width

    q = _dot(_rms(h, xg_ref[...]).astype(BF16), wq_ref[...]).astype(BF16)
    scale = XA_HEAD_DIM ** -0.5
    heads = []
    for hd in range(XA_HEADS):
        sl = slice(hd * XA_HEAD_DIM, (hd + 1) * XA_HEAD_DIM)
        s = _dot_nt(q[:, sl], mk_ref[:, sl]) * scale
        e = jnp.exp(s - jnp.max(s, axis=-1, keepdims=True))
        den = jnp.sum(e, axis=-1, keepdims=True)
        heads.append((_dot(e.astype(BF16), mv_ref[:, sl]) * (1.0 / den)).astype(BF16))
    h = h + _dot(jnp.concatenate(heads, axis=1), wo_ref[...])

    hn = _rms(h, fg_ref[...]).astype(BF16)
    for c0, width in _ff_chunks(d_ff):
        gate = _dot(hn, wgu_ref[:, c0:c0 + width])
        up = _dot(hn, wgu_ref[:, d_ff + c0:d_ff + c0 + width])
        act = (gate * jax.nn.sigmoid(gate) * up).astype(BF16)
        h = h + _dot(act, wd_ref[c0:c0 + width, :])
    if last:
        ng_ref, out_ref = epilogue
        out_ref[...] = _rms(h, ng_ref[...])
    else:
        out_ref = epilogue[6]
        out_ref[...] = h
        _swa_qkv(h, *epilogue[:6], *epilogue[7:])


def _layer_tail(h, acts, w_out, b_out, mk, mv, p, layer, nxt=None, tm=TOK_TILE):
    bsz, seq, d = h.shape
    d_ff = p["wd"].shape[1]
    assert sum(a.shape[-1] for a in acts) == w_out.shape[1]
    tok = lambda width: pl.BlockSpec((None, tm, width), lambda b, i: (b, i, 0))
    kv = pl.BlockSpec((None, None, N_MEM, XA_WIDTH), lambda b, i: (layer, b, 0, 0))
    res = jax.ShapeDtypeStruct((bsz, seq, d), F32)
    if nxt is None:
        extra, extra_specs = (p["ng"],), [_const_spec(p["ng"].shape)]
        out_specs, out_shape = tok(d), res
    else:
        g2, w2, b2, tables = nxt
        extra = (g2, w2, b2, tables, tables, tables)
        extra_specs = [_layer_spec(a.shape, 0) for a in (g2, w2, b2)] + _table_specs(tm)
        widths = (C_Q_WIDTH, 2 * C_KV_WIDTH, 2 * C_KV_WIDTH)
        out_specs = (tok(d),) + tuple(tok(w) for w in widths)
        out_shape = (res,) + tuple(jax.ShapeDtypeStruct((bsz, seq, w), BF16) for w in widths)
    return pl.pallas_call(
        functools.partial(_layer_tail_kernel, n_act=len(acts), d_ff=d_ff, last=nxt is None),
        grid=(bsz, seq // tm),
        in_specs=[tok(d)] + [tok(a.shape[-1]) for a in acts] + [
            _layer_spec(w_out.shape, 0), _layer_spec(b_out.shape, 0),
            _layer_spec(p["xg"].shape, layer), _layer_spec(p["wq"].shape, layer), kv, kv,
            _layer_spec(p["wo"].shape, layer), _layer_spec(p["fg"].shape, layer),
            _layer_spec(p["wgu"].shape, layer), _layer_spec(p["wd"].shape, layer)] + extra_specs,
        out_specs=out_specs,
        out_shape=out_shape,
        compiler_params=_params("arbitrary", "arbitrary"),
        name="layer_tail",
    )(h, *acts, w_out, b_out, p["xg"], p["wq"], mk, mv, p["wo"], p["fg"], p["wgu"], p["wd"],
      *extra)


def kernel(x, mem, mix_norm, ab_w_in, lru_conv_w, lru_conv_b, lru_wa, lru_ba, lru_wx, lru_bx, lru_lambda, ab_w_out, c_w_qkv, c_b_qkv, c_sinks, c_w_out, c_b_out, xa_norm, xa_mem_norm, xa_wq, xa_wkv, xa_wo, ffn_norm, ffn_w_gate_up, ffn_w_down, final_norm):
    bsz, seq, d = x.shape
    tables = _rope_tables(seq)
    rows = lambda a: a.reshape(a.shape[0], 1, -1)
    bf = lambda a: a.astype(BF16)
    mix_g = rows(mix_norm)
    p = dict(xg=rows(xa_norm), wq=bf(xa_wq), wo=bf(xa_wo), fg=rows(ffn_norm),
             wgu=bf(ffn_w_gate_up), wd=bf(ffn_w_down), ng=final_norm.reshape(1, d))
    mk, mv = _mem_kv(mem, rows(xa_mem_norm), bf(xa_wkv))

    rec, q, k, v = _lru_proj(x, mix_g, bf(ab_w_in), lru_conv_w, rows(lru_conv_b), bf(lru_wa),
                             rows(lru_ba), bf(lru_wx), rows(lru_bx), rows(lru_lambda), tables)
    att = _dilated_attention(q, k, v)
    h, q, k, v = _layer_tail(x, (rec, att), bf(ab_w_out), jnp.zeros((1, 1, d), F32), mk, mv, p,
                             layer=0, nxt=(mix_g[1:2], bf(c_w_qkv), rows(c_b_qkv), tables))

    att = _swa_attention(q, k, v, c_sinks[0])
    return _layer_tail(h, (att,), bf(c_w_out), rows(c_b_out), mk, mv, p, layer=1,
                       tm=LAST_TAIL_TILE)
```

```python
import functools
import math

import jax
import jax.numpy as jnp
from jax import lax
from jax.experimental import pallas as pl
from jax.experimental.pallas import tpu as pltpu

F32 = jnp.float32
BF16 = jnp.bfloat16

D_MODEL = 1024
N_MEM = 256
NORM_EPS = 1e-6
ROPE_THETA = 500000.0
HEAD_DIM = 64
ROT_DIM = HEAD_DIM // 4
ROT_HALF = ROT_DIM // 2
LANES = 128
SUBLANES = 8
BLOCK = 128
LRU_WIDTH = D_MODEL
LRU_HEADS = 4
LRU_HEAD_DIM = LRU_WIDTH // LRU_HEADS
CONV_WIDTH = 4
LRU_C = 8.0
B_HEADS = 8
B_WIDTH = B_HEADS * HEAD_DIM
DILATED_PATTERN = ((128, 1), (512, 4), (2048, 16))
C_HEADS = 16
C_KV_HEADS = 2
C_WINDOW = 128
C_Q_WIDTH = C_HEADS * HEAD_DIM
C_KV_WIDTH = C_KV_HEADS * HEAD_DIM
XA_HEADS = 4
XA_HEAD_DIM = 128
XA_WIDTH = XA_HEADS * XA_HEAD_DIM
NEG = -1e30
VMEM_LIMIT = 56 * 1024 * 1024

TOK_TILE = 512
MXU_TILE = 256
FF_CHUNK_TILES = 3
LRU_TILE = 1024
LAST_TAIL_TILE = 1024


def _ff_chunks(d_ff):
    assert d_ff % MXU_TILE == 0
    step = FF_CHUNK_TILES * MXU_TILE
    return [(c0, min(step, d_ff - c0)) for c0 in range(0, d_ff, step)]


def _dot(a, b):
    return jnp.dot(a, b, preferred_element_type=F32)


def _dot_nt(a, b):
    return lax.dot_general(a, b, (((1,), (1,)), ((), ())), preferred_element_type=F32)


def _rms(x, g):
    return x * lax.rsqrt(jnp.mean(x * x, axis=-1, keepdims=True) + NORM_EPS) * g


def _params(*sem):
    return pltpu.CompilerParams(dimension_semantics=sem, vmem_limit_bytes=VMEM_LIMIT)


def _const_spec(shape):
    nd = len(shape)
    return pl.BlockSpec(shape, lambda *_: (0,) * nd, pipeline_mode=pl.Buffered(1))


def _layer_spec(shape, layer):
    nd = len(shape)
    return pl.BlockSpec((None,) + tuple(shape[1:]), lambda *_: (layer,) + (0,) * (nd - 1),
                        pipeline_mode=pl.Buffered(1))


def _rope(x, c, s1, s2):
    cols = []
    for j in range(x.shape[1] // LANES):
        xc = x[:, j * LANES:(j + 1) * LANES]
        cols.append(xc * c + pltpu.roll(xc, ROT_HALF, 1) * s1
                    + pltpu.roll(xc, LANES - ROT_HALF, 1) * s2)
    return cols[0] if len(cols) == 1 else jnp.concatenate(cols, axis=1)


def _rope_tables(seq):
    lane = jnp.arange(LANES) % HEAD_DIM
    inv = ROPE_THETA ** (-(2 * (lane % ROT_HALF)).astype(F32) / ROT_DIM)
    ang = jnp.arange(seq, dtype=F32)[:, None] * inv[None, :]
    cos, sin = jnp.cos(ang), jnp.sin(ang)
    c = jnp.where(lane < ROT_DIM, cos, 1.0)
    s1 = jnp.where(jnp.logical_and(lane >= ROT_HALF, lane < ROT_DIM), sin, 0.0)
    s2 = jnp.where(lane < ROT_HALF, -sin, 0.0)
    return jnp.stack([c, s1, s2])


def _table_specs(tm):
    return [pl.BlockSpec((None, tm, LANES), lambda b, i, j=j: (j, i, 0)) for j in range(3)]


def _sigmoid(x):
    return 0.5 * jnp.tanh(0.5 * x) + 0.5


def _shift_rows(x, k, head):
    rolled = pltpu.roll(x, k, 0)
    sub = lax.broadcasted_iota(jnp.int32, head.shape, 0)
    first = jnp.where(sub < k, pltpu.roll(head, k, 0), rolled[0:SUBLANES])
    return jnp.concatenate([first, rolled[SUBLANES:]], axis=0)


def _lru_proj_kernel(h_ref, g_ref, w_ref, cw_ref, cb_ref, wa_ref, ba_ref, wx_ref, bx_ref,
                     lam_ref, rc_ref, rs1_ref, rs2_ref,
                     rec_ref, q_ref, k_ref, v_ref,
                     hn_ref, tail_ref, as_ref, us_ref, gl_ref, hprev_ref, *, tl):
    @pl.when(pl.program_id(1) == 0)
    def _():
        tail_ref[...] = jnp.zeros_like(tail_ref)
        hprev_ref[...] = jnp.zeros_like(hprev_ref)

    hn_ref[...] = _rms(h_ref[...], g_ref[...]).astype(BF16)
    hn = hn_ref[...]
    c, s1, s2 = rc_ref[...], rs1_ref[...], rs2_ref[...]
    ones = jnp.ones((SUBLANES, LRU_HEAD_DIM), F32)
    zeros = jnp.zeros((SUBLANES, LRU_HEAD_DIM), F32)
    attn_out = (q_ref, k_ref, v_ref)

    group_cols = lambda hd: slice(hd * LRU_HEAD_DIM, (hd + 1) * LRU_HEAD_DIM)
    half = B_WIDTH // 2

    def attn_piece(hd, j):
        o = 2 * LRU_WIDTH + hd * B_WIDTH + j * half
        t = _dot(hn, w_ref[:, o:o + half])
        if hd < 2:
            t = _rope(t, c, s1, s2)
        if hd == 0:
            t = t * (HEAD_DIM ** -0.5)
        attn_out[hd][:, j * half:(j + 1) * half] = t.astype(BF16)

    x_next = _dot(hn, w_ref[:, group_cols(0)])
    for hd in range(LRU_HEADS):
        cols = group_cols(hd)
        x = x_next
        if hd + 1 < LRU_HEADS:
            x_next = _dot(hn, w_ref[:, group_cols(hd + 1)])

        tail = tail_ref[:, cols]
        xc = cb_ref[:, cols] + _shift_rows(x, 3, tail) * cw_ref[0:1, cols]
        xc = xc + _shift_rows(x, 2, tail) * cw_ref[1:2, cols]
        xc = xc + _shift_rows(x, 1, tail) * cw_ref[2:3, cols]
        xc = xc + x * cw_ref[3:4, cols]
        tail_ref[:, cols] = x[tl - SUBLANES:tl, :]

        y = _dot(hn, w_ref[:, LRU_WIDTH + hd * LRU_HEAD_DIM:LRU_WIDTH + (hd + 1) * LRU_HEAD_DIM])

        xcb = xc.astype(BF16)
        r = _sigmoid(_dot(xcb, wa_ref[hd]) + ba_ref[:, cols])
        ig = _sigmoid(_dot(xcb, wx_ref[hd]) + bx_ref[:, cols])
        nlam = -lam_ref[:, cols]
        softplus = jnp.maximum(nlam, 0.0) + jnp.log1p(jnp.exp(-jnp.abs(nlam)))
        log_a = (-LRU_C) * r * softplus
        a = jnp.exp(log_a)
        u = jnp.sqrt(jnp.tanh(-log_a) * (1.0 + a * a)) * (ig * xc)

        if hd < len(attn_out):
            attn_piece(hd, 0)

        for s in (1, 2, 4):
            u = a * _shift_rows(u, s, zeros) + u
            a = a * _shift_rows(a, s, ones)
        as_ref[:, cols] = a
        us_ref[:, cols] = u

        if hd < len(attn_out):
            attn_piece(hd, 1)

        gl_ref[:, cols] = 0.5 * y * (1.0 + jnp.tanh(
            math.sqrt(2.0 / math.pi) * (y + 0.044715 * (y * y * y))))

    def chain(g, hcar):
        rows = pl.ds(pl.multiple_of(g * SUBLANES, SUBLANES), SUBLANES)
        hcar = as_ref[rows, :] * hcar + us_ref[rows, :]
        us_ref[rows, :] = hcar
        return hcar

    h0 = jnp.broadcast_to(hprev_ref[SUBLANES - 1:SUBLANES, :], (SUBLANES, LRU_WIDTH))
    hprev_ref[...] = lax.fori_loop(0, tl // SUBLANES, chain, h0, unroll=8)
    rec_ref[...] = (us_ref[...] * gl_ref[...]).astype(BF16)


def _lru_proj(h, g, w_in, conv_w, conv_b, wa, ba, wx, bx, lam, tables, tl=LRU_TILE):
    bsz, seq, d = h.shape
    tok = lambda w: pl.BlockSpec((None, tl, w), lambda b, i: (b, i, 0))
    out_shape = (jax.ShapeDtypeStruct((bsz, seq, LRU_WIDTH), BF16),) + tuple(
        jax.ShapeDtypeStruct((bsz, seq, B_WIDTH), BF16) for _ in range(3))
    params = (g, w_in, conv_w, conv_b, wa, ba, wx, bx, lam)
    return pl.pallas_call(
        functools.partial(_lru_proj_kernel, tl=tl),
        grid=(bsz, seq // tl),
        in_specs=[tok(d)] + [_layer_spec(a.shape, 0) for a in params] + _table_specs(tl),
        out_specs=(tok(LRU_WIDTH), tok(B_WIDTH), tok(B_WIDTH), tok(B_WIDTH)),
        out_shape=out_shape,
        scratch_shapes=[pltpu.VMEM((tl, d), BF16),
                        pltpu.VMEM((SUBLANES, LRU_WIDTH), F32),
                        pltpu.VMEM((tl, LRU_WIDTH), F32),
                        pltpu.VMEM((tl, LRU_WIDTH), F32),
                        pltpu.VMEM((tl, LRU_WIDTH), F32),
                        pltpu.VMEM((SUBLANES, LRU_WIDTH), F32)],
        compiler_params=_params("arbitrary", "arbitrary"),
        name="lru_proj",
    )(h, *params, tables, tables, tables)


def _sink_column(first):
    return (2 * BLOCK - 1) * (1 - first)


def _fill_band_bias(bias_ref, max_dist, reps, sink_of_rep=None):
    rows = lax.broadcasted_iota(jnp.int32, (BLOCK, 2 * BLOCK), 0)
    cols = lax.broadcasted_iota(jnp.int32, (BLOCK, 2 * BLOCK), 1)
    for first, off in ((0, 0), (1, BLOCK)):
        dist = off + rows - cols
        valid = jnp.logical_and(dist >= 0, dist <= max_dist)
        tile = jnp.where(valid, 0.0, NEG).astype(F32)
        for rep in range(reps):
            rep_tile = tile
            if sink_of_rep is not None:
                assert max_dist < BLOCK
                rep_tile = jnp.where(cols == _sink_column(first), sink_of_rep(rep), tile)
            bias_ref[first, rep * BLOCK:(rep + 1) * BLOCK, :] = rep_tile


def _band_attend(qb, kb, vb, bias, want_lse):
    r = qb.shape[0]
    lane_q = lax.broadcasted_iota(jnp.int32, qb.shape, 1)
    zero = jnp.zeros((), qb.dtype)
    qz = jnp.concatenate([jnp.where(lane_q < HEAD_DIM, qb, zero),
                          jnp.where(lane_q >= HEAD_DIM, qb, zero)], axis=0)
    s = _dot_nt(qz, kb) + bias
    m = jnp.max(s, axis=-1, keepdims=True)
    e = jnp.exp(s - m).astype(vb.dtype)
    ob = _dot(e, jnp.concatenate([vb, jnp.ones_like(vb)], axis=1))
    first_head = lax.broadcasted_iota(jnp.int32, (r, LANES), 1) < HEAD_DIM
    num = jnp.where(first_head, ob[:r, :LANES], ob[r:, :LANES])
    den = jnp.where(first_head, ob[:r, LANES:], ob[r:, LANES:])
    o = num * (1.0 / den)
    if not want_lse:
        return o, None
    return o, jnp.where(first_head, m[:r], m[r:]) + jnp.log(den)


def _band_window(n, nblk):
    first = jnp.minimum(lax.rem(n, nblk), 1)
    q0 = pl.multiple_of(n * BLOCK, BLOCK)
    ks = pl.multiple_of(q0 - first * BLOCK, BLOCK)
    return q0, ks, first


def _dilated_kernel(q_ref, k_ref, v_ref, out_ref, stage_ref, p4_ref, perm_ref, o_ref, l_ref,
                    bias_ref, *, seq, max_dist):
    _fill_band_bias(bias_ref, max_dist, 2)

    l4, l16 = seq // 4, seq // 16
    for a, x_ref in enumerate((q_ref, k_ref, v_ref)):
        stage_ref[...] = x_ref[...].astype(F32)
        for r in range(4):
            blk = stage_ref[pl.ds(r, l4, stride=4), :]
            p4_ref[r * l4:(r + 1) * l4, :] = blk
            perm_ref[0, a, r * l4:(r + 1) * l4, :] = blk.astype(BF16)
        for r in range(4):
            for rb in range(4):
                blk = p4_ref[pl.ds(r * l4 + rb, l16, stride=4), :]
                res = r + 4 * rb
                perm_ref[1, a, res * l16:(res + 1) * l16, :] = blk.astype(BF16)

    branches = ((1, (q_ref, k_ref, v_ref)),
                (4, tuple(perm_ref.at[0, a] for a in range(3))),
                (16, tuple(perm_ref.at[1, a] for a in range(3))))
    for bi, (d, (qr, kr, vr)) in enumerate(branches):
        nblk = seq // d // BLOCK

        def body(n, carry, d=d, bi=bi, qr=qr, kr=kr, vr=vr, nblk=nblk):
            q0, ks, first = _band_window(n, nblk)
            o, lse = _band_attend(qr[pl.ds(q0, BLOCK), :], kr[pl.ds(ks, 2 * BLOCK), :],
                                  vr[pl.ds(ks, 2 * BLOCK), :], bias_ref[first], True)
            if d == 1:
                rows = pl.ds(q0, BLOCK)
            else:
                rows = pl.ds(lax.rem(n, nblk) * (BLOCK * d) + n // nblk, BLOCK, stride=d)
            o_ref.at[bi][rows, :] = o
            l_ref.at[bi][rows, :] = lse
            return carry

        lax.fori_loop(0, seq // BLOCK, body, 0, unroll=32)

    chunk = 4 * BLOCK

    def merge(c, carry):
        rows = pl.ds(pl.multiple_of(c * chunk, chunk), chunk)
        ls = [l_ref[bi, rows, :] for bi in range(3)]
        m = jnp.maximum(jnp.maximum(ls[0], ls[1]), ls[2])
        ws = [jnp.exp(l - m) for l in ls]
        acc = ws[0] * o_ref[0, rows, :] + ws[1] * o_ref[1, rows, :] + ws[2] * o_ref[2, rows, :]
        out_ref[rows, :] = (acc * (1.0 / (ws[0] + ws[1] + ws[2]))).astype(out_ref.dtype)
        return carry

    lax.fori_loop(0, seq // chunk, merge, 0)


def _dilated_attention(q, k, v):
    assert tuple(d for _, d in DILATED_PATTERN) == (1, 4, 16)
    assert len({w // d for w, d in DILATED_PATTERN}) == 1
    max_dist = DILATED_PATTERN[0][0] // DILATED_PATTERN[0][1]
    bsz, seq, w = q.shape
    spec = pl.BlockSpec((None, seq, LANES), lambda b, c: (b, 0, c))
    return pl.pallas_call(
        functools.partial(_dilated_kernel, seq=seq, max_dist=max_dist),
        grid=(bsz, w // LANES),
        in_specs=[spec, spec, spec],
        out_specs=spec,
        out_shape=jax.ShapeDtypeStruct((bsz, seq, w), BF16),
        scratch_shapes=[pltpu.VMEM((seq, LANES), F32), pltpu.VMEM((seq, LANES), F32),
                        pltpu.VMEM((2, 3, seq, LANES), BF16),
                        pltpu.VMEM((3, seq, LANES), F32), pltpu.VMEM((3, seq, LANES), F32),
                        pltpu.VMEM((2, 2 * BLOCK, 2 * BLOCK), F32)],
        compiler_params=_params("arbitrary", "arbitrary"),
        name="dilated",
    )(q, k, v)


def _swa_kernel(sink_ref, q_ref, k_ref, v_ref, o_ref, bias_ref, *, nblk, max_dist, pairs):
    head0 = pl.program_id(1) * (2 * pairs)
    _fill_band_bias(bias_ref, max_dist, 2 * pairs,
                    lambda rep: sink_ref[head0 + 2 * (rep % pairs) + rep // pairs])
    key_row = lax.broadcasted_iota(jnp.int32, (2 * BLOCK, LANES), 0)

    def body(i, carry):
        q0, ks, first = _band_window(i, nblk)
        qrow = q_ref[pl.ds(q0, BLOCK), :]
        qb = jnp.concatenate([qrow[:, p * LANES:(p + 1) * LANES] for p in range(pairs)], axis=0)
        unseen = key_row == _sink_column(first)
        zero = jnp.zeros((), k_ref.dtype)
        kb = jnp.where(unseen, zero, k_ref[pl.ds(ks, 2 * BLOCK), :])
        vb = jnp.where(unseen, zero, v_ref[pl.ds(ks, 2 * BLOCK), :])
        o, _ = _band_attend(qb, kb, vb, bias_ref[first], False)
        o_ref[pl.ds(q0, BLOCK), :] = jnp.concatenate(
            [o[p * BLOCK:(p + 1) * BLOCK, :] for p in range(pairs)], axis=1).astype(o_ref.dtype)
        return carry

    lax.fori_loop(0, nblk, body, 0, unroll=16)


def _swa_attention(q, kdup, vdup, sinks):
    bsz, seq, w = q.shape
    pairs = w // LANES // C_KV_HEADS
    qspec = pl.BlockSpec((None, seq, pairs * LANES), lambda b, g: (b, 0, g))
    kvspec = pl.BlockSpec((None, seq, LANES), lambda b, g: (b, 0, g))
    return pl.pallas_call(
        functools.partial(_swa_kernel, nblk=seq // BLOCK, max_dist=C_WINDOW - 1, pairs=pairs),
        grid=(bsz, C_KV_HEADS),
        in_specs=[pl.BlockSpec(memory_space=pltpu.SMEM), qspec, kvspec, kvspec],
        out_specs=qspec,
        out_shape=jax.ShapeDtypeStruct((bsz, seq, w), BF16),
        scratch_shapes=[pltpu.VMEM((2, 2 * pairs * BLOCK, 2 * BLOCK), F32)],
        compiler_params=_params("arbitrary", "arbitrary"),
        name="swa_sink",
    )(sinks, q, kdup, vdup)


def _dup_heads(a):
    swapped = pltpu.roll(a, HEAD_DIM, 1)
    low = lax.broadcasted_iota(jnp.int32, a.shape, 1) < HEAD_DIM
    return jnp.concatenate([jnp.where(low, a, swapped), jnp.where(low, swapped, a)], axis=1)


def _swa_qkv(h, g_ref, w_ref, b_ref, rc_ref, rs1_ref, rs2_ref, q_ref, k_ref, v_ref):
    hn = _rms(h, g_ref[...]).astype(BF16)
    c, s1, s2 = rc_ref[...], rs1_ref[...], rs2_ref[...]
    ko, vo = C_Q_WIDTH, C_Q_WIDTH + C_KV_WIDTH
    q = _dot(hn, w_ref[:, 0:ko]) + b_ref[:, 0:ko]
    q_ref[...] = (_rope(q, c, s1, s2) * (HEAD_DIM ** -0.5)).astype(BF16)
    kv = _dot(hn, w_ref[:, ko:vo + C_KV_WIDTH]) + b_ref[:, ko:vo + C_KV_WIDTH]
    k_ref[...] = _dup_heads(_rope(kv[:, :C_KV_WIDTH], c, s1, s2)).astype(BF16)
    v_ref[...] = _dup_heads(kv[:, C_KV_WIDTH:]).astype(BF16)


def _mem_kv_kernel(mem_ref, g_ref, w_ref, k_ref, v_ref):
    mn = _rms(mem_ref[...], g_ref[...]).astype(BF16)
    k_ref[...] = _dot(mn, w_ref[:, 0:XA_WIDTH]).astype(BF16)
    v_ref[...] = _dot(mn, w_ref[:, XA_WIDTH:2 * XA_WIDTH]).astype(BF16)


def _mem_kv(mem, g, wkv):
    bsz, m, d = mem.shape
    layers = wkv.shape[0]
    spec = pl.BlockSpec((None, None, m, XA_WIDTH), lambda l, b: (l, b, 0, 0))
    per_layer = lambda a: pl.BlockSpec((None,) + a.shape[1:], lambda l, b: (l, 0, 0))
    return pl.pallas_call(
        _mem_kv_kernel,
        grid=(layers, bsz),
        in_specs=[pl.BlockSpec((None, m, d), lambda l, b: (b, 0, 0)), per_layer(g), per_layer(wkv)],
        out_specs=(spec, spec),
        out_shape=(jax.ShapeDtypeStruct((layers, bsz, m, XA_WIDTH), BF16),) * 2,
        compiler_params=_params("arbitrary", "arbitrary"),
        name="mem_kv",
    )(mem, g, wkv)


def _layer_tail_kernel(*refs, n_act, d_ff, last):
    h_ref, act_refs = refs[0], refs[1:1 + n_act]
    (wout_ref, bout_ref, xg_ref, wq_ref, mk_ref, mv_ref, wo_ref,
     fg_ref, wgu_ref, wd_ref) = refs[1 + n_act:11 + n_act]
    epilogue = refs[11 + n_act:]

    h = h_ref[...] + bout_ref[...]
    row = 0
    for a_ref in act_refs:
        width = a_ref.shape[-1]
        h = h + _dot(a_ref[...], wout_ref[row:row + width, :])
        row += width

    q = _dot(_rms(h, xg_ref[...]).astype(BF16), wq_ref[...]).astype(BF16)
    scale = XA_HEAD_DIM ** -0.5
    heads = []
    for hd in range(XA_HEADS):
        sl = slice(hd * XA_HEAD_DIM, (hd + 1) * XA_HEAD_DIM)
        s = _dot_nt(q[:, sl], mk_ref[:, sl]) * scale
        e = jnp.exp(s - jnp.max(s, axis=-1, keepdims=True))
        den = jnp.sum(e, axis=-1, keepdims=True)
        heads.append((_dot(e.astype(BF16), mv_ref[:, sl]) * (1.0 / den)).astype(BF16))
    h = h + _dot(jnp.concatenate(heads, axis=1), wo_ref[...])

    hn = _rms(h, fg_ref[...]).astype(BF16)
    for c0, width in _ff_chunks(d_ff):
        gate = _dot(hn, wgu_ref[:, c0:c0 + width])
        up = _dot(hn, wgu_ref[:, d_ff + c0:d_ff + c0 + width])
        act = (gate * jax.nn.sigmoid(gate) * up).astype(BF16)
        h = h + _dot(act, wd_ref[c0:c0 + width, :])
    if last:
        ng_ref, out_ref = epilogue
        out_ref[...] = _rms(h, ng_ref[...])
    else:
        out_ref = epilogue[6]
        out_ref[...] = h
        _swa_qkv(h, *epilogue[:6], *epilogue[7:])


def _layer_tail(h, acts, w_out, b_out, mk, mv, p, layer, nxt=None, tm=TOK_TILE):
    bsz, seq, d = h.shape
    d_ff = p["wd"].shape[1]
    assert sum(a.shape[-1] for a in acts) == w_out.shape[1]
    tok = lambda width: pl.BlockSpec((None, tm, width), lambda b, i: (b, i, 0))
    kv = pl.BlockSpec((None, None, N_MEM, XA_WIDTH), lambda b, i: (layer, b, 0, 0))
    res = jax.ShapeDtypeStruct((bsz, seq, d), F32)
    if nxt is None:
        extra, extra_specs = (p["ng"],), [_const_spec(p["ng"].shape)]
        out_specs, out_shape = tok(d), res
    else:
        g2, w2, b2, tables = nxt
        extra = (g2, w2, b2, tables, tables, tables)
        extra_specs = [_layer_spec(a.shape, 0) for a in (g2, w2, b2)] + _table_specs(tm)
        widths = (C_Q_WIDTH, 2 * C_KV_WIDTH, 2 * C_KV_WIDTH)
        out_specs = (tok(d),) + tuple(tok(w) for w in widths)
        out_shape = (res,) + tuple(jax.ShapeDtypeStruct((bsz, seq, w), BF16) for w in widths)
    return pl.pallas_call(
        functools.partial(_layer_tail_kernel, n_act=len(acts), d_ff=d_ff, last=nxt is None),
        grid=(bsz, seq // tm),
        in_specs=[tok(d)] + [tok(a.shape[-1]) for a in acts] + [
            _layer_spec(w_out.shape, 0), _layer_spec(b_out.shape, 0),
            _layer_spec(p["xg"].shape, layer), _layer_spec(p["wq"].shape, layer), kv, kv,
            _layer_spec(p["wo"].shape, layer), _layer_spec(p["fg"].shape, layer),
            _layer_spec(p["wgu"].shape, layer), _layer_spec(p["wd"].shape, layer)] + extra_specs,
        out_specs=out_specs,
        out_shape=out_shape,
        compiler_params=_params("arbitrary", "arbitrary"),
        name="layer_tail",
    )(h, *acts, w_out, b_out, p["xg"], p["wq"], mk, mv, p["wo"], p["fg"], p["wgu"], p["wd"],
      *extra)


def kernel(x, mem, mix_norm, ab_w_in, lru_conv_w, lru_conv_b, lru_wa, lru_ba, lru_wx, lru_bx, lru_lambda, ab_w_out, c_w_qkv, c_b_qkv, c_sinks, c_w_out, c_b_out, xa_norm, xa_mem_norm, xa_wq, xa_wkv, xa_wo, ffn_norm, ffn_w_gate_up, ffn_w_down, final_norm):
    bsz, seq, d = x.shape
    tables = _rope_tables(seq)
    rows = lambda a: a.reshape(a.shape[0], 1, -1)
    bf = lambda a: a.astype(BF16)
    mix_g = rows(mix_norm)
    p = dict(xg=rows(xa_norm), wq=bf(xa_wq), wo=bf(xa_wo), fg=rows(ffn_norm),
             wgu=bf(ffn_w_gate_up), wd=bf(ffn_w_down), ng=final_norm.reshape(1, d))
    mk, mv = _mem_kv(mem, rows(xa_mem_norm), bf(xa_wkv))

    rec, q, k, v = _lru_proj(x, mix_g, bf(ab_w_in), lru_conv_w, rows(lru_conv_b), bf(lru_wa),
                             rows(lru_ba), bf(lru_wx), rows(lru_bx), rows(lru_lambda), tables)
    att = _dilated_attention(q, k, v)
    h, q, k, v = _layer_tail(x, (rec, att), bf(ab_w_out), jnp.zeros((1, 1, d), F32), mk, mv, p,
                             layer=0, nxt=(mix_g[1:2], bf(c_w_qkv), rows(c_b_qkv), tables))

    att = _swa_attention(q, k, v, c_sinks[0])
    return _layer_tail(h, (att,), bf(c_w_out), rows(c_b_out), mk, mv, p, layer=1,
                       tm=LAST_TAIL_TILE)
```

```python
import functools
import math

import jax
import jax.numpy as jnp
from jax import lax
from jax.experimental import pallas as pl
from jax.experimental.pallas import tpu as pltpu

F32 = jnp.float32
BF16 = jnp.bfloat16

D_MODEL = 1024
N_MEM = 256
NORM_EPS = 1e-6
ROPE_THETA = 500000.0
HEAD_DIM = 64
ROT_DIM = HEAD_DIM // 4
ROT_HALF = ROT_DIM // 2
LANES = 128
SUBLANES = 8
BLOCK = 128
LRU_WIDTH = D_MODEL
LRU_HEADS = 4
LRU_HEAD_DIM = LRU_WIDTH // LRU_HEADS
CONV_WIDTH = 4
LRU_C = 8.0
B_HEADS = 8
B_WIDTH = B_HEADS * HEAD_DIM
DILATED_PATTERN = ((128, 1), (512, 4), (2048, 16))
C_HEADS = 16
C_KV_HEADS = 2
C_WINDOW = 128
C_Q_WIDTH = C_HEADS * HEAD_DIM
C_KV_WIDTH = C_KV_HEADS * HEAD_DIM
XA_HEADS = 4
XA_HEAD_DIM = 128
XA_WIDTH = XA_HEADS * XA_HEAD_DIM
NEG = -1e30
VMEM_LIMIT = 56 * 1024 * 1024

TOK_TILE = 512
MXU_TILE = 256
FF_CHUNK_TILES = 3
LRU_TILE = 1024
LAST_TAIL_TILE = 1024


def _ff_chunks(d_ff):
    assert d_ff % MXU_TILE == 0
    step = FF_CHUNK_TILES * MXU_TILE
    return [(c0, min(step, d_ff - c0)) for c0 in range(0, d_ff, step)]


def _dot(a, b):
    return jnp.dot(a, b, preferred_element_type=F32)


def _dot_nt(a, b):
    return lax.dot_general(a, b, (((1,), (1,)), ((), ())), preferred_element_type=F32)


def _rms(x, g):
    return x * lax.rsqrt(jnp.mean(x * x, axis=-1, keepdims=True) + NORM_EPS) * g


def _params(*sem):
    return pltpu.CompilerParams(dimension_semantics=sem, vmem_limit_bytes=VMEM_LIMIT)


def _const_spec(shape):
    nd = len(shape)
    return pl.BlockSpec(shape, lambda *_: (0,) * nd, pipeline_mode=pl.Buffered(1))


def _layer_spec(shape, layer):
    nd = len(shape)
    return pl.BlockSpec((None,) + tuple(shape[1:]), lambda *_: (layer,) + (0,) * (nd - 1),
                        pipeline_mode=pl.Buffered(1))


def _rope(x, c, s1, s2):
    cols = []
    for j in range(x.shape[1] // LANES):
        xc = x[:, j * LANES:(j + 1) * LANES]
        cols.append(xc * c + pltpu.roll(xc, ROT_HALF, 1) * s1
                    + pltpu.roll(xc, LANES - ROT_HALF, 1) * s2)
    return cols[0] if len(cols) == 1 else jnp.concatenate(cols, axis=1)


def _rope_tables(seq):
    lane = jnp.arange(LANES) % HEAD_DIM
    inv = ROPE_THETA ** (-(2 * (lane % ROT_HALF)).astype(F32) / ROT_DIM)
    ang = jnp.arange(seq, dtype=F32)[:, None] * inv[None, :]
    cos, sin = jnp.cos(ang), jnp.sin(ang)
    c = jnp.where(lane < ROT_DIM, cos, 1.0)
    s1 = jnp.where(jnp.logical_and(lane >= ROT_HALF, lane < ROT_DIM), sin, 0.0)
    s2 = jnp.where(lane < ROT_HALF, -sin, 0.0)
    return jnp.stack([c, s1, s2])


def _table_specs(tm):
    return [pl.BlockSpec((None, tm, LANES), lambda b, i, j=j: (j, i, 0)) for j in range(3)]


def _sigmoid(x):
    return 0.5 * jnp.tanh(0.5 * x) + 0.5


def _shift_rows(x, k, head):
    rolled = pltpu.roll(x, k, 0)
    sub = lax.broadcasted_iota(jnp.int32, head.shape, 0)
    first = jnp.where(sub < k, pltpu.roll(head, k, 0), rolled[0:SUBLANES])
    return jnp.concatenate([first, rolled[SUBLANES:]], axis=0)


def _lru_proj_kernel(h_ref, g_ref, w_ref, cw_ref, cb_ref, wa_ref, ba_ref, wx_ref, bx_ref,
                     lam_ref, rc_ref, rs1_ref, rs2_ref,
                     rec_ref, q_ref, k_ref, v_ref,
                     hn_ref, tail_ref, as_ref, us_ref, gl_ref, hprev_ref, *, tl):
    @pl.when(pl.program_id(1) == 0)
    def _():
        tail_ref[...] = jnp.zeros_like(tail_ref)
        hprev_ref[...] = jnp.zeros_like(hprev_ref)

    hn_ref[...] = _rms(h_ref[...], g_ref[...]).astype(BF16)
    hn = hn_ref[...]
    c, s1, s2 = rc_ref[...], rs1_ref[...], rs2_ref[...]
    ones = jnp.ones((SUBLANES, LRU_HEAD_DIM), F32)
    zeros = jnp.zeros((SUBLANES, LRU_HEAD_DIM), F32)
    attn_out = (q_ref, k_ref, v_ref)

    group_cols = lambda hd: slice(hd * LRU_HEAD_DIM, (hd + 1) * LRU_HEAD_DIM)
    half = B_WIDTH // 2

    def attn_piece(hd, j):
        o = 2 * LRU_WIDTH + hd * B_WIDTH + j * half
        t = _dot(hn, w_ref[:, o:o + half])
        if hd < 2:
            t = _rope(t, c, s1, s2)
        if hd == 0:
            t = t * (HEAD_DIM ** -0.5)
        attn_out[hd][:, j * half:(j + 1) * half] = t.astype(BF16)

    x_next = _dot(hn, w_ref[:, group_cols(0)])
    for hd in range(LRU_HEADS):
        cols = group_cols(hd)
        x = x_next
        if hd + 1 < LRU_HEADS:
            x_next = _dot(hn, w_ref[:, group_cols(hd + 1)])

        tail = tail_ref[:, cols]
        xc = cb_ref[:, cols] + _shift_rows(x, 3, tail) * cw_ref[0:1, cols]
        xc = xc + _shift_rows(x, 2, tail) * cw_ref[1:2, cols]
        xc = xc + _shift_rows(x, 1, tail) * cw_ref[2:3, cols]
        xc = xc + x * cw_ref[3:4, cols]
        tail_ref[:, cols] = x[tl - SUBLANES:tl, :]

        y = _dot(hn, w_ref[:, LRU_WIDTH + hd * LRU_HEAD_DIM:LRU_WIDTH + (hd + 1) * LRU_HEAD_DIM])

        xcb = xc.astype(BF16)
        r = _sigmoid(_dot(xcb, wa_ref[hd]) + ba_ref[:, cols])
        ig = _sigmoid(_dot(xcb, wx_ref[hd]) + bx_ref[:, cols])
        nlam = -lam_ref[:, cols]
        softplus = jnp.maximum(nlam, 0.0) + jnp.log1p(jnp.exp(-jnp.abs(nlam)))
        neg_log_a = r * (LRU_C * softplus)
        a = jnp.exp2(r * ((-LRU_C * math.log2(math.e)) * softplus))
        u = jnp.sqrt(jnp.tanh(neg_log_a) * (1.0 + a * a)) * (ig * xc)

        if hd < len(attn_out):
            attn_piece(hd, 0)

        for s in (1, 2, 4):
            u = a * _shift_rows(u, s, zeros) + u
            a = a * _shift_rows(a, s, ones)
        as_ref[:, cols] = a
        us_ref[:, cols] = u

        if hd < len(attn_out):
            attn_piece(hd, 1)

        gl_ref[:, cols] = 0.5 * y * (1.0 + jnp.tanh(
            math.sqrt(2.0 / math.pi) * (y + 0.044715 * (y * y * y))))

    def chain(g, hcar):
        rows = pl.ds(pl.multiple_of(g * SUBLANES, SUBLANES), SUBLANES)
        hcar = as_ref[rows, :] * hcar + us_ref[rows, :]
        us_ref[rows, :] = hcar
        return hcar

    h0 = jnp.broadcast_to(hprev_ref[SUBLANES - 1:SUBLANES, :], (SUBLANES, LRU_WIDTH))
    hprev_ref[...] = lax.fori_loop(0, tl // SUBLANES, chain, h0, unroll=8)
    rec_ref[...] = (us_ref[...] * gl_ref[...]).astype(BF16)


def _lru_proj(h, g, w_in, conv_w, conv_b, wa, ba, wx, bx, lam, tables, tl=LRU_TILE):
    bsz, seq, d = h.shape
    tok = lambda w: pl.BlockSpec((None, tl, w), lambda b, i: (b, i, 0))
    out_shape = (jax.ShapeDtypeStruct((bsz, seq, LRU_WIDTH), BF16),) + tuple(
        jax.ShapeDtypeStruct((bsz, seq, B_WIDTH), BF16) for _ in range(3))
    params = (g, w_in, conv_w, conv_b, wa, ba, wx, bx, lam)
    return pl.pallas_call(
        functools.partial(_lru_proj_kernel, tl=tl),
        grid=(bsz, seq // tl),
        in_specs=[tok(d)] + [_layer_spec(a.shape, 0) for a in params] + _table_specs(tl),
        out_specs=(tok(LRU_WIDTH), tok(B_WIDTH), tok(B_WIDTH), tok(B_WIDTH)),
        out_shape=out_shape,
        scratch_shapes=[pltpu.VMEM((tl, d), BF16),
                        pltpu.VMEM((SUBLANES, LRU_WIDTH), F32),
                        pltpu.VMEM((tl, LRU_WIDTH), F32),
                        pltpu.VMEM((tl, LRU_WIDTH), F32),
                        pltpu.VMEM((tl, LRU_WIDTH), F32),
                        pltpu.VMEM((SUBLANES, LRU_WIDTH), F32)],
        compiler_params=_params("arbitrary", "arbitrary"),
        name="lru_proj",
    )(h, *params, tables, tables, tables)


def _sink_column(first):
    return (2 * BLOCK - 1) * (1 - first)


def _fill_band_bias(bias_ref, max_dist, reps, sink_of_rep=None):
    rows = lax.broadcasted_iota(jnp.int32, (BLOCK, 2 * BLOCK), 0)
    cols = lax.broadcasted_iota(jnp.int32, (BLOCK, 2 * BLOCK), 1)
    for first, off in ((0, 0), (1, BLOCK)):
        dist = off + rows - cols
        valid = jnp.logical_and(dist >= 0, dist <= max_dist)
        tile = jnp.where(valid, 0.0, NEG).astype(F32)
        for rep in range(reps):
            rep_tile = tile
            if sink_of_rep is not None:
                assert max_dist < BLOCK
                rep_tile = jnp.where(cols == _sink_column(first), sink_of_rep(rep), tile)
            bias_ref[first, rep * BLOCK:(rep + 1) * BLOCK, :] = rep_tile


def _band_attend(qb, kb, vb, bias, want_lse):
    r = qb.shape[0]
    lane_q = lax.broadcasted_iota(jnp.int32, qb.shape, 1)
    zero = jnp.zeros((), qb.dtype)
    qz = jnp.concatenate([jnp.where(lane_q < HEAD_DIM, qb, zero),
                          jnp.where(lane_q >= HEAD_DIM, qb, zero)], axis=0)
    s = _dot_nt(qz, kb) + bias
    m = jnp.max(s, axis=-1, keepdims=True)
    e = jnp.exp(s - m).astype(vb.dtype)
    ob = _dot(e, jnp.concatenate([vb, jnp.ones_like(vb)], axis=1))
    first_head = lax.broadcasted_iota(jnp.int32, (r, LANES), 1) < HEAD_DIM
    num = jnp.where(first_head, ob[:r, :LANES], ob[r:, :LANES])
    den = jnp.where(first_head, ob[:r, LANES:], ob[r:, LANES:])
    o = num * (1.0 / den)
    if not want_lse:
        return o, None
    return o, jnp.where(first_head, m[:r], m[r:]) + jnp.log(den)


def _band_window(n, nblk):
    first = jnp.minimum(lax.rem(n, nblk), 1)
    q0 = pl.multiple_of(n * BLOCK, BLOCK)
    ks = pl.multiple_of(q0 - first * BLOCK, BLOCK)
    return q0, ks, first


def _dilated_kernel(q_ref, k_ref, v_ref, out_ref, stage_ref, p4_ref, perm_ref, o_ref, l_ref,
                    bias_ref, *, seq, max_dist):
    _fill_band_bias(bias_ref, max_dist, 2)

    l4, l16 = seq // 4, seq // 16
    for a, x_ref in enumerate((q_ref, k_ref, v_ref)):
        stage_ref[...] = x_ref[...].astype(F32)
        for r in range(4):
            blk = stage_ref[pl.ds(r, l4, stride=4), :]
            p4_ref[r * l4:(r + 1) * l4, :] = blk
            perm_ref[0, a, r * l4:(r + 1) * l4, :] = blk.astype(BF16)
        for r in range(4):
            for rb in range(4):
                blk = p4_ref[pl.ds(r * l4 + rb, l16, stride=4), :]
                res = r + 4 * rb
                perm_ref[1, a, res * l16:(res + 1) * l16, :] = blk.astype(BF16)

    branches = ((1, (q_ref, k_ref, v_ref)),
                (4, tuple(perm_ref.at[0, a] for a in range(3))),
                (16, tuple(perm_ref.at[1, a] for a in range(3))))
    for bi, (d, (qr, kr, vr)) in enumerate(branches):
        nblk = seq // d // BLOCK

        def body(n, carry, d=d, bi=bi, qr=qr, kr=kr, vr=vr, nblk=nblk):
            q0, ks, first = _band_window(n, nblk)
            o, lse = _band_attend(qr[pl.ds(q0, BLOCK), :], kr[pl.ds(ks, 2 * BLOCK), :],
                                  vr[pl.ds(ks, 2 * BLOCK), :], bias_ref[first], True)
            if d == 1:
                rows = pl.ds(q0, BLOCK)
            else:
                rows = pl.ds(lax.rem(n, nblk) * (BLOCK * d) + n // nblk, BLOCK, stride=d)
            o_ref.at[bi][rows, :] = o
            l_ref.at[bi][rows, :] = lse
            return carry

        lax.fori_loop(0, seq // BLOCK, body, 0, unroll=32)

    chunk = 4 * BLOCK

    def merge(c, carry):
        rows = pl.ds(pl.multiple_of(c * chunk, chunk), chunk)
        ls = [l_ref[bi, rows, :] for bi in range(3)]
        m = jnp.maximum(jnp.maximum(ls[0], ls[1]), ls[2])
        ws = [jnp.exp(l - m) for l in ls]
        acc = ws[0] * o_ref[0, rows, :] + ws[1] * o_ref[1, rows, :] + ws[2] * o_ref[2, rows, :]
        out_ref[rows, :] = (acc * (1.0 / (ws[0] + ws[1] + ws[2]))).astype(out_ref.dtype)
        return carry

    lax.fori_loop(0, seq // chunk, merge, 0)


def _dilated_attention(q, k, v):
    assert tuple(d for _, d in DILATED_PATTERN) == (1, 4, 16)
    assert len({w // d for w, d in DILATED_PATTERN}) == 1
    max_dist = DILATED_PATTERN[0][0] // DILATED_PATTERN[0][1]
    bsz, seq, w = q.shape
    spec = pl.BlockSpec((None, seq, LANES), lambda b, c: (b, 0, c))
    return pl.pallas_call(
        functools.partial(_dilated_kernel, seq=seq, max_dist=max_dist),
        grid=(bsz, w // LANES),
        in_specs=[spec, spec, spec],
        out_specs=spec,
        out_shape=jax.ShapeDtypeStruct((bsz, seq, w), BF16),
        scratch_shapes=[pltpu.VMEM((seq, LANES), F32), pltpu.VMEM((seq, LANES), F32),
                        pltpu.VMEM((2, 3, seq, LANES), BF16),
                        pltpu.VMEM((3, seq, LANES), F32), pltpu.VMEM((3, seq, LANES), F32),
                        pltpu.VMEM((2, 2 * BLOCK, 2 * BLOCK), F32)],
        compiler_params=_params("arbitrary", "arbitrary"),
        name="dilated",
    )(q, k, v)


def _swa_kernel(sink_ref, q_ref, k_ref, v_ref, o_ref, bias_ref, *, nblk, max_dist, pairs):
    head0 = pl.program_id(1) * (2 * pairs)
    _fill_band_bias(bias_ref, max_dist, 2 * pairs,
                    lambda rep: sink_ref[head0 + 2 * (rep % pairs) + rep // pairs])
    key_row = lax.broadcasted_iota(jnp.int32, (2 * BLOCK, LANES), 0)

    def body(i, carry):
        q0, ks, first = _band_window(i, nblk)
        qrow = q_ref[pl.ds(q0, BLOCK), :]
        qb = jnp.concatenate([qrow[:, p * LANES:(p + 1) * LANES] for p in range(pairs)], axis=0)
        unseen = key_row == _sink_column(first)
        zero = jnp.zeros((), k_ref.dtype)
        kb = jnp.where(unseen, zero, k_ref[pl.ds(ks, 2 * BLOCK), :])
        vb = jnp.where(unseen, zero, v_ref[pl.ds(ks, 2 * BLOCK), :])
        o, _ = _band_attend(qb, kb, vb, bias_ref[first], False)
        o_ref[pl.ds(q0, BLOCK), :] = jnp.concatenate(
            [o[p * BLOCK:(p + 1) * BLOCK, :] for p in range(pairs)], axis=1).astype(o_ref.dtype)
        return carry

    lax.fori_loop(0, nblk, body, 0, unroll=16)


def _swa_attention(q, kdup, vdup, sinks):
    bsz, seq, w = q.shape
    pairs = w // LANES // C_KV_HEADS
    qspec = pl.BlockSpec((None, seq, pairs * LANES), lambda b, g: (b, 0, g))
    kvspec = pl.BlockSpec((None, seq, LANES), lambda b, g: (b, 0, g))
    return pl.pallas_call(
        functools.partial(_swa_kernel, nblk=seq // BLOCK, max_dist=C_WINDOW - 1, pairs=pairs),
        grid=(bsz, C_KV_HEADS),
        in_specs=[pl.BlockSpec(memory_space=pltpu.SMEM), qspec, kvspec, kvspec],
        out_specs=qspec,
        out_shape=jax.ShapeDtypeStruct((bsz, seq, w), BF16),
        scratch_shapes=[pltpu.VMEM((2, 2 * pairs * BLOCK, 2 * BLOCK), F32)],
        compiler_params=_params("arbitrary", "arbitrary"),
        name="swa_sink",
    )(sinks, q, kdup, vdup)


def _dup_heads(a):
    swapped = pltpu.roll(a, HEAD_DIM, 1)
    low = lax.broadcasted_iota(jnp.int32, a.shape, 1) < HEAD_DIM
    return jnp.concatenate([jnp.where(low, a, swapped), jnp.where(low, swapped, a)], axis=1)


def _swa_qkv(h, g_ref, w_ref, b_ref, rc_ref, rs1_ref, rs2_ref, q_ref, k_ref, v_ref):
    hn = _rms(h, g_ref[...]).astype(BF16)
    c, s1, s2 = rc_ref[...], rs1_ref[...], rs2_ref[...]
    ko, vo = C_Q_WIDTH, C_Q_WIDTH + C_KV_WIDTH
    q = _dot(hn, w_ref[:, 0:ko]) + b_ref[:, 0:ko]
    q_ref[...] = (_rope(q, c, s1, s2) * (HEAD_DIM ** -0.5)).astype(BF16)
    kv = _dot(hn, w_ref[:, ko:vo + C_KV_WIDTH]) + b_ref[:, ko:vo + C_KV_WIDTH]
    k_ref[...] = _dup_heads(_rope(kv[:, :C_KV_WIDTH], c, s1, s2)).astype(BF16)
    v_ref[...] = _dup_heads(kv[:, C_KV_WIDTH:]).astype(BF16)


def _mem_kv_kernel(mem_ref, g_ref, w_ref, k_ref, v_ref):
    mn = _rms(mem_ref[...], g_ref[...]).astype(BF16)
    k_ref[...] = _dot(mn, w_ref[:, 0:XA_WIDTH]).astype(BF16)
    v_ref[...] = _dot(mn, w_ref[:, XA_WIDTH:2 * XA_WIDTH]).astype(BF16)


def _mem_kv(mem, g, wkv):
    bsz, m, d = mem.shape
    layers = wkv.shape[0]
    spec = pl.BlockSpec((None, bsz * m, XA_WIDTH), lambda l: (l, 0, 0))
    per_layer = lambda a: pl.BlockSpec((None,) + a.shape[1:], lambda l: (l, 0, 0))
    return pl.pallas_call(
        _mem_kv_kernel,
        grid=(layers,),
        in_specs=[pl.BlockSpec((bsz * m, d), lambda l: (0, 0)), per_layer(g), per_layer(wkv)],
        out_specs=(spec, spec),
        out_shape=(jax.ShapeDtypeStruct((layers, bsz * m, XA_WIDTH), BF16),) * 2,
        compiler_params=_params("arbitrary"),
        name="mem_kv",
    )(mem.reshape(bsz * m, d), g, wkv)


def _layer_tail_kernel(*refs, n_act, d_ff, last):
    h_ref, act_refs = refs[0], refs[1:1 + n_act]
    (wout_ref, bout_ref, xg_ref, wq_ref, mk_ref, mv_ref, wo_ref,
     fg_ref, wgu_ref, wd_ref) = refs[1 + n_act:11 + n_act]
    epilogue = refs[11 + n_act:]

    h = h_ref[...] + bout_ref[...]
    row = 0
    for a_ref in act_refs:
        width = a_ref.shape[-1]
        h = h + _dot(a_ref[...], wout_ref[row:row + width, :])
        row += width

    q = _dot(_rms(h, xg_ref[...]).astype(BF16), wq_ref[...]).astype(BF16)
    scale = XA_HEAD_DIM ** -0.5
    heads = []
    for hd in range(XA_HEADS):
        sl = slice(hd * XA_HEAD_DIM, (hd + 1) * XA_HEAD_DIM)
        s = _dot_nt(q[:, sl], mk_ref[:, sl]) * scale
        e = jnp.exp(s - jnp.max(s, axis=-1, keepdims=True))
        den = jnp.sum(e, axis=-1, keepdims=True)
        heads.append((_dot(e.astype(BF16), mv_ref[:, sl]) * (1.0 / den)).astype(BF16))
    h = h + _dot(jnp.concatenate(heads, axis=1), wo_ref[...])

    hn = _rms(h, fg_ref[...]).astype(BF16)
    for c0, width in _ff_chunks(d_ff):
        gate = _dot(hn, wgu_ref[:, c0:c0 + width])
        up = _dot(hn, wgu_ref[:, d_ff + c0:d_ff + c0 + width])
        act = (gate * jax.nn.sigmoid(gate) * up).astype(BF16)
        h = h + _dot(act, wd_ref[c0:c0 + width, :])
    if last:
        ng_ref, out_ref = epilogue
        out_ref[...] = _rms(h, ng_ref[...])
    else:
        out_ref = epilogue[6]
        out_ref[...] = h
        _swa_qkv(h, *epilogue[:6], *epilogue[7:])


def _layer_tail(h, acts, w_out, b_out, mk, mv, p, layer, nxt=None, tm=TOK_TILE):
    bsz, seq, d = h.shape
    d_ff = p["wd"].shape[1]
    assert sum(a.shape[-1] for a in acts) == w_out.shape[1]
    tok = lambda width: pl.BlockSpec((None, tm, width), lambda b, i: (b, i, 0))
    kv = pl.BlockSpec((None, N_MEM, XA_WIDTH), lambda b, i: (layer, b, 0))
    res = jax.ShapeDtypeStruct((bsz, seq, d), F32)
    if nxt is None:
        extra, extra_specs = (p["ng"],), [_const_spec(p["ng"].shape)]
        out_specs, out_shape = tok(d), res
    else:
        g2, w2, b2, tables = nxt
        extra = (g2, w2, b2, tables, tables, tables)
        extra_specs = [_layer_spec(a.shape, 0) for a in (g2, w2, b2)] + _table_specs(tm)
        widths = (C_Q_WIDTH, 2 * C_KV_WIDTH, 2 * C_KV_WIDTH)
        out_specs = (tok(d),) + tuple(tok(w) for w in widths)
        out_shape = (res,) + tuple(jax.ShapeDtypeStruct((bsz, seq, w), BF16) for w in widths)
    return pl.pallas_call(
        functools.partial(_layer_tail_kernel, n_act=len(acts), d_ff=d_ff, last=nxt is None),
        grid=(bsz, seq // tm),
        in_specs=[tok(d)] + [tok(a.shape[-1]) for a in acts] + [
            _layer_spec(w_out.shape, 0), _layer_spec(b_out.shape, 0),
            _layer_spec(p["xg"].shape, layer), _layer_spec(p["wq"].shape, layer), kv, kv,
            _layer_spec(p["wo"].shape, layer), _layer_spec(p["fg"].shape, layer),
            _layer_spec(p["wgu"].shape, layer), _layer_spec(p["wd"].shape, layer)] + extra_specs,
        out_specs=out_specs,
        out_shape=out_shape,
        compiler_params=_params("arbitrary", "arbitrary"),
        name="layer_tail",
    )(h, *acts, w_out, b_out, p["xg"], p["wq"], mk, mv, p["wo"], p["fg"], p["wgu"], p["wd"],
      *extra)


def kernel(x, mem, mix_norm, ab_w_in, lru_conv_w, lru_conv_b, lru_wa, lru_ba, lru_wx, lru_bx, lru_lambda, ab_w_out, c_w_qkv, c_b_qkv, c_sinks, c_w_out, c_b_out, xa_norm, xa_mem_norm, xa_wq, xa_wkv, xa_wo, ffn_norm, ffn_w_gate_up, ffn_w_down, final_norm):
    bsz, seq, d = x.shape
    tables = _rope_tables(seq)
    rows = lambda a: a.reshape(a.shape[0], 1, -1)
    bf = lambda a: a.astype(BF16)
    mix_g = rows(mix_norm)
    p = dict(xg=rows(xa_norm), wq=bf(xa_wq), wo=bf(xa_wo), fg=rows(ffn_norm),
             wgu=bf(ffn_w_gate_up), wd=bf(ffn_w_down), ng=final_norm.reshape(1, d))
    mk, mv = _mem_kv(mem, rows(xa_mem_norm), bf(xa_wkv))

    rec, q, k, v = _lru_proj(x, mix_g, bf(ab_w_in), lru_conv_w, rows(lru_conv_b), bf(lru_wa),
                             rows(lru_ba), bf(lru_wx), rows(lru_bx), rows(lru_lambda), tables)
    att = _dilated_attention(q, k, v)
    h, q, k, v = _layer_tail(x, (rec, att), bf(ab_w_out), jnp.zeros((1, 1, d), F32), mk, mv, p,
                             layer=0, nxt=(mix_g[1:2], bf(c_w_qkv), rows(c_b_qkv), tables))

    att = _swa_attention(q, k, v, c_sinks[0])
    return _layer_tail(h, (att,), bf(c_w_out), rows(c_b_out), mk, mv, p, layer=1,
                       tm=LAST_TAIL_TILE)
```

```python
import functools
import math

import jax
import jax.numpy as jnp
from jax import lax
from jax.experimental import pallas as pl
from jax.experimental.pallas import tpu as pltpu

F32 = jnp.float32
BF16 = jnp.bfloat16

D_MODEL = 1024
N_MEM = 256
NORM_EPS = 1e-6
ROPE_THETA = 500000.0
HEAD_DIM = 64
ROT_DIM = HEAD_DIM // 4
ROT_HALF = ROT_DIM // 2
LANES = 128
SUBLANES = 8
BLOCK = 128
LRU_WIDTH = D_MODEL
LRU_HEADS = 4
LRU_HEAD_DIM = LRU_WIDTH // LRU_HEADS
CONV_WIDTH = 4
LRU_C = 8.0
B_HEADS = 8
B_WIDTH = B_HEADS * HEAD_DIM
DILATED_PATTERN = ((128, 1), (512, 4), (2048, 16))
C_HEADS = 16
C_KV_HEADS = 2
C_WINDOW = 128
C_Q_WIDTH = C_HEADS * HEAD_DIM
C_KV_WIDTH = C_KV_HEADS * HEAD_DIM
XA_HEADS = 4
XA_HEAD_DIM = 128
XA_WIDTH = XA_HEADS * XA_HEAD_DIM
NEG = -1e30
VMEM_LIMIT = 56 * 1024 * 1024

TOK_TILE = 512
MXU_TILE = 256
FF_CHUNK_TILES = 3
LRU_TILE = 1024
LAST_TAIL_TILE = 1024


def _ff_chunks(d_ff):
    assert d_ff % MXU_TILE == 0
    step = FF_CHUNK_TILES * MXU_TILE
    return [(c0, min(step, d_ff - c0)) for c0 in range(0, d_ff, step)]


def _dot(a, b):
    return jnp.dot(a, b, preferred_element_type=F32)


def _dot_nt(a, b):
    return lax.dot_general(a, b, (((1,), (1,)), ((), ())), preferred_element_type=F32)


def _rms(x, g):
    return x * lax.rsqrt(jnp.mean(x * x, axis=-1, keepdims=True) + NORM_EPS) * g


def _params(*sem):
    return pltpu.CompilerParams(dimension_semantics=sem, vmem_limit_bytes=VMEM_LIMIT)


def _const_spec(shape):
    nd = len(shape)
    return pl.BlockSpec(shape, lambda *_: (0,) * nd, pipeline_mode=pl.Buffered(1))


def _layer_spec(shape, layer):
    nd = len(shape)
    return pl.BlockSpec((None,) + tuple(shape[1:]), lambda *_: (layer,) + (0,) * (nd - 1),
                        pipeline_mode=pl.Buffered(1))


def _rope(x, c, s1, s2):
    cols = []
    for j in range(x.shape[1] // LANES):
        xc = x[:, j * LANES:(j + 1) * LANES]
        cols.append(xc * c + pltpu.roll(xc, ROT_HALF, 1) * s1
                    + pltpu.roll(xc, LANES - ROT_HALF, 1) * s2)
    return cols[0] if len(cols) == 1 else jnp.concatenate(cols, axis=1)


def _rope_tables(seq):
    lane = jnp.arange(LANES) % HEAD_DIM
    inv = ROPE_THETA ** (-(2 * (lane % ROT_HALF)).astype(F32) / ROT_DIM)
    ang = jnp.arange(seq, dtype=F32)[:, None] * inv[None, :]
    cos, sin = jnp.cos(ang), jnp.sin(ang)
    c = jnp.where(lane < ROT_DIM, cos, 1.0)
    s1 = jnp.where(jnp.logical_and(lane >= ROT_HALF, lane < ROT_DIM), sin, 0.0)
    s2 = jnp.where(lane < ROT_HALF, -sin, 0.0)
    return jnp.stack([c, s1, s2])


def _table_specs(tm):
    return [pl.BlockSpec((None, tm, LANES), lambda b, i, j=j: (j, i, 0)) for j in range(3)]


def _sigmoid(x):
    return 0.5 * jnp.tanh(0.5 * x) + 0.5


def _shift_rows(x, k, head):
    rolled = pltpu.roll(x, k, 0)
    sub = lax.broadcasted_iota(jnp.int32, head.shape, 0)
    first = jnp.where(sub < k, pltpu.roll(head, k, 0), rolled[0:SUBLANES])
    return jnp.concatenate([first, rolled[SUBLANES:]], axis=0)


def _lru_proj_kernel(h_ref, g_ref, w_ref, cw_ref, cb_ref, wa_ref, ba_ref, wx_ref, bx_ref,
                     lam_ref, rc_ref, rs1_ref, rs2_ref,
                     rec_ref, q_ref, k_ref, v_ref,
                     hn_ref, tail_ref, as_ref, us_ref, gl_ref, hprev_ref, *, tl):
    @pl.when(pl.program_id(1) == 0)
    def _():
        tail_ref[...] = jnp.zeros_like(tail_ref)
        hprev_ref[...] = jnp.zeros_like(hprev_ref)

    hn_ref[...] = _rms(h_ref[...], g_ref[...]).astype(BF16)
    hn = hn_ref[...]
    c, s1, s2 = rc_ref[...], rs1_ref[...], rs2_ref[...]
    ones = jnp.ones((SUBLANES, LRU_HEAD_DIM), F32)
    zeros = jnp.zeros((SUBLANES, LRU_HEAD_DIM), F32)
    attn_out = (q_ref, k_ref, v_ref)

    group_cols = lambda hd: slice(hd * LRU_HEAD_DIM, (hd + 1) * LRU_HEAD_DIM)
    half = B_WIDTH // 2

    def attn_piece(hd, j):
        o = 2 * LRU_WIDTH + hd * B_WIDTH + j * half
        t = _dot(hn, w_ref[:, o:o + half])
        if hd < 2:
            t = _rope(t, c, s1, s2)
        if hd == 0:
            t = t * (HEAD_DIM ** -0.5)
        attn_out[hd][:, j * half:(j + 1) * half] = t.astype(BF16)

    x_next = _dot(hn, w_ref[:, group_cols(0)])
    for hd in range(LRU_HEADS):
        cols = group_cols(hd)
        x = x_next
        if hd + 1 < LRU_HEADS:
            x_next = _dot(hn, w_ref[:, group_cols(hd + 1)])

        tail = tail_ref[:, cols]
        xc = cb_ref[:, cols] + _shift_rows(x, 3, tail) * cw_ref[0:1, cols]
        xc = xc + _shift_rows(x, 2, tail) * cw_ref[1:2, cols]
        xc = xc + _shift_rows(x, 1, tail) * cw_ref[2:3, cols]
        xc = xc + x * cw_ref[3:4, cols]
        tail_ref[:, cols] = x[tl - SUBLANES:tl, :]

        y = _dot(hn, w_ref[:, LRU_WIDTH + hd * LRU_HEAD_DIM:LRU_WIDTH + (hd + 1) * LRU_HEAD_DIM])

        xcb = xc.astype(BF16)
        r = _sigmoid(_dot(xcb, wa_ref[hd]) + ba_ref[:, cols])
        ig = _sigmoid(_dot(xcb, wx_ref[hd]) + bx_ref[:, cols])
        nlam = -lam_ref[:, cols]
        softplus = jnp.maximum(nlam, 0.0) + jnp.log1p(jnp.exp(-jnp.abs(nlam)))
        neg_log_a = r * (LRU_C * softplus)
        a = jnp.exp2(r * ((-LRU_C * math.log2(math.e)) * softplus))
        u = jnp.sqrt(jnp.tanh(neg_log_a) * (1.0 + a * a)) * (ig * xc)

        if hd < len(attn_out):
            attn_piece(hd, 0)

        for s in (1, 2, 4):
            u = a * _shift_rows(u, s, zeros) + u
            a = a * _shift_rows(a, s, ones)
        as_ref[:, cols] = a
        us_ref[:, cols] = u

        if hd < len(attn_out):
            attn_piece(hd, 1)

        gl_ref[:, cols] = 0.5 * y * (1.0 + jnp.tanh(
            math.sqrt(2.0 / math.pi) * (y + 0.044715 * (y * y * y))))

    def chain(g, hcar):
        rows = pl.ds(pl.multiple_of(g * SUBLANES, SUBLANES), SUBLANES)
        hcar = as_ref[rows, :] * hcar + us_ref[rows, :]
        us_ref[rows, :] = hcar
        return hcar

    h0 = jnp.broadcast_to(hprev_ref[SUBLANES - 1:SUBLANES, :], (SUBLANES, LRU_WIDTH))
    hprev_ref[...] = lax.fori_loop(0, tl // SUBLANES, chain, h0, unroll=8)
    rec_ref[...] = (us_ref[...] * gl_ref[...]).astype(BF16)


def _lru_proj(h, g, w_in, conv_w, conv_b, wa, ba, wx, bx, lam, tables, tl=LRU_TILE):
    bsz, seq, d = h.shape
    tok = lambda w: pl.BlockSpec((None, tl, w), lambda b, i: (b, i, 0))
    out_shape = (jax.ShapeDtypeStruct((bsz, seq, LRU_WIDTH), BF16),) + tuple(
        jax.ShapeDtypeStruct((bsz, seq, B_WIDTH), BF16) for _ in range(3))
    params = (g, w_in, conv_w, conv_b, wa, ba, wx, bx, lam)
    return pl.pallas_call(
        functools.partial(_lru_proj_kernel, tl=tl),
        grid=(bsz, seq // tl),
        in_specs=[tok(d)] + [_layer_spec(a.shape, 0) for a in params] + _table_specs(tl),
        out_specs=(tok(LRU_WIDTH), tok(B_WIDTH), tok(B_WIDTH), tok(B_WIDTH)),
        out_shape=out_shape,
        scratch_shapes=[pltpu.VMEM((tl, d), BF16),
                        pltpu.VMEM((SUBLANES, LRU_WIDTH), F32),
                        pltpu.VMEM((tl, LRU_WIDTH), F32),
                        pltpu.VMEM((tl, LRU_WIDTH), F32),
                        pltpu.VMEM((tl, LRU_WIDTH), F32),
                        pltpu.VMEM((SUBLANES, LRU_WIDTH), F32)],
        compiler_params=_params("arbitrary", "arbitrary"),
        name="lru_proj",
    )(h, *params, tables, tables, tables)


def _sink_column(first):
    return (2 * BLOCK - 1) * (1 - first)


def _fill_band_bias(bias_ref, max_dist, reps, sink_of_rep=None):
    rows = lax.broadcasted_iota(jnp.int32, (BLOCK, 2 * BLOCK), 0)
    cols = lax.broadcasted_iota(jnp.int32, (BLOCK, 2 * BLOCK), 1)
    for first, off in ((0, 0), (1, BLOCK)):
        dist = off + rows - cols
        valid = jnp.logical_and(dist >= 0, dist <= max_dist)
        tile = jnp.where(valid, 0.0, NEG).astype(F32)
        for rep in range(reps):
            rep_tile = tile
            if sink_of_rep is not None:
                assert max_dist < BLOCK
                rep_tile = jnp.where(cols == _sink_column(first), sink_of_rep(rep), tile)
            bias_ref[first, rep * BLOCK:(rep + 1) * BLOCK, :] = rep_tile


def _band_attend(qb, kb, vb, bias, want_lse):
    r = qb.shape[0]
    lane_q = lax.broadcasted_iota(jnp.int32, qb.shape, 1)
    zero = jnp.zeros((), qb.dtype)
    qz = jnp.concatenate([jnp.where(lane_q < HEAD_DIM, qb, zero),
                          jnp.where(lane_q >= HEAD_DIM, qb, zero)], axis=0)
    s = _dot_nt(qz, kb) + bias
    m = jnp.max(s, axis=-1, keepdims=True)
    e = jnp.exp(s - m).astype(vb.dtype)
    ob = _dot(e, jnp.concatenate([vb, jnp.ones_like(vb)], axis=1))
    first_head = lax.broadcasted_iota(jnp.int32, (r, LANES), 1) < HEAD_DIM
    num = jnp.where(first_head, ob[:r, :LANES], ob[r:, :LANES])
    den = jnp.where(first_head, ob[:r, LANES:], ob[r:, LANES:])
    o = num * (1.0 / den)
    if not want_lse:
        return o, None
    return o, jnp.where(first_head, m[:r], m[r:]) + jnp.log(den)


def _band_window(n, nblk):
    first = jnp.minimum(lax.rem(n, nblk), 1)
    q0 = pl.multiple_of(n * BLOCK, BLOCK)
    ks = pl.multiple_of(q0 - first * BLOCK, BLOCK)
    return q0, ks, first


def _cast_specs(ws, layer, steps, step_of):
    in_specs, out_specs, out_shapes = [], [], []
    for w in ws:
        assert w.shape[1] % (steps * 2 * SUBLANES) == 0
        blk = (None, w.shape[1] // steps, w.shape[2])
        in_specs.append(pl.BlockSpec(blk, lambda *ids: (layer, step_of(*ids), 0)))
        out_specs.append(pl.BlockSpec(blk, lambda *ids: (0, step_of(*ids), 0)))
        out_shapes.append(jax.ShapeDtypeStruct((1,) + tuple(w.shape[1:]), BF16))
    return in_specs, out_specs, out_shapes


def _cast_slabs(src_refs, dst_refs):
    for src, dst in zip(src_refs, dst_refs):
        dst[...] = src[...].astype(dst.dtype)


def _dilated_kernel(*refs, seq, max_dist, n_cast):
    q_ref, k_ref, v_ref = refs[:3]
    out_ref = refs[3 + n_cast]
    stage_ref, p4_ref, perm_ref, o_ref, l_ref, bias_ref = refs[4 + 2 * n_cast:]
    _cast_slabs(refs[3:3 + n_cast], refs[4 + n_cast:4 + 2 * n_cast])
    _fill_band_bias(bias_ref, max_dist, 2)

    l4, l16 = seq // 4, seq // 16
    for a, x_ref in enumerate((q_ref, k_ref, v_ref)):
        stage_ref[...] = x_ref[...].astype(F32)
        for r in range(4):
            blk = stage_ref[pl.ds(r, l4, stride=4), :]
            p4_ref[r * l4:(r + 1) * l4, :] = blk
            perm_ref[0, a, r * l4:(r + 1) * l4, :] = blk.astype(BF16)
        for r in range(4):
            for rb in range(4):
                blk = p4_ref[pl.ds(r * l4 + rb, l16, stride=4), :]
                res = r + 4 * rb
                perm_ref[1, a, res * l16:(res + 1) * l16, :] = blk.astype(BF16)

    branches = ((1, (q_ref, k_ref, v_ref)),
                (4, tuple(perm_ref.at[0, a] for a in range(3))),
                (16, tuple(perm_ref.at[1, a] for a in range(3))))
    for bi, (d, (qr, kr, vr)) in enumerate(branches):
        nblk = seq // d // BLOCK

        def body(n, carry, d=d, bi=bi, qr=qr, kr=kr, vr=vr, nblk=nblk):
            q0, ks, first = _band_window(n, nblk)
            o, lse = _band_attend(qr[pl.ds(q0, BLOCK), :], kr[pl.ds(ks, 2 * BLOCK), :],
                                  vr[pl.ds(ks, 2 * BLOCK), :], bias_ref[first], True)
            if d == 1:
                rows = pl.ds(q0, BLOCK)
            else:
                rows = pl.ds(lax.rem(n, nblk) * (BLOCK * d) + n // nblk, BLOCK, stride=d)
            o_ref.at[bi][rows, :] = o
            l_ref.at[bi][rows, :] = lse
            return carry

        lax.fori_loop(0, seq // BLOCK, body, 0, unroll=32)

    chunk = 4 * BLOCK

    def merge(c, carry):
        rows = pl.ds(pl.multiple_of(c * chunk, chunk), chunk)
        ls = [l_ref[bi, rows, :] for bi in range(3)]
        m = jnp.maximum(jnp.maximum(ls[0], ls[1]), ls[2])
        ws = [jnp.exp(l - m) for l in ls]
        acc = ws[0] * o_ref[0, rows, :] + ws[1] * o_ref[1, rows, :] + ws[2] * o_ref[2, rows, :]
        out_ref[rows, :] = (acc * (1.0 / (ws[0] + ws[1] + ws[2]))).astype(out_ref.dtype)
        return carry

    lax.fori_loop(0, seq // chunk, merge, 0)


def _dilated_attention(q, k, v, cast, cast_layer):
    assert tuple(d for _, d in DILATED_PATTERN) == (1, 4, 16)
    assert len({w // d for w, d in DILATED_PATTERN}) == 1
    max_dist = DILATED_PATTERN[0][0] // DILATED_PATTERN[0][1]
    bsz, seq, w = q.shape
    cols = w // LANES
    spec = pl.BlockSpec((None, seq, LANES), lambda b, c: (b, 0, c))
    cast_in, cast_out, cast_shapes = _cast_specs(cast, cast_layer, bsz * cols,
                                                 lambda b, c: b * cols + c)
    return pl.pallas_call(
        functools.partial(_dilated_kernel, seq=seq, max_dist=max_dist, n_cast=len(cast)),
        grid=(bsz, cols),
        in_specs=[spec, spec, spec] + cast_in,
        out_specs=[spec] + cast_out,
        out_shape=[jax.ShapeDtypeStruct((bsz, seq, w), BF16)] + cast_shapes,
        scratch_shapes=[pltpu.VMEM((seq, LANES), F32), pltpu.VMEM((seq, LANES), F32),
                        pltpu.VMEM((2, 3, seq, LANES), BF16),
                        pltpu.VMEM((3, seq, LANES), F32), pltpu.VMEM((3, seq, LANES), F32),
                        pltpu.VMEM((2, 2 * BLOCK, 2 * BLOCK), F32)],
        compiler_params=_params("arbitrary", "arbitrary"),
        name="dilated",
    )(q, k, v, *cast)


def _swa_kernel(*refs, nblk, max_dist, pairs, n_cast):
    sink_ref, q_ref, k_ref, v_ref = refs[:4]
    o_ref = refs[4 + n_cast]
    bias_ref = refs[5 + 2 * n_cast]
    _cast_slabs(refs[4:4 + n_cast], refs[5 + n_cast:5 + 2 * n_cast])
    head0 = pl.program_id(1) * (2 * pairs)
    _fill_band_bias(bias_ref, max_dist, 2 * pairs,
                    lambda rep: sink_ref[head0 + 2 * (rep % pairs) + rep // pairs])
    key_row = lax.broadcasted_iota(jnp.int32, (2 * BLOCK, LANES), 0)

    def body(i, carry):
        q0, ks, first = _band_window(i, nblk)
        qrow = q_ref[pl.ds(q0, BLOCK), :]
        qb = jnp.concatenate([qrow[:, p * LANES:(p + 1) * LANES] for p in range(pairs)], axis=0)
        unseen = key_row == _sink_column(first)
        zero = jnp.zeros((), k_ref.dtype)
        kb = jnp.where(unseen, zero, k_ref[pl.ds(ks, 2 * BLOCK), :])
        vb = jnp.where(unseen, zero, v_ref[pl.ds(ks, 2 * BLOCK), :])
        o, _ = _band_attend(qb, kb, vb, bias_ref[first], False)
        o_ref[pl.ds(q0, BLOCK), :] = jnp.concatenate(
            [o[p * BLOCK:(p + 1) * BLOCK, :] for p in range(pairs)], axis=1).astype(o_ref.dtype)
        return carry

    lax.fori_loop(0, nblk, body, 0, unroll=16)


def _swa_attention(q, kdup, vdup, sinks, cast, cast_layer):
    bsz, seq, w = q.shape
    pairs = w // LANES // C_KV_HEADS
    qspec = pl.BlockSpec((None, seq, pairs * LANES), lambda b, g: (b, 0, g))
    kvspec = pl.BlockSpec((None, seq, LANES), lambda b, g: (b, 0, g))
    cast_in, cast_out, cast_shapes = _cast_specs(cast, cast_layer, bsz * C_KV_HEADS,
                                                 lambda b, g: b * C_KV_HEADS + g)
    return pl.pallas_call(
        functools.partial(_swa_kernel, nblk=seq // BLOCK, max_dist=C_WINDOW - 1, pairs=pairs,
                          n_cast=len(cast)),
        grid=(bsz, C_KV_HEADS),
        in_specs=[pl.BlockSpec(memory_space=pltpu.SMEM), qspec, kvspec, kvspec] + cast_in,
        out_specs=[qspec] + cast_out,
        out_shape=[jax.ShapeDtypeStruct((bsz, seq, w), BF16)] + cast_shapes,
        scratch_shapes=[pltpu.VMEM((2, 2 * pairs * BLOCK, 2 * BLOCK), F32)],
        compiler_params=_params("arbitrary", "arbitrary"),
        name="swa_sink",
    )(sinks, q, kdup, vdup, *cast)


def _dup_heads(a):
    swapped = pltpu.roll(a, HEAD_DIM, 1)
    low = lax.broadcasted_iota(jnp.int32, a.shape, 1) < HEAD_DIM
    return jnp.concatenate([jnp.where(low, a, swapped), jnp.where(low, swapped, a)], axis=1)


def _swa_qkv(h, g_ref, w_ref, b_ref, rc_ref, rs1_ref, rs2_ref, q_ref, k_ref, v_ref):
    hn = _rms(h, g_ref[...]).astype(BF16)
    c, s1, s2 = rc_ref[...], rs1_ref[...], rs2_ref[...]
    ko, vo = C_Q_WIDTH, C_Q_WIDTH + C_KV_WIDTH
    q = _dot(hn, w_ref[:, 0:ko]) + b_ref[:, 0:ko]
    q_ref[...] = (_rope(q, c, s1, s2) * (HEAD_DIM ** -0.5)).astype(BF16)
    kv = _dot(hn, w_ref[:, ko:vo + C_KV_WIDTH]) + b_ref[:, ko:vo + C_KV_WIDTH]
    k_ref[...] = _dup_heads(_rope(kv[:, :C_KV_WIDTH], c, s1, s2)).astype(BF16)
    v_ref[...] = _dup_heads(kv[:, C_KV_WIDTH:]).astype(BF16)


def _mem_kv_kernel(mem_ref, g_ref, w_ref, k_ref, v_ref):
    mn = _rms(mem_ref[...], g_ref[...]).astype(BF16)
    k_ref[...] = _dot(mn, w_ref[:, 0:XA_WIDTH]).astype(BF16)
    v_ref[...] = _dot(mn, w_ref[:, XA_WIDTH:2 * XA_WIDTH]).astype(BF16)


def _mem_kv(mem, g, wkv):
    bsz, m, d = mem.shape
    layers = wkv.shape[0]
    spec = pl.BlockSpec((None, bsz * m, XA_WIDTH), lambda l: (l, 0, 0))
    per_layer = lambda a: pl.BlockSpec((None,) + a.shape[1:], lambda l: (l, 0, 0))
    return pl.pallas_call(
        _mem_kv_kernel,
        grid=(layers,),
        in_specs=[pl.BlockSpec((bsz * m, d), lambda l: (0, 0)), per_layer(g), per_layer(wkv)],
        out_specs=(spec, spec),
        out_shape=(jax.ShapeDtypeStruct((layers, bsz * m, XA_WIDTH), BF16),) * 2,
        compiler_params=_params("arbitrary"),
        name="mem_kv",
    )(mem.reshape(bsz * m, d), g, wkv)


def _layer_tail_kernel(*refs, n_act, d_ff, last):
    h_ref, act_refs = refs[0], refs[1:1 + n_act]
    (wout_ref, bout_ref, xg_ref, wq_ref, mk_ref, mv_ref, wo_ref,
     fg_ref, wgu_ref, wd_ref) = refs[1 + n_act:11 + n_act]
    epilogue = refs[11 + n_act:]

    h = h_ref[...] + bout_ref[...]
    row = 0
    for a_ref in act_refs:
        width = a_ref.shape[-1]
        h = h + _dot(a_ref[...], wout_ref[row:row + width, :])
        row += width

    q = _dot(_rms(h, xg_ref[...]).astype(BF16), wq_ref[...]).astype(BF16)
    scale = XA_HEAD_DIM ** -0.5
    heads = []
    for hd in range(XA_HEADS):
        sl = slice(hd * XA_HEAD_DIM, (hd + 1) * XA_HEAD_DIM)
        s = _dot_nt(q[:, sl], mk_ref[:, sl]) * scale
        e = jnp.exp(s - jnp.max(s, axis=-1, keepdims=True))
        den = jnp.sum(e, axis=-1, keepdims=True)
        heads.append((_dot(e.astype(BF16), mv_ref[:, sl]) * (1.0 / den)).astype(BF16))
    h = h + _dot(jnp.concatenate(heads, axis=1), wo_ref[...])

    hn = _rms(h, fg_ref[...]).astype(BF16)
    for c0, width in _ff_chunks(d_ff):
        gate = _dot(hn, wgu_ref[:, c0:c0 + width])
        up = _dot(hn, wgu_ref[:, d_ff + c0:d_ff + c0 + width])
        act = (gate * jax.nn.sigmoid(gate) * up).astype(BF16)
        h = h + _dot(act, wd_ref[c0:c0 + width, :])
    if last:
        ng_ref, out_ref = epilogue
        out_ref[...] = _rms(h, ng_ref[...])
    else:
        out_ref = epilogue[6]
        out_ref[...] = h
        _swa_qkv(h, *epilogue[:6], *epilogue[7:])


def _layer_tail(h, acts, w_out, b_out, mk, mv, p, ffn, layer, nxt=None, tm=TOK_TILE):
    bsz, seq, d = h.shape
    wgu, wd = ffn
    d_ff = wd.shape[1]
    assert sum(a.shape[-1] for a in acts) == w_out.shape[1]
    tok = lambda width: pl.BlockSpec((None, tm, width), lambda b, i: (b, i, 0))
    kv = pl.BlockSpec((None, N_MEM, XA_WIDTH), lambda b, i: (layer, b, 0))
    res = jax.ShapeDtypeStruct((bsz, seq, d), F32)
    if nxt is None:
        extra, extra_specs = (p["ng"],), [_const_spec(p["ng"].shape)]
        out_specs, out_shape = tok(d), res
    else:
        g2, w2, b2, tables = nxt
        extra = (g2, w2, b2, tables, tables, tables)
        extra_specs = [_layer_spec(a.shape, 0) for a in (g2, w2, b2)] + _table_specs(tm)
        widths = (C_Q_WIDTH, 2 * C_KV_WIDTH, 2 * C_KV_WIDTH)
        out_specs = (tok(d),) + tuple(tok(w) for w in widths)
        out_shape = (res,) + tuple(jax.ShapeDtypeStruct((bsz, seq, w), BF16) for w in widths)
    return pl.pallas_call(
        functools.partial(_layer_tail_kernel, n_act=len(acts), d_ff=d_ff, last=nxt is None),
        grid=(bsz, seq // tm),
        in_specs=[tok(d)] + [tok(a.shape[-1]) for a in acts] + [
            _layer_spec(w_out.shape, 0), _layer_spec(b_out.shape, 0),
            _layer_spec(p["xg"].shape, layer), _layer_spec(p["wq"].shape, layer), kv, kv,
            _layer_spec(p["wo"].shape, layer), _layer_spec(p["fg"].shape, layer),
            _layer_spec(wgu.shape, 0), _layer_spec(wd.shape, 0)] + extra_specs,
        out_specs=out_specs,
        out_shape=out_shape,
        compiler_params=_params("arbitrary", "arbitrary"),
        name="layer_tail",
    )(h, *acts, w_out, b_out, p["xg"], p["wq"], mk, mv, p["wo"], p["fg"], wgu, wd,
      *extra)


def kernel(x, mem, mix_norm, ab_w_in, lru_conv_w, lru_conv_b, lru_wa, lru_ba, lru_wx, lru_bx, lru_lambda, ab_w_out, c_w_qkv, c_b_qkv, c_sinks, c_w_out, c_b_out, xa_norm, xa_mem_norm, xa_wq, xa_wkv, xa_wo, ffn_norm, ffn_w_gate_up, ffn_w_down, final_norm):
    bsz, seq, d = x.shape
    tables = _rope_tables(seq)
    rows = lambda a: a.reshape(a.shape[0], 1, -1)
    bf = lambda a: a.astype(BF16)
    mix_g = rows(mix_norm)
    p = dict(xg=rows(xa_norm), wq=bf(xa_wq), wo=bf(xa_wo), fg=rows(ffn_norm),
             ng=final_norm.reshape(1, d))
    ffn_f32 = (ffn_w_gate_up, ffn_w_down)
    mk, mv = _mem_kv(mem, rows(xa_mem_norm), bf(xa_wkv))

    rec, q, k, v = _lru_proj(x, mix_g, bf(ab_w_in), lru_conv_w, rows(lru_conv_b), bf(lru_wa),
                             rows(lru_ba), bf(lru_wx), rows(lru_bx), rows(lru_lambda), tables)
    att, *ffn = _dilated_attention(q, k, v, ffn_f32, 0)
    h, q, k, v = _layer_tail(x, (rec, att), bf(ab_w_out), jnp.zeros((1, 1, d), F32), mk, mv, p,
                             ffn, layer=0,
                             nxt=(mix_g[1:2], bf(c_w_qkv), rows(c_b_qkv), tables))

    att, *ffn = _swa_attention(q, k, v, c_sinks[0], ffn_f32, 1)
    return _layer_tail(h, (att,), bf(c_w_out), rows(c_b_out), mk, mv, p, ffn, layer=1,
                       tm=LAST_TAIL_TILE)
```

```python
import functools
import math

import jax
import jax.numpy as jnp
from jax import lax
from jax.experimental import pallas as pl
from jax.experimental.pallas import tpu as pltpu

F32 = jnp.float32
BF16 = jnp.bfloat16

D_MODEL = 1024
N_MEM = 256
NORM_EPS = 1e-6
ROPE_THETA = 500000.0
HEAD_DIM = 64
ROT_DIM = HEAD_DIM // 4
ROT_HALF = ROT_DIM // 2
LANES = 128
SUBLANES = 8
BLOCK = 128
LRU_WIDTH = D_MODEL
LRU_HEADS = 4
LRU_HEAD_DIM = LRU_WIDTH // LRU_HEADS
CONV_WIDTH = 4
LRU_C = 8.0
B_HEADS = 8
B_WIDTH = B_HEADS * HEAD_DIM
DILATED_PATTERN = ((128, 1), (512, 4), (2048, 16))
C_HEADS = 16
C_KV_HEADS = 2
C_WINDOW = 128
C_Q_WIDTH = C_HEADS * HEAD_DIM
C_KV_WIDTH = C_KV_HEADS * HEAD_DIM
XA_HEADS = 4
XA_HEAD_DIM = 128
XA_WIDTH = XA_HEADS * XA_HEAD_DIM
NEG = -1e30
VMEM_LIMIT = 56 * 1024 * 1024

TOK_TILE = 512
MXU_TILE = 256
FF_CHUNK_TILES = 3
LRU_TILE = 1024
LAST_TAIL_TILE = 1024


def _ff_chunks(d_ff):
    assert d_ff % MXU_TILE == 0
    step = FF_CHUNK_TILES * MXU_TILE
    return [(c0, min(step, d_ff - c0)) for c0 in range(0, d_ff, step)]


def _dot(a, b):
    return jnp.dot(a, b, preferred_element_type=F32)


def _dot_nt(a, b):
    return lax.dot_general(a, b, (((1,), (1,)), ((), ())), preferred_element_type=F32)


def _rms(x, g):
    return x * lax.rsqrt(jnp.mean(x * x, axis=-1, keepdims=True) + NORM_EPS) * g


def _params(*sem):
    return pltpu.CompilerParams(dimension_semantics=sem, vmem_limit_bytes=VMEM_LIMIT)


def _const_spec(shape):
    nd = len(shape)
    return pl.BlockSpec(shape, lambda *_: (0,) * nd, pipeline_mode=pl.Buffered(1))


def _layer_spec(shape, layer):
    nd = len(shape)
    return pl.BlockSpec((None,) + tuple(shape[1:]), lambda *_: (layer,) + (0,) * (nd - 1),
                        pipeline_mode=pl.Buffered(1))


def _rope(x, c, s1, s2):
    cols = []
    for j in range(x.shape[1] // LANES):
        xc = x[:, j * LANES:(j + 1) * LANES]
        cols.append(xc * c + pltpu.roll(xc, ROT_HALF, 1) * s1
                    + pltpu.roll(xc, LANES - ROT_HALF, 1) * s2)
    return cols[0] if len(cols) == 1 else jnp.concatenate(cols, axis=1)


def _rope_tables(seq):
    lane = jnp.arange(LANES) % HEAD_DIM
    inv = ROPE_THETA ** (-(2 * (lane % ROT_HALF)).astype(F32) / ROT_DIM)
    ang = jnp.arange(seq, dtype=F32)[:, None] * inv[None, :]
    cos, sin = jnp.cos(ang), jnp.sin(ang)
    c = jnp.where(lane < ROT_DIM, cos, 1.0)
    s1 = jnp.where(jnp.logical_and(lane >= ROT_HALF, lane < ROT_DIM), sin, 0.0)
    s2 = jnp.where(lane < ROT_HALF, -sin, 0.0)
    return jnp.stack([c, s1, s2])


def _table_specs(tm):
    return [pl.BlockSpec((None, tm, LANES), lambda b, i, j=j: (j, i, 0)) for j in range(3)]


def _sigmoid(x):
    return 0.5 * jnp.tanh(0.5 * x) + 0.5


def _shift_rows(x, k, head):
    rolled = pltpu.roll(x, k, 0)
    sub = lax.broadcasted_iota(jnp.int32, head.shape, 0)
    first = jnp.where(sub < k, pltpu.roll(head, k, 0), rolled[0:SUBLANES])
    return jnp.concatenate([first, rolled[SUBLANES:]], axis=0)


def _lru_proj_kernel(h_ref, g_ref, w_ref, cw_ref, cb_ref, wa_ref, ba_ref, wx_ref, bx_ref,
                     lam_ref, rc_ref, rs1_ref, rs2_ref,
                     rec_ref, q_ref, k_ref, v_ref,
                     hn_ref, tail_ref, as_ref, us_ref, gl_ref, hprev_ref, *, tl):
    @pl.when(pl.program_id(1) == 0)
    def _():
        tail_ref[...] = jnp.zeros_like(tail_ref)
        hprev_ref[...] = jnp.zeros_like(hprev_ref)

    hn_ref[...] = _rms(h_ref[...], g_ref[...]).astype(BF16)
    hn = hn_ref[...]
    c, s1, s2 = rc_ref[...], rs1_ref[...], rs2_ref[...]
    ones = jnp.ones((SUBLANES, LRU_HEAD_DIM), F32)
    zeros = jnp.zeros((SUBLANES, LRU_HEAD_DIM), F32)
    attn_out = (q_ref, k_ref, v_ref)

    group_cols = lambda hd: slice(hd * LRU_HEAD_DIM, (hd + 1) * LRU_HEAD_DIM)
    half = B_WIDTH // 2

    def attn_piece(hd, j):
        o = 2 * LRU_WIDTH + hd * B_WIDTH + j * half
        t = _dot(hn, w_ref[:, o:o + half])
        if hd < 2:
            t = _rope(t, c, s1, s2)
        if hd == 0:
            t = t * (HEAD_DIM ** -0.5)
        attn_out[hd][:, j * half:(j + 1) * half] = t.astype(BF16)

    x_next = _dot(hn, w_ref[:, group_cols(0)])
    for hd in range(LRU_HEADS):
        cols = group_cols(hd)
        x = x_next
        if hd + 1 < LRU_HEADS:
            x_next = _dot(hn, w_ref[:, group_cols(hd + 1)])

        tail = tail_ref[:, cols]
        xc = cb_ref[:, cols] + _shift_rows(x, 3, tail) * cw_ref[0:1, cols]
        xc = xc + _shift_rows(x, 2, tail) * cw_ref[1:2, cols]
        xc = xc + _shift_rows(x, 1, tail) * cw_ref[2:3, cols]
        xc = xc + x * cw_ref[3:4, cols]
        tail_ref[:, cols] = x[tl - SUBLANES:tl, :]

        y = _dot(hn, w_ref[:, LRU_WIDTH + hd * LRU_HEAD_DIM:LRU_WIDTH + (hd + 1) * LRU_HEAD_DIM])

        xcb = xc.astype(BF16)
        r = _sigmoid(_dot(xcb, wa_ref[hd]) + ba_ref[:, cols])
        ig = _sigmoid(_dot(xcb, wx_ref[hd]) + bx_ref[:, cols])
        nlam = -lam_ref[:, cols]
        softplus = jnp.maximum(nlam, 0.0) + jnp.log1p(jnp.exp(-jnp.abs(nlam)))
        neg_log_a = r * (LRU_C * softplus)
        a = jnp.exp2(r * ((-LRU_C * math.log2(math.e)) * softplus))
        u = jnp.sqrt(jnp.tanh(neg_log_a) * (1.0 + a * a)) * (ig * xc)

        if hd < len(attn_out):
            attn_piece(hd, 0)

        for s in (1, 2, 4):
            u = a * _shift_rows(u, s, zeros) + u
            a = a * _shift_rows(a, s, ones)
        as_ref[:, cols] = a
        us_ref[:, cols] = u

        if hd < len(attn_out):
            attn_piece(hd, 1)

        gl_ref[:, cols] = 0.5 * y * (1.0 + jnp.tanh(
            math.sqrt(2.0 / math.pi) * (y + 0.044715 * (y * y * y))))

    def chain(g, hcar):
        rows = pl.ds(pl.multiple_of(g * SUBLANES, SUBLANES), SUBLANES)
        hcar = as_ref[rows, :] * hcar + us_ref[rows, :]
        us_ref[rows, :] = hcar
        return hcar

    h0 = jnp.broadcast_to(hprev_ref[SUBLANES - 1:SUBLANES, :], (SUBLANES, LRU_WIDTH))
    hprev_ref[...] = lax.fori_loop(0, tl // SUBLANES, chain, h0, unroll=8)
    rec_ref[...] = (us_ref[...] * gl_ref[...]).astype(BF16)


def _lru_proj(h, g, w_in, conv_w, conv_b, wa, ba, wx, bx, lam, tables, tl=LRU_TILE):
    bsz, seq, d = h.shape
    tok = lambda w: pl.BlockSpec((None, tl, w), lambda b, i: (b, i, 0))
    out_shape = (jax.ShapeDtypeStruct((bsz, seq, LRU_WIDTH), BF16),) + tuple(
        jax.ShapeDtypeStruct((bsz, seq, B_WIDTH), BF16) for _ in range(3))
    params = (g, w_in, conv_w, conv_b, wa, ba, wx, bx, lam)
    return pl.pallas_call(
        functools.partial(_lru_proj_kernel, tl=tl),
        grid=(bsz, seq // tl),
        in_specs=[tok(d)] + [_layer_spec(a.shape, 0) for a in params] + _table_specs(tl),
        out_specs=(tok(LRU_WIDTH), tok(B_WIDTH), tok(B_WIDTH), tok(B_WIDTH)),
        out_shape=out_shape,
        scratch_shapes=[pltpu.VMEM((tl, d), BF16),
                        pltpu.VMEM((SUBLANES, LRU_WIDTH), F32),
                        pltpu.VMEM((tl, LRU_WIDTH), F32),
                        pltpu.VMEM((tl, LRU_WIDTH), F32),
                        pltpu.VMEM((tl, LRU_WIDTH), F32),
                        pltpu.VMEM((SUBLANES, LRU_WIDTH), F32)],
        compiler_params=_params("arbitrary", "arbitrary"),
        name="lru_proj",
    )(h, *params, tables, tables, tables)


def _sink_column(first):
    return (2 * BLOCK - 1) * (1 - first)


def _fill_band_bias(bias_ref, max_dist, reps, sink_of_rep=None):
    rows = lax.broadcasted_iota(jnp.int32, (BLOCK, 2 * BLOCK), 0)
    cols = lax.broadcasted_iota(jnp.int32, (BLOCK, 2 * BLOCK), 1)
    for first, off in ((0, 0), (1, BLOCK)):
        dist = off + rows - cols
        valid = jnp.logical_and(dist >= 0, dist <= max_dist)
        tile = jnp.where(valid, 0.0, NEG).astype(F32)
        for rep in range(reps):
            rep_tile = tile
            if sink_of_rep is not None:
                assert max_dist < BLOCK
                rep_tile = jnp.where(cols == _sink_column(first), sink_of_rep(rep), tile)
            bias_ref[first, rep * BLOCK:(rep + 1) * BLOCK, :] = rep_tile


def _band_attend(qb, kb, vb, bias, want_lse):
    r = qb.shape[0]
    lane_q = lax.broadcasted_iota(jnp.int32, qb.shape, 1)
    zero = jnp.zeros((), qb.dtype)
    qz = jnp.concatenate([jnp.where(lane_q < HEAD_DIM, qb, zero),
                          jnp.where(lane_q >= HEAD_DIM, qb, zero)], axis=0)
    s = _dot_nt(qz, kb) + bias
    m = jnp.max(s, axis=-1, keepdims=True)
    e = jnp.exp(s - m).astype(vb.dtype)
    ob = _dot(e, jnp.concatenate([vb, jnp.ones_like(vb)], axis=1))
    first_head = lax.broadcasted_iota(jnp.int32, (r, LANES), 1) < HEAD_DIM
    num = jnp.where(first_head, ob[:r, :LANES], ob[r:, :LANES])
    den = jnp.where(first_head, ob[:r, LANES:], ob[r:, LANES:])
    o = num * (1.0 / den)
    if not want_lse:
        return o, None
    return o, jnp.where(first_head, m[:r], m[r:]) + jnp.log(den)


def _band_window(n, nblk):
    first = jnp.minimum(lax.rem(n, nblk), 1)
    q0 = pl.multiple_of(n * BLOCK, BLOCK)
    ks = pl.multiple_of(q0 - first * BLOCK, BLOCK)
    return q0, ks, first


def _cast_specs(ws, layer, steps, step_of):
    in_specs, out_specs, out_shapes = [], [], []
    for w in ws:
        assert w.shape[1] % (steps * 2 * SUBLANES) == 0
        blk = (None, w.shape[1] // steps, w.shape[2])
        src = layer if w.shape[0] > 1 else 0
        in_specs.append(pl.BlockSpec(blk, lambda *ids, src=src: (src, step_of(*ids), 0)))
        out_specs.append(pl.BlockSpec(blk, lambda *ids: (0, step_of(*ids), 0)))
        out_shapes.append(jax.ShapeDtypeStruct((1,) + tuple(w.shape[1:]), BF16))
    return in_specs, out_specs, out_shapes


def _cast_slabs(src_refs, dst_refs):
    for src, dst in zip(src_refs, dst_refs):
        dst[...] = src[...].astype(dst.dtype)


def _dilated_kernel(*refs, seq, max_dist, n_cast):
    q_ref, k_ref, v_ref = refs[:3]
    out_ref = refs[3 + n_cast]
    stage_ref, p4_ref, perm_ref, o_ref, l_ref, bias_ref = refs[4 + 2 * n_cast:]
    _cast_slabs(refs[3:3 + n_cast], refs[4 + n_cast:4 + 2 * n_cast])
    _fill_band_bias(bias_ref, max_dist, 2)

    l4, l16 = seq // 4, seq // 16
    for a, x_ref in enumerate((q_ref, k_ref, v_ref)):
        stage_ref[...] = x_ref[...].astype(F32)
        for r in range(4):
            blk = stage_ref[pl.ds(r, l4, stride=4), :]
            p4_ref[r * l4:(r + 1) * l4, :] = blk
            perm_ref[0, a, r * l4:(r + 1) * l4, :] = blk.astype(BF16)
        for r in range(4):
            for rb in range(4):
                blk = p4_ref[pl.ds(r * l4 + rb, l16, stride=4), :]
                res = r + 4 * rb
                perm_ref[1, a, res * l16:(res + 1) * l16, :] = blk.astype(BF16)

    branches = ((1, (q_ref, k_ref, v_ref)),
                (4, tuple(perm_ref.at[0, a] for a in range(3))),
                (16, tuple(perm_ref.at[1, a] for a in range(3))))
    for bi, (d, (qr, kr, vr)) in enumerate(branches):
        nblk = seq // d // BLOCK

        def body(n, carry, d=d, bi=bi, qr=qr, kr=kr, vr=vr, nblk=nblk):
            q0, ks, first = _band_window(n, nblk)
            o, lse = _band_attend(qr[pl.ds(q0, BLOCK), :], kr[pl.ds(ks, 2 * BLOCK), :],
                                  vr[pl.ds(ks, 2 * BLOCK), :], bias_ref[first], True)
            if d == 1:
                rows = pl.ds(q0, BLOCK)
            else:
                rows = pl.ds(lax.rem(n, nblk) * (BLOCK * d) + n // nblk, BLOCK, stride=d)
            o_ref.at[bi][rows, :] = o
            l_ref.at[bi][rows, :] = lse
            return carry

        lax.fori_loop(0, seq // BLOCK, body, 0, unroll=32)

    chunk = 4 * BLOCK

    def merge(c, carry):
        rows = pl.ds(pl.multiple_of(c * chunk, chunk), chunk)
        ls = [l_ref[bi, rows, :] for bi in range(3)]
        m = jnp.maximum(jnp.maximum(ls[0], ls[1]), ls[2])
        ws = [jnp.exp(l - m) for l in ls]
        acc = ws[0] * o_ref[0, rows, :] + ws[1] * o_ref[1, rows, :] + ws[2] * o_ref[2, rows, :]
        out_ref[rows, :] = (acc * (1.0 / (ws[0] + ws[1] + ws[2]))).astype(out_ref.dtype)
        return carry

    lax.fori_loop(0, seq // chunk, merge, 0)


def _dilated_attention(q, k, v, cast, cast_layer):
    assert tuple(d for _, d in DILATED_PATTERN) == (1, 4, 16)
    assert len({w // d for w, d in DILATED_PATTERN}) == 1
    max_dist = DILATED_PATTERN[0][0] // DILATED_PATTERN[0][1]
    bsz, seq, w = q.shape
    cols = w // LANES
    spec = pl.BlockSpec((None, seq, LANES), lambda b, c: (b, 0, c))
    cast_in, cast_out, cast_shapes = _cast_specs(cast, cast_layer, bsz * cols,
                                                 lambda b, c: b * cols + c)
    return pl.pallas_call(
        functools.partial(_dilated_kernel, seq=seq, max_dist=max_dist, n_cast=len(cast)),
        grid=(bsz, cols),
        in_specs=[spec, spec, spec] + cast_in,
        out_specs=[spec] + cast_out,
        out_shape=[jax.ShapeDtypeStruct((bsz, seq, w), BF16)] + cast_shapes,
        scratch_shapes=[pltpu.VMEM((seq, LANES), F32), pltpu.VMEM((seq, LANES), F32),
                        pltpu.VMEM((2, 3, seq, LANES), BF16),
                        pltpu.VMEM((3, seq, LANES), F32), pltpu.VMEM((3, seq, LANES), F32),
                        pltpu.VMEM((2, 2 * BLOCK, 2 * BLOCK), F32)],
        compiler_params=_params("arbitrary", "arbitrary"),
        name="dilated",
    )(q, k, v, *cast)


def _swa_kernel(*refs, nblk, max_dist, pairs, n_cast):
    sink_ref, q_ref, k_ref, v_ref = refs[:4]
    o_ref = refs[4 + n_cast]
    bias_ref = refs[5 + 2 * n_cast]
    _cast_slabs(refs[4:4 + n_cast], refs[5 + n_cast:5 + 2 * n_cast])
    head0 = pl.program_id(1) * (2 * pairs)
    _fill_band_bias(bias_ref, max_dist, 2 * pairs,
                    lambda rep: sink_ref[head0 + 2 * (rep % pairs) + rep // pairs])
    key_row = lax.broadcasted_iota(jnp.int32, (2 * BLOCK, LANES), 0)

    def body(i, carry):
        q0, ks, first = _band_window(i, nblk)
        qrow = q_ref[pl.ds(q0, BLOCK), :]
        qb = jnp.concatenate([qrow[:, p * LANES:(p + 1) * LANES] for p in range(pairs)], axis=0)
        unseen = key_row == _sink_column(first)
        zero = jnp.zeros((), k_ref.dtype)
        kb = jnp.where(unseen, zero, k_ref[pl.ds(ks, 2 * BLOCK), :])
        vb = jnp.where(unseen, zero, v_ref[pl.ds(ks, 2 * BLOCK), :])
        o, _ = _band_attend(qb, kb, vb, bias_ref[first], False)
        o_ref[pl.ds(q0, BLOCK), :] = jnp.concatenate(
            [o[p * BLOCK:(p + 1) * BLOCK, :] for p in range(pairs)], axis=1).astype(o_ref.dtype)
        return carry

    lax.fori_loop(0, nblk, body, 0, unroll=16)


def _swa_attention(q, kdup, vdup, sinks, cast, cast_layer):
    bsz, seq, w = q.shape
    pairs = w // LANES // C_KV_HEADS
    qspec = pl.BlockSpec((None, seq, pairs * LANES), lambda b, g: (b, 0, g))
    kvspec = pl.BlockSpec((None, seq, LANES), lambda b, g: (b, 0, g))
    cast_in, cast_out, cast_shapes = _cast_specs(cast, cast_layer, bsz * C_KV_HEADS,
                                                 lambda b, g: b * C_KV_HEADS + g)
    return pl.pallas_call(
        functools.partial(_swa_kernel, nblk=seq // BLOCK, max_dist=C_WINDOW - 1, pairs=pairs,
                          n_cast=len(cast)),
        grid=(bsz, C_KV_HEADS),
        in_specs=[pl.BlockSpec(memory_space=pltpu.SMEM), qspec, kvspec, kvspec] + cast_in,
        out_specs=[qspec] + cast_out,
        out_shape=[jax.ShapeDtypeStruct((bsz, seq, w), BF16)] + cast_shapes,
        scratch_shapes=[pltpu.VMEM((2, 2 * pairs * BLOCK, 2 * BLOCK), F32)],
        compiler_params=_params("arbitrary", "arbitrary"),
        name="swa_sink",
    )(sinks, q, kdup, vdup, *cast)


def _dup_heads(a):
    swapped = pltpu.roll(a, HEAD_DIM, 1)
    low = lax.broadcasted_iota(jnp.int32, a.shape, 1) < HEAD_DIM
    return jnp.concatenate([jnp.where(low, a, swapped), jnp.where(low, swapped, a)], axis=1)


def _swa_qkv(h, g_ref, w_ref, b_ref, rc_ref, rs1_ref, rs2_ref, q_ref, k_ref, v_ref):
    hn = _rms(h, g_ref[...]).astype(BF16)
    c, s1, s2 = rc_ref[...], rs1_ref[...], rs2_ref[...]
    ko, vo = C_Q_WIDTH, C_Q_WIDTH + C_KV_WIDTH
    q = _dot(hn, w_ref[:, 0:ko]) + b_ref[:, 0:ko]
    q_ref[...] = (_rope(q, c, s1, s2) * (HEAD_DIM ** -0.5)).astype(BF16)
    kv = _dot(hn, w_ref[:, ko:vo + C_KV_WIDTH]) + b_ref[:, ko:vo + C_KV_WIDTH]
    k_ref[...] = _dup_heads(_rope(kv[:, :C_KV_WIDTH], c, s1, s2)).astype(BF16)
    v_ref[...] = _dup_heads(kv[:, C_KV_WIDTH:]).astype(BF16)


def _mem_kv_kernel(mem_ref, g_ref, w_ref, k_ref, v_ref):
    mn = _rms(mem_ref[...], g_ref[...]).astype(BF16)
    k_ref[...] = _dot(mn, w_ref[:, 0:XA_WIDTH]).astype(BF16)
    v_ref[...] = _dot(mn, w_ref[:, XA_WIDTH:2 * XA_WIDTH]).astype(BF16)


def _mem_kv(mem, g, wkv):
    bsz, m, d = mem.shape
    layers = wkv.shape[0]
    spec = pl.BlockSpec((None, bsz * m, XA_WIDTH), lambda l: (l, 0, 0))
    per_layer = lambda a: pl.BlockSpec((None,) + a.shape[1:], lambda l: (l, 0, 0))
    return pl.pallas_call(
        _mem_kv_kernel,
        grid=(layers,),
        in_specs=[pl.BlockSpec((bsz * m, d), lambda l: (0, 0)), per_layer(g), per_layer(wkv)],
        out_specs=(spec, spec),
        out_shape=(jax.ShapeDtypeStruct((layers, bsz * m, XA_WIDTH), BF16),) * 2,
        compiler_params=_params("arbitrary"),
        name="mem_kv",
    )(mem.reshape(bsz * m, d), g, wkv)


def _layer_tail_kernel(*refs, n_act, d_ff, last):
    h_ref, act_refs = refs[0], refs[1:1 + n_act]
    (wout_ref, bout_ref, xg_ref, wq_ref, mk_ref, mv_ref, wo_ref,
     fg_ref, wgu_ref, wd_ref) = refs[1 + n_act:11 + n_act]
    epilogue = refs[11 + n_act:]

    h = h_ref[...] + bout_ref[...]
    row = 0
    for a_ref in act_refs:
        width = a_ref.shape[-1]
        h = h + _dot(a_ref[...], wout_ref[row:row + width, :])
        row += width

    q = _dot(_rms(h, xg_ref[...]).astype(BF16), wq_ref[...]).astype(BF16)
    scale = XA_HEAD_DIM ** -0.5
    heads = []
    for hd in range(XA_HEADS):
        sl = slice(hd * XA_HEAD_DIM, (hd + 1) * XA_HEAD_DIM)
        s = _dot_nt(q[:, sl], mk_ref[:, sl]) * scale
        e = jnp.exp(s - jnp.max(s, axis=-1, keepdims=True))
        den = jnp.sum(e, axis=-1, keepdims=True)
        heads.append((_dot(e.astype(BF16), mv_ref[:, sl]) * (1.0 / den)).astype(BF16))
    h = h + _dot(jnp.concatenate(heads, axis=1), wo_ref[...])

    hn = _rms(h, fg_ref[...]).astype(BF16)
    for c0, width in _ff_chunks(d_ff):
        gate = _dot(hn, wgu_ref[:, c0:c0 + width])
        up = _dot(hn, wgu_ref[:, d_ff + c0:d_ff + c0 + width])
        act = (gate * jax.nn.sigmoid(gate) * up).astype(BF16)
        h = h + _dot(act, wd_ref[c0:c0 + width, :])
    if last:
        ng_ref, out_ref = epilogue
        out_ref[...] = _rms(h, ng_ref[...])
    else:
        out_ref = epilogue[6]
        out_ref[...] = h
        _swa_qkv(h, *epilogue[:6], *epilogue[7:])


def _layer_tail(h, acts, b_out, mk, mv, p, weights, layer, nxt=None, tm=TOK_TILE):
    bsz, seq, d = h.shape
    wgu, wd, w_out, wq, wo = weights
    d_ff = wd.shape[1]
    assert sum(a.shape[-1] for a in acts) == w_out.shape[1]
    tok = lambda width: pl.BlockSpec((None, tm, width), lambda b, i: (b, i, 0))
    kv = pl.BlockSpec((None, N_MEM, XA_WIDTH), lambda b, i: (layer, b, 0))
    res = jax.ShapeDtypeStruct((bsz, seq, d), F32)
    if nxt is None:
        extra, extra_specs = (p["ng"],), [_const_spec(p["ng"].shape)]
        out_specs, out_shape = tok(d), res
    else:
        g2, w2, b2, tables = nxt
        extra = (g2, w2, b2, tables, tables, tables)
        extra_specs = [_layer_spec(a.shape, 0) for a in (g2, w2, b2)] + _table_specs(tm)
        widths = (C_Q_WIDTH, 2 * C_KV_WIDTH, 2 * C_KV_WIDTH)
        out_specs = (tok(d),) + tuple(tok(w) for w in widths)
        out_shape = (res,) + tuple(jax.ShapeDtypeStruct((bsz, seq, w), BF16) for w in widths)
    return pl.pallas_call(
        functools.partial(_layer_tail_kernel, n_act=len(acts), d_ff=d_ff, last=nxt is None),
        grid=(bsz, seq // tm),
        in_specs=[tok(d)] + [tok(a.shape[-1]) for a in acts] + [
            _layer_spec(w_out.shape, 0), _layer_spec(b_out.shape, 0),
            _layer_spec(p["xg"].shape, layer), _layer_spec(wq.shape, 0), kv, kv,
            _layer_spec(wo.shape, 0), _layer_spec(p["fg"].shape, layer),
            _layer_spec(wgu.shape, 0), _layer_spec(wd.shape, 0)] + extra_specs,
        out_specs=out_specs,
        out_shape=out_shape,
        compiler_params=_params("arbitrary", "arbitrary"),
        name="layer_tail",
    )(h, *acts, w_out, b_out, p["xg"], wq, mk, mv, wo, p["fg"], wgu, wd, *extra)


def kernel(x, mem, mix_norm, ab_w_in, lru_conv_w, lru_conv_b, lru_wa, lru_ba, lru_wx, lru_bx, lru_lambda, ab_w_out, c_w_qkv, c_b_qkv, c_sinks, c_w_out, c_b_out, xa_norm, xa_mem_norm, xa_wq, xa_wkv, xa_wo, ffn_norm, ffn_w_gate_up, ffn_w_down, final_norm):
    bsz, seq, d = x.shape
    tables = _rope_tables(seq)
    rows = lambda a: a.reshape(a.shape[0], 1, -1)
    bf = lambda a: a.astype(BF16)
    mix_g = rows(mix_norm)
    p = dict(xg=rows(xa_norm), fg=rows(ffn_norm), ng=final_norm.reshape(1, d))
    mk, mv = _mem_kv(mem, rows(xa_mem_norm), bf(xa_wkv))

    rec, q, k, v = _lru_proj(x, mix_g, bf(ab_w_in), lru_conv_w, rows(lru_conv_b), bf(lru_wa),
                             rows(lru_ba), bf(lru_wx), rows(lru_bx), rows(lru_lambda), tables)
    att, *weights, w_qkv1 = _dilated_attention(
        q, k, v, (ffn_w_gate_up, ffn_w_down, ab_w_out, xa_wq, xa_wo, c_w_qkv), 0)
    h, q, k, v = _layer_tail(x, (rec, att), jnp.zeros((1, 1, d), F32), mk, mv, p, weights,
                             layer=0, nxt=(mix_g[1:2], w_qkv1, rows(c_b_qkv), tables))

    att, *weights = _swa_attention(
        q, k, v, c_sinks[0], (ffn_w_gate_up, ffn_w_down, c_w_out, xa_wq, xa_wo), 1)
    return _layer_tail(h, (att,), rows(c_b_out), mk, mv, p, weights, layer=1,
                       tm=LAST_TAIL_TILE)
```

```python
import functools
import math

import jax
import jax.numpy as jnp
from jax import lax
from jax.experimental import pallas as pl
from jax.experimental.pallas import tpu as pltpu

F32 = jnp.float32
BF16 = jnp.bfloat16

D_MODEL = 1024
N_MEM = 256
NORM_EPS = 1e-6
ROPE_THETA = 500000.0
HEAD_DIM = 64
ROT_DIM = HEAD_DIM // 4
ROT_HALF = ROT_DIM // 2
LANES = 128
SUBLANES = 8
BLOCK = 128
LRU_WIDTH = D_MODEL
LRU_HEADS = 4
LRU_HEAD_DIM = LRU_WIDTH // LRU_HEADS
CONV_WIDTH = 4
LRU_C = 8.0
B_HEADS = 8
B_WIDTH = B_HEADS * HEAD_DIM
DILATED_PATTERN = ((128, 1), (512, 4), (2048, 16))
C_HEADS = 16
C_KV_HEADS = 2
C_WINDOW = 128
C_Q_WIDTH = C_HEADS * HEAD_DIM
C_KV_WIDTH = C_KV_HEADS * HEAD_DIM
XA_HEADS = 4
XA_HEAD_DIM = 128
XA_WIDTH = XA_HEADS * XA_HEAD_DIM
NEG = -1e30
VMEM_LIMIT = 56 * 1024 * 1024

TOK_TILE = 512
MXU_TILE = 256
FF_CHUNK_TILES = 3
LRU_TILE = 1024
LAST_TAIL_TILE = 1024


def _ff_chunks(d_ff):
    assert d_ff % MXU_TILE == 0
    step = FF_CHUNK_TILES * MXU_TILE
    return [(c0, min(step, d_ff - c0)) for c0 in range(0, d_ff, step)]


def _dot(a, b):
    return jnp.dot(a, b, preferred_element_type=F32)


def _dot_nt(a, b):
    return lax.dot_general(a, b, (((1,), (1,)), ((), ())), preferred_element_type=F32)


def _rms(x, g):
    return x * lax.rsqrt(jnp.mean(x * x, axis=-1, keepdims=True) + NORM_EPS) * g


def _params(*sem):
    return pltpu.CompilerParams(dimension_semantics=sem, vmem_limit_bytes=VMEM_LIMIT)


def _const_spec(shape):
    nd = len(shape)
    return pl.BlockSpec(shape, lambda *_: (0,) * nd, pipeline_mode=pl.Buffered(1))


def _layer_spec(shape, layer):
    nd = len(shape)
    return pl.BlockSpec((None,) + tuple(shape[1:]), lambda *_: (layer,) + (0,) * (nd - 1),
                        pipeline_mode=pl.Buffered(1))


def _rope(x, c, s1, s2):
    cols = []
    for j in range(x.shape[1] // LANES):
        xc = x[:, j * LANES:(j + 1) * LANES]
        cols.append(xc * c + pltpu.roll(xc, ROT_HALF, 1) * s1
                    + pltpu.roll(xc, LANES - ROT_HALF, 1) * s2)
    return cols[0] if len(cols) == 1 else jnp.concatenate(cols, axis=1)


def _rope_tables(seq):
    lane = jnp.arange(LANES) % HEAD_DIM
    inv = ROPE_THETA ** (-jnp.arange(0, ROT_DIM, 2, dtype=F32) / ROT_DIM)
    ang = jnp.arange(seq, dtype=F32)[:, None] * inv[None, :]
    cos, sin = (jnp.tile(t, (1, LANES // ROT_HALF)) for t in (jnp.cos(ang), jnp.sin(ang)))
    c = jnp.where(lane < ROT_DIM, cos, 1.0)
    s1 = jnp.where(jnp.logical_and(lane >= ROT_HALF, lane < ROT_DIM), sin, 0.0)
    s2 = jnp.where(lane < ROT_HALF, -sin, 0.0)
    return jnp.stack([c, s1, s2])


def _table_specs(tm):
    return [pl.BlockSpec((None, tm, LANES), lambda b, i, j=j: (j, i, 0)) for j in range(3)]


def _sigmoid(x):
    return 0.5 * jnp.tanh(0.5 * x) + 0.5


def _shift_rows(x, k, head):
    rolled = pltpu.roll(x, k, 0)
    sub = lax.broadcasted_iota(jnp.int32, head.shape, 0)
    first = jnp.where(sub < k, pltpu.roll(head, k, 0), rolled[0:SUBLANES])
    return jnp.concatenate([first, rolled[SUBLANES:]], axis=0)


def _lru_proj_kernel(h_ref, g_ref, w_ref, cw_ref, cb_ref, wa_ref, ba_ref, wx_ref, bx_ref,
                     lam_ref, rc_ref, rs1_ref, rs2_ref,
                     rec_ref, q_ref, k_ref, v_ref,
                     hn_ref, tail_ref, as_ref, us_ref, gl_ref, hprev_ref, *, tl):
    @pl.when(pl.program_id(1) == 0)
    def _():
        tail_ref[...] = jnp.zeros_like(tail_ref)
        hprev_ref[...] = jnp.zeros_like(hprev_ref)

    hn_ref[...] = _rms(h_ref[...], g_ref[...]).astype(BF16)
    hn = hn_ref[...]
    c, s1, s2 = rc_ref[...], rs1_ref[...], rs2_ref[...]
    ones = jnp.ones((SUBLANES, LRU_HEAD_DIM), F32)
    zeros = jnp.zeros((SUBLANES, LRU_HEAD_DIM), F32)
    attn_out = (q_ref, k_ref, v_ref)

    group_cols = lambda hd: slice(hd * LRU_HEAD_DIM, (hd + 1) * LRU_HEAD_DIM)
    half = B_WIDTH // 2

    def attn_piece(hd, j):
        o = 2 * LRU_WIDTH + hd * B_WIDTH + j * half
        t = _dot(hn, w_ref[:, o:o + half])
        if hd < 2:
            t = _rope(t, c, s1, s2)
        if hd == 0:
            t = t * (HEAD_DIM ** -0.5)
        attn_out[hd][:, j * half:(j + 1) * half] = t.astype(BF16)

    x_next = _dot(hn, w_ref[:, group_cols(0)])
    for hd in range(LRU_HEADS):
        cols = group_cols(hd)
        x = x_next
        if hd + 1 < LRU_HEADS:
            x_next = _dot(hn, w_ref[:, group_cols(hd + 1)])

        tail = tail_ref[:, cols]
        xc = cb_ref[:, cols] + _shift_rows(x, 3, tail) * cw_ref[0:1, cols]
        xc = xc + _shift_rows(x, 2, tail) * cw_ref[1:2, cols]
        xc = xc + _shift_rows(x, 1, tail) * cw_ref[2:3, cols]
        xc = xc + x * cw_ref[3:4, cols]
        tail_ref[:, cols] = x[tl - SUBLANES:tl, :]

        y = _dot(hn, w_ref[:, LRU_WIDTH + hd * LRU_HEAD_DIM:LRU_WIDTH + (hd + 1) * LRU_HEAD_DIM])

        xcb = xc.astype(BF16)
        r = _sigmoid(_dot(xcb, wa_ref[hd]) + ba_ref[:, cols])
        ig = _sigmoid(_dot(xcb, wx_ref[hd]) + bx_ref[:, cols])
        nlam = -lam_ref[:, cols]
        softplus = jnp.maximum(nlam, 0.0) + jnp.log1p(jnp.exp(-jnp.abs(nlam)))
        neg_log_a = r * (LRU_C * softplus)
        a = jnp.exp2(r * ((-LRU_C * math.log2(math.e)) * softplus))
        u = jnp.sqrt(jnp.tanh(neg_log_a) * (1.0 + a * a)) * (ig * xc)

        if hd < len(attn_out):
            attn_piece(hd, 0)

        for s in (1, 2, 4):
            u = a * _shift_rows(u, s, zeros) + u
            a = a * _shift_rows(a, s, ones)
        as_ref[:, cols] = a
        us_ref[:, cols] = u

        if hd < len(attn_out):
            attn_piece(hd, 1)

        gl_ref[:, cols] = 0.5 * y * (1.0 + jnp.tanh(
            math.sqrt(2.0 / math.pi) * (y + 0.044715 * (y * y * y))))

    def chain(g, hcar):
        rows = pl.ds(pl.multiple_of(g * SUBLANES, SUBLANES), SUBLANES)
        hcar = as_ref[rows, :] * hcar + us_ref[rows, :]
        us_ref[rows, :] = hcar
        return hcar

    h0 = jnp.broadcast_to(hprev_ref[SUBLANES - 1:SUBLANES, :], (SUBLANES, LRU_WIDTH))
    hprev_ref[...] = lax.fori_loop(0, tl // SUBLANES, chain, h0, unroll=8)
    rec_ref[...] = (us_ref[...] * gl_ref[...]).astype(BF16)


def _lru_proj(h, g, w_in, conv_w, conv_b, wa, ba, wx, bx, lam, tables, tl=LRU_TILE):
    bsz, seq, d = h.shape
    tok = lambda w: pl.BlockSpec((None, tl, w), lambda b, i: (b, i, 0))
    out_shape = (jax.ShapeDtypeStruct((bsz, seq, LRU_WIDTH), BF16),) + tuple(
        jax.ShapeDtypeStruct((bsz, seq, B_WIDTH), BF16) for _ in range(3))
    params = (g, w_in, conv_w, conv_b, wa, ba, wx, bx, lam)
    return pl.pallas_call(
        functools.partial(_lru_proj_kernel, tl=tl),
        grid=(bsz, seq // tl),
        in_specs=[tok(d)] + [_layer_spec(a.shape, 0) for a in params] + _table_specs(tl),
        out_specs=(tok(LRU_WIDTH), tok(B_WIDTH), tok(B_WIDTH), tok(B_WIDTH)),
        out_shape=out_shape,
        scratch_shapes=[pltpu.VMEM((tl, d), BF16),
                        pltpu.VMEM((SUBLANES, LRU_WIDTH), F32),
                        pltpu.VMEM((tl, LRU_WIDTH), F32),
                        pltpu.VMEM((tl, LRU_WIDTH), F32),
                        pltpu.VMEM((tl, LRU_WIDTH), F32),
                        pltpu.VMEM((SUBLANES, LRU_WIDTH), F32)],
        compiler_params=_params("arbitrary", "arbitrary"),
        name="lru_proj",
    )(h, *params, tables, tables, tables)


def _sink_column(first):
    return (2 * BLOCK - 1) * (1 - first)


def _fill_band_bias(bias_ref, max_dist, reps, sink_of_rep=None):
    rows = lax.broadcasted_iota(jnp.int32, (BLOCK, 2 * BLOCK), 0)
    cols = lax.broadcasted_iota(jnp.int32, (BLOCK, 2 * BLOCK), 1)
    for first, off in ((0, 0), (1, BLOCK)):
        dist = off + rows - cols
        valid = jnp.logical_and(dist >= 0, dist <= max_dist)
        tile = jnp.where(valid, 0.0, NEG).astype(F32)
        for rep in range(reps):
            rep_tile = tile
            if sink_of_rep is not None:
                assert max_dist < BLOCK
                rep_tile = jnp.where(cols == _sink_column(first), sink_of_rep(rep), tile)
            bias_ref[first, rep * BLOCK:(rep + 1) * BLOCK, :] = rep_tile


def _band_attend(qb, kb, vb, bias, want_lse):
    r = qb.shape[0]
    lane_q = lax.broadcasted_iota(jnp.int32, qb.shape, 1)
    zero = jnp.zeros((), qb.dtype)
    qz = jnp.concatenate([jnp.where(lane_q < HEAD_DIM, qb, zero),
                          jnp.where(lane_q >= HEAD_DIM, qb, zero)], axis=0)
    s = _dot_nt(qz, kb) + bias
    m = jnp.max(s, axis=-1, keepdims=True)
    e = jnp.exp(s - m).astype(vb.dtype)
    ob = _dot(e, jnp.concatenate([vb, jnp.ones_like(vb)], axis=1))
    first_head = lax.broadcasted_iota(jnp.int32, (r, LANES), 1) < HEAD_DIM
    num = jnp.where(first_head, ob[:r, :LANES], ob[r:, :LANES])
    den = jnp.where(first_head, ob[:r, LANES:], ob[r:, LANES:])
    o = num * (1.0 / den)
    if not want_lse:
        return o, None
    return o, jnp.where(first_head, m[:r], m[r:]) + jnp.log(den)


def _band_window(n, nblk):
    first = jnp.minimum(lax.rem(n, nblk), 1)
    q0 = pl.multiple_of(n * BLOCK, BLOCK)
    ks = pl.multiple_of(q0 - first * BLOCK, BLOCK)
    return q0, ks, first


def _cast_specs(ws, layer, steps, step_of):
    in_specs, out_specs, out_shapes = [], [], []
    for w in ws:
        assert w.shape[1] % (steps * 2 * SUBLANES) == 0
        blk = (None, w.shape[1] // steps, w.shape[2])
        in_specs.append(pl.BlockSpec(blk, lambda *ids: (layer, step_of(*ids), 0)))
        out_specs.append(pl.BlockSpec(blk, lambda *ids: (0, step_of(*ids), 0)))
        out_shapes.append(jax.ShapeDtypeStruct((1,) + tuple(w.shape[1:]), BF16))
    return in_specs, out_specs, out_shapes


def _cast_slabs(src_refs, dst_refs):
    for src, dst in zip(src_refs, dst_refs):
        dst[...] = src[...].astype(dst.dtype)


def _dilated_kernel(*refs, seq, max_dist, n_cast):
    q_ref, k_ref, v_ref = refs[:3]
    out_ref = refs[3 + n_cast]
    stage_ref, p4_ref, perm_ref, o_ref, l_ref, bias_ref = refs[4 + 2 * n_cast:]
    _cast_slabs(refs[3:3 + n_cast], refs[4 + n_cast:4 + 2 * n_cast])
    _fill_band_bias(bias_ref, max_dist, 2)

    l4, l16 = seq // 4, seq // 16
    for a, x_ref in enumerate((q_ref, k_ref, v_ref)):
        stage_ref[...] = x_ref[...].astype(F32)
        for r in range(4):
            blk = stage_ref[pl.ds(r, l4, stride=4), :]
            p4_ref[r * l4:(r + 1) * l4, :] = blk
            perm_ref[0, a, r * l4:(r + 1) * l4, :] = blk.astype(BF16)
        for r in range(4):
            for rb in range(4):
                blk = p4_ref[pl.ds(r * l4 + rb, l16, stride=4), :]
                res = r + 4 * rb
                perm_ref[1, a, res * l16:(res + 1) * l16, :] = blk.astype(BF16)

    branches = ((1, (q_ref, k_ref, v_ref)),
                (4, tuple(perm_ref.at[0, a] for a in range(3))),
                (16, tuple(perm_ref.at[1, a] for a in range(3))))
    for bi, (d, (qr, kr, vr)) in enumerate(branches):
        nblk = seq // d // BLOCK

        def body(n, carry, d=d, bi=bi, qr=qr, kr=kr, vr=vr, nblk=nblk):
            q0, ks, first = _band_window(n, nblk)
            o, lse = _band_attend(qr[pl.ds(q0, BLOCK), :], kr[pl.ds(ks, 2 * BLOCK), :],
                                  vr[pl.ds(ks, 2 * BLOCK), :], bias_ref[first], True)
            if d == 1:
                rows = pl.ds(q0, BLOCK)
            else:
                rows = pl.ds(lax.rem(n, nblk) * (BLOCK * d) + n // nblk, BLOCK, stride=d)
            o_ref.at[bi][rows, :] = o
            l_ref.at[bi][rows, :] = lse
            return carry

        lax.fori_loop(0, seq // BLOCK, body, 0, unroll=32)

    chunk = 4 * BLOCK

    def merge(c, carry):
        rows = pl.ds(pl.multiple_of(c * chunk, chunk), chunk)
        ls = [l_ref[bi, rows, :] for bi in range(3)]
        m = jnp.maximum(jnp.maximum(ls[0], ls[1]), ls[2])
        ws = [jnp.exp(l - m) for l in ls]
        acc = ws[0] * o_ref[0, rows, :] + ws[1] * o_ref[1, rows, :] + ws[2] * o_ref[2, rows, :]
        out_ref[rows, :] = (acc * (1.0 / (ws[0] + ws[1] + ws[2]))).astype(out_ref.dtype)
        return carry

    lax.fori_loop(0, seq // chunk, merge, 0)


def _dilated_attention(q, k, v, cast, cast_layer):
    assert tuple(d for _, d in DILATED_PATTERN) == (1, 4, 16)
    assert len({w // d for w, d in DILATED_PATTERN}) == 1
    max_dist = DILATED_PATTERN[0][0] // DILATED_PATTERN[0][1]
    bsz, seq, w = q.shape
    cols = w // LANES
    spec = pl.BlockSpec((None, seq, LANES), lambda b, c: (b, 0, c))
    cast_in, cast_out, cast_shapes = _cast_specs(cast, cast_layer, bsz * cols,
                                                 lambda b, c: b * cols + c)
    return pl.pallas_call(
        functools.partial(_dilated_kernel, seq=seq, max_dist=max_dist, n_cast=len(cast)),
        grid=(bsz, cols),
        in_specs=[spec, spec, spec] + cast_in,
        out_specs=[spec] + cast_out,
        out_shape=[jax.ShapeDtypeStruct((bsz, seq, w), BF16)] + cast_shapes,
        scratch_shapes=[pltpu.VMEM((seq, LANES), F32), pltpu.VMEM((seq, LANES), F32),
                        pltpu.VMEM((2, 3, seq, LANES), BF16),
                        pltpu.VMEM((3, seq, LANES), F32), pltpu.VMEM((3, seq, LANES), F32),
                        pltpu.VMEM((2, 2 * BLOCK, 2 * BLOCK), F32)],
        compiler_params=_params("arbitrary", "arbitrary"),
        name="dilated",
    )(q, k, v, *cast)


def _swa_kernel(*refs, nblk, max_dist, pairs, n_cast):
    sink_ref, q_ref, k_ref, v_ref = refs[:4]
    o_ref = refs[4 + n_cast]
    bias_ref = refs[5 + 2 * n_cast]
    _cast_slabs(refs[4:4 + n_cast], refs[5 + n_cast:5 + 2 * n_cast])
    head0 = pl.program_id(1) * (2 * pairs)
    _fill_band_bias(bias_ref, max_dist, 2 * pairs,
                    lambda rep: sink_ref[head0 + 2 * (rep % pairs) + rep // pairs])
    key_row = lax.broadcasted_iota(jnp.int32, (2 * BLOCK, LANES), 0)

    def body(i, carry):
        q0, ks, first = _band_window(i, nblk)
        qrow = q_ref[pl.ds(q0, BLOCK), :]
        qb = jnp.concatenate([qrow[:, p * LANES:(p + 1) * LANES] for p in range(pairs)], axis=0)
        unseen = key_row == _sink_column(first)
        zero = jnp.zeros((), k_ref.dtype)
        kb = jnp.where(unseen, zero, k_ref[pl.ds(ks, 2 * BLOCK), :])
        vb = jnp.where(unseen, zero, v_ref[pl.ds(ks, 2 * BLOCK), :])
        o, _ = _band_attend(qb, kb, vb, bias_ref[first], False)
        o_ref[pl.ds(q0, BLOCK), :] = jnp.concatenate(
            [o[p * BLOCK:(p + 1) * BLOCK, :] for p in range(pairs)], axis=1).astype(o_ref.dtype)
        return carry

    lax.fori_loop(0, nblk, body, 0, unroll=16)


def _swa_attention(q, kdup, vdup, sinks, cast, cast_layer):
    bsz, seq, w = q.shape
    pairs = w // LANES // C_KV_HEADS
    qspec = pl.BlockSpec((None, seq, pairs * LANES), lambda b, g: (b, 0, g))
    kvspec = pl.BlockSpec((None, seq, LANES), lambda b, g: (b, 0, g))
    cast_in, cast_out, cast_shapes = _cast_specs(cast, cast_layer, bsz * C_KV_HEADS,
                                                 lambda b, g: b * C_KV_HEADS + g)
    return pl.pallas_call(
        functools.partial(_swa_kernel, nblk=seq // BLOCK, max_dist=C_WINDOW - 1, pairs=pairs,
                          n_cast=len(cast)),
        grid=(bsz, C_KV_HEADS),
        in_specs=[pl.BlockSpec(memory_space=pltpu.SMEM), qspec, kvspec, kvspec] + cast_in,
        out_specs=[qspec] + cast_out,
        out_shape=[jax.ShapeDtypeStruct((bsz, seq, w), BF16)] + cast_shapes,
        scratch_shapes=[pltpu.VMEM((2, 2 * pairs * BLOCK, 2 * BLOCK), F32)],
        compiler_params=_params("arbitrary", "arbitrary"),
        name="swa_sink",
    )(sinks, q, kdup, vdup, *cast)


def _dup_heads(a):
    swapped = pltpu.roll(a, HEAD_DIM, 1)
    low = lax.broadcasted_iota(jnp.int32, a.shape, 1) < HEAD_DIM
    return jnp.concatenate([jnp.where(low, a, swapped), jnp.where(low, swapped, a)], axis=1)


def _swa_qkv(h, g_ref, w_ref, b_ref, rc_ref, rs1_ref, rs2_ref, q_ref, k_ref, v_ref):
    hn = _rms(h, g_ref[...]).astype(BF16)
    c, s1, s2 = rc_ref[...], rs1_ref[...], rs2_ref[...]
    ko, vo = C_Q_WIDTH, C_Q_WIDTH + C_KV_WIDTH
    q = _dot(hn, w_ref[:, 0:ko]) + b_ref[:, 0:ko]
    q_ref[...] = (_rope(q, c, s1, s2) * (HEAD_DIM ** -0.5)).astype(BF16)
    kv = _dot(hn, w_ref[:, ko:vo + C_KV_WIDTH]) + b_ref[:, ko:vo + C_KV_WIDTH]
    k_ref[...] = _dup_heads(_rope(kv[:, :C_KV_WIDTH], c, s1, s2)).astype(BF16)
    v_ref[...] = _dup_heads(kv[:, C_KV_WIDTH:]).astype(BF16)


def _mem_kv_kernel(mem_ref, g_ref, w_ref, k_ref, v_ref):
    mn = _rms(mem_ref[...], g_ref[...]).astype(BF16)
    k_ref[...] = _dot(mn, w_ref[:, 0:XA_WIDTH]).astype(BF16)
    v_ref[...] = _dot(mn, w_ref[:, XA_WIDTH:2 * XA_WIDTH]).astype(BF16)


def _mem_kv(mem, g, wkv):
    bsz, m, d = mem.shape
    layers = wkv.shape[0]
    spec = pl.BlockSpec((None, bsz * m, XA_WIDTH), lambda l: (l, 0, 0))
    per_layer = lambda a: pl.BlockSpec((None,) + a.shape[1:], lambda l: (l, 0, 0))
    return pl.pallas_call(
        _mem_kv_kernel,
        grid=(layers,),
        in_specs=[pl.BlockSpec((bsz * m, d), lambda l: (0, 0)), per_layer(g), per_layer(wkv)],
        out_specs=(spec, spec),
        out_shape=(jax.ShapeDtypeStruct((layers, bsz * m, XA_WIDTH), BF16),) * 2,
        compiler_params=_params("arbitrary"),
        name="mem_kv",
    )(mem.reshape(bsz * m, d), g, wkv)


def _layer_tail_kernel(*refs, n_act, d_ff, last):
    h_ref, act_refs = refs[0], refs[1:1 + n_act]
    (wout_ref, bout_ref, xg_ref, wq_ref, mk_ref, mv_ref, wo_ref,
     fg_ref, wgu_ref, wd_ref) = refs[1 + n_act:11 + n_act]
    epilogue = refs[11 + n_act:]

    h = h_ref[...] + bout_ref[...]
    row = 0
    for a_ref in act_refs:
        width = a_ref.shape[-1]
        h = h + _dot(a_ref[...], wout_ref[row:row + width, :])
        row += width

    q = _dot(_rms(h, xg_ref[...]).astype(BF16), wq_ref[...]).astype(BF16)
    scale = XA_HEAD_DIM ** -0.5
    heads = []
    for hd in range(XA_HEADS):
        sl = slice(hd * XA_HEAD_DIM, (hd + 1) * XA_HEAD_DIM)
        s = _dot_nt(q[:, sl], mk_ref[:, sl]) * scale
        e = jnp.exp(s - jnp.max(s, axis=-1, keepdims=True))
        den = jnp.sum(e, axis=-1, keepdims=True)
        heads.append((_dot(e.astype(BF16), mv_ref[:, sl]) * (1.0 / den)).astype(BF16))
    h = h + _dot(jnp.concatenate(heads, axis=1), wo_ref[...])

    hn = _rms(h, fg_ref[...]).astype(BF16)
    for c0, width in _ff_chunks(d_ff):
        gate = _dot(hn, wgu_ref[:, c0:c0 + width])
        up = _dot(hn, wgu_ref[:, d_ff + c0:d_ff + c0 + width])
        act = (gate * jax.nn.sigmoid(gate) * up).astype(BF16)
        h = h + _dot(act, wd_ref[c0:c0 + width, :])
    if last:
        ng_ref, out_ref = epilogue
        out_ref[...] = _rms(h, ng_ref[...])
    else:
        out_ref = epilogue[6]
        out_ref[...] = h
        _swa_qkv(h, *epilogue[:6], *epilogue[7:])


def _layer_tail(h, acts, w_out, b_out, mk, mv, p, ffn, layer, nxt=None, tm=TOK_TILE):
    bsz, seq, d = h.shape
    wgu, wd = ffn
    d_ff = wd.shape[1]
    assert sum(a.shape[-1] for a in acts) == w_out.shape[1]
    tok = lambda width: pl.BlockSpec((None, tm, width), lambda b, i: (b, i, 0))
    kv = pl.BlockSpec((None, N_MEM, XA_WIDTH), lambda b, i: (layer, b, 0))
    res = jax.ShapeDtypeStruct((bsz, seq, d), F32)
    if nxt is None:
        extra, extra_specs = (p["ng"],), [_const_spec(p["ng"].shape)]
        out_specs, out_shape = tok(d), res
    else:
        g2, w2, b2, tables = nxt
        extra = (g2, w2, b2, tables, tables, tables)
        extra_specs = [_layer_spec(a.shape, 0) for a in (g2, w2, b2)] + _table_specs(tm)
        widths = (C_Q_WIDTH, 2 * C_KV_WIDTH, 2 * C_KV_WIDTH)
        out_specs = (tok(d),) + tuple(tok(w) for w in widths)
        out_shape = (res,) + tuple(jax.ShapeDtypeStruct((bsz, seq, w), BF16) for w in widths)
    return pl.pallas_call(
        functools.partial(_layer_tail_kernel, n_act=len(acts), d_ff=d_ff, last=nxt is None),
        grid=(bsz, seq // tm),
        in_specs=[tok(d)] + [tok(a.shape[-1]) for a in acts] + [
            _layer_spec(w_out.shape, 0), _layer_spec(b_out.shape, 0),
            _layer_spec(p["xg"].shape, layer), _layer_spec(p["wq"].shape, layer), kv, kv,
            _layer_spec(p["wo"].shape, layer), _layer_spec(p["fg"].shape, layer),
            _layer_spec(wgu.shape, 0), _layer_spec(wd.shape, 0)] + extra_specs,
        out_specs=out_specs,
        out_shape=out_shape,
        compiler_params=_params("arbitrary", "arbitrary"),
        name="layer_tail",
    )(h, *acts, w_out, b_out, p["xg"], p["wq"], mk, mv, p["wo"], p["fg"], wgu, wd,
      *extra)


def kernel(x, mem, mix_norm, ab_w_in, lru_conv_w, lru_conv_b, lru_wa, lru_ba, lru_wx, lru_bx, lru_lambda, ab_w_out, c_w_qkv, c_b_qkv, c_sinks, c_w_out, c_b_out, xa_norm, xa_mem_norm, xa_wq, xa_wkv, xa_wo, ffn_norm, ffn_w_gate_up, ffn_w_down, final_norm):
    bsz, seq, d = x.shape
    tables = _rope_tables(seq)
    rows = lambda a: a.reshape(a.shape[0], 1, -1)
    bf = lambda a: a.astype(BF16)
    mix_g = rows(mix_norm)
    p = dict(xg=rows(xa_norm), wq=bf(xa_wq), wo=bf(xa_wo), fg=rows(ffn_norm),
             ng=final_norm.reshape(1, d))
    ffn_f32 = (ffn_w_gate_up, ffn_w_down)
    mk, mv = _mem_kv(mem, rows(xa_mem_norm), bf(xa_wkv))

    rec, q, k, v = _lru_proj(x, mix_g, bf(ab_w_in), lru_conv_w, rows(lru_conv_b), bf(lru_wa),
                             rows(lru_ba), bf(lru_wx), rows(lru_bx), rows(lru_lambda), tables)
    att, *ffn = _dilated_attention(q, k, v, ffn_f32, 0)
    h, q, k, v = _layer_tail(x, (rec, att), bf(ab_w_out), jnp.zeros((1, 1, d), F32), mk, mv, p,
                             ffn, layer=0,
                             nxt=(mix_g[1:2], bf(c_w_qkv), rows(c_b_qkv), tables))

    att, *ffn = _swa_attention(q, k, v, c_sinks[0], ffn_f32, 1)
    return _layer_tail(h, (att,), bf(c_w_out), rows(c_b_out), mk, mv, p, ffn, layer=1,
                       tm=LAST_TAIL_TILE)
```

```python
import functools
import math

import jax
import jax.numpy as jnp
from jax import lax
from jax.experimental import pallas as pl
from jax.experimental.pallas import tpu as pltpu

F32 = jnp.float32
BF16 = jnp.bfloat16

D_MODEL = 1024
N_MEM = 256
NORM_EPS = 1e-6
ROPE_THETA = 500000.0
HEAD_DIM = 64
ROT_DIM = HEAD_DIM // 4
ROT_HALF = ROT_DIM // 2
LANES = 128
SUBLANES = 8
BLOCK = 128
LRU_WIDTH = D_MODEL
LRU_HEADS = 4
LRU_HEAD_DIM = LRU_WIDTH // LRU_HEADS
CONV_WIDTH = 4
LRU_C = 8.0
B_HEADS = 8
B_WIDTH = B_HEADS * HEAD_DIM
DILATED_PATTERN = ((128, 1), (512, 4), (2048, 16))
C_HEADS = 16
C_KV_HEADS = 2
C_WINDOW = 128
C_Q_WIDTH = C_HEADS * HEAD_DIM
C_KV_WIDTH = C_KV_HEADS * HEAD_DIM
XA_HEADS = 4
XA_HEAD_DIM = 128
XA_WIDTH = XA_HEADS * XA_HEAD_DIM
NEG = -1e30
VMEM_LIMIT = 56 * 1024 * 1024

TOK_TILE = 512
MXU_TILE = 256
FF_CHUNK_TILES = 3
LRU_TILE = 1024
LAST_TAIL_TILE = 1024


def _ff_chunks(d_ff):
    assert d_ff % MXU_TILE == 0
    step = FF_CHUNK_TILES * MXU_TILE
    return [(c0, min(step, d_ff - c0)) for c0 in range(0, d_ff, step)]


def _dot(a, b):
    return jnp.dot(a, b, preferred_element_type=F32)


def _dot_nt(a, b):
    return lax.dot_general(a, b, (((1,), (1,)), ((), ())), preferred_element_type=F32)


def _rms(x, g):
    return x * lax.rsqrt(jnp.mean(x * x, axis=-1, keepdims=True) + NORM_EPS) * g


def _params(*sem):
    return pltpu.CompilerParams(dimension_semantics=sem, vmem_limit_bytes=VMEM_LIMIT)


def _const_spec(shape):
    nd = len(shape)
    return pl.BlockSpec(shape, lambda *_: (0,) * nd, pipeline_mode=pl.Buffered(1))


def _layer_spec(shape, layer):
    nd = len(shape)
    return pl.BlockSpec((None,) + tuple(shape[1:]), lambda *_: (layer,) + (0,) * (nd - 1),
                        pipeline_mode=pl.Buffered(1))


def _rope(x, c, s1, s2):
    cols = []
    for j in range(x.shape[1] // LANES):
        xc = x[:, j * LANES:(j + 1) * LANES]
        cols.append(xc * c + pltpu.roll(xc, ROT_HALF, 1) * s1
                    + pltpu.roll(xc, LANES - ROT_HALF, 1) * s2)
    return cols[0] if len(cols) == 1 else jnp.concatenate(cols, axis=1)


def _rope_tables(seq):
    lane = jnp.arange(LANES) % HEAD_DIM
    inv = ROPE_THETA ** (-(2 * (lane % ROT_HALF)).astype(F32) / ROT_DIM)
    ang = jnp.arange(seq, dtype=F32)[:, None] * inv[None, :]
    cos, sin = jnp.cos(ang), jnp.sin(ang)
    c = jnp.where(lane < ROT_DIM, cos, 1.0)
    s1 = jnp.where(jnp.logical_and(lane >= ROT_HALF, lane < ROT_DIM), sin, 0.0)
    s2 = jnp.where(lane < ROT_HALF, -sin, 0.0)
    return jnp.stack([c, s1, s2])


def _table_specs(tm):
    return [pl.BlockSpec((None, tm, LANES), lambda b, i, j=j: (j, i, 0)) for j in range(3)]


def _sigmoid(x):
    return 0.5 * jnp.tanh(0.5 * x) + 0.5


SEG_PAD = SUBLANES


def _to_segments(x, pad_ref):
    tl, w = x.shape
    seg = tl // SUBLANES
    pitch = seg + SEG_PAD
    for c in range(w // LANES):
        for s in range(SUBLANES):
            pad_ref[c, s * pitch:s * pitch + seg, :] = x[s * seg:(s + 1) * seg,
                                                         c * LANES:(c + 1) * LANES]
    groups = [jnp.concatenate([pad_ref[c, pl.ds(j, SUBLANES, stride=pitch), :]
                               for c in range(w // LANES)], axis=1) for j in range(seg)]
    return jnp.concatenate(groups, axis=0)


def _shift_segments(xs, k, prev):
    last = xs[xs.shape[0] - k * SUBLANES:, :]
    sub = lax.broadcasted_iota(jnp.int32, last.shape, 0) % SUBLANES
    head = jnp.where(sub == 0, _roll_groups(prev, k), _roll_groups(last, k))
    return jnp.concatenate([head, xs[:xs.shape[0] - k * SUBLANES, :]], axis=0)


def _roll_groups(x, k):
    return jnp.concatenate([pltpu.roll(x[g * SUBLANES:(g + 1) * SUBLANES, :], 1, 0)
                            for g in range(k)], axis=0)


def _lru_proj_kernel(h_ref, g_ref, w_ref, cw_ref, cb_ref, wa_ref, ba_ref, wx_ref, bx_ref,
                     lam_ref, rc_ref, rs1_ref, rs2_ref,
                     rec_ref, q_ref, k_ref, v_ref,
                     hn_ref, tail_ref, as_ref, us_ref, gl_ref, hprev_ref, xpad_ref, hpad_ref, *, tl):
    seg = tl // SUBLANES
    pitch = seg + SEG_PAD

    @pl.when(pl.program_id(1) == 0)
    def _():
        tail_ref[...] = jnp.zeros_like(tail_ref)
        hprev_ref[...] = jnp.zeros_like(hprev_ref)

    hn_ref[...] = _rms(h_ref[...], g_ref[...]).astype(BF16)
    hn = hn_ref[...]
    c, s1, s2 = rc_ref[...], rs1_ref[...], rs2_ref[...]
    attn_out = (q_ref, k_ref, v_ref)

    group_cols = lambda hd: slice(hd * LRU_HEAD_DIM, (hd + 1) * LRU_HEAD_DIM)
    half = B_WIDTH // 2

    def attn_piece(hd, j):
        o = 2 * LRU_WIDTH + hd * B_WIDTH + j * half
        t = _dot(hn, w_ref[:, o:o + half])
        if hd < 2:
            t = _rope(t, c, s1, s2)
        if hd == 0:
            t = t * (HEAD_DIM ** -0.5)
        attn_out[hd][:, j * half:(j + 1) * half] = t.astype(BF16)

    x_next = _dot(hn, w_ref[:, group_cols(0)])
    for hd in range(LRU_HEADS):
        cols = group_cols(hd)
        x = x_next
        if hd + 1 < LRU_HEADS:
            x_next = _dot(hn, w_ref[:, group_cols(hd + 1)])

        x = _to_segments(x, xpad_ref)
        tail = tail_ref[:, cols]
        taps = CONV_WIDTH - 1
        xc = cb_ref[:, cols] + x * cw_ref[taps:taps + 1, cols]
        for k in range(1, CONV_WIDTH):
            prev = tail[(taps - k) * SUBLANES:, :]
            xc = xc + _shift_segments(x, k, prev) * cw_ref[taps - k:taps - k + 1, cols]
        tail_ref[:, cols] = x[tl - taps * SUBLANES:tl, :]

        y = _dot(hn, w_ref[:, LRU_WIDTH + hd * LRU_HEAD_DIM:LRU_WIDTH + (hd + 1) * LRU_HEAD_DIM])

        xcb = xc.astype(BF16)
        r = _sigmoid(_dot(xcb, wa_ref[hd]) + ba_ref[:, cols])
        ig = _sigmoid(_dot(xcb, wx_ref[hd]) + bx_ref[:, cols])
        nlam = -lam_ref[:, cols]
        softplus = jnp.maximum(nlam, 0.0) + jnp.log1p(jnp.exp(-jnp.abs(nlam)))
        neg_log_a = r * (LRU_C * softplus)
        a = jnp.exp2(r * ((-LRU_C * math.log2(math.e)) * softplus))
        u = jnp.sqrt(jnp.tanh(neg_log_a) * (1.0 + a * a)) * (ig * xc)

        if hd < len(attn_out):
            attn_piece(hd, 0)

        as_ref[:, cols] = a
        us_ref[:, cols] = u

        if hd < len(attn_out):
            attn_piece(hd, 1)

        gl_ref[:, cols] = 0.5 * y * (1.0 + jnp.tanh(
            math.sqrt(2.0 / math.pi) * (y + 0.044715 * (y * y * y))))

    def chain(j, carry):
        hloc, prod = carry
        rows = pl.ds(pl.multiple_of(j * SUBLANES, SUBLANES), SUBLANES)
        a = as_ref[rows, :]
        hloc = a * hloc + us_ref[rows, :]
        prod = a * prod
        us_ref[rows, :] = hloc
        as_ref[rows, :] = prod
        return hloc, prod

    init = (jnp.zeros((SUBLANES, LRU_WIDTH), F32), jnp.ones((SUBLANES, LRU_WIDTH), F32))
    hfin, pfin = lax.fori_loop(0, seg, chain, init, unroll=8)

    state = hprev_ref[0:1, :]
    entering = []
    for s in range(SUBLANES):
        entering.append(state)
        state = pfin[s:s + 1, :] * state + hfin[s:s + 1, :]
    hprev_ref[...] = jnp.broadcast_to(state, hprev_ref.shape)
    entering = jnp.concatenate(entering, axis=0)

    def restore(j, carry):
        rows = pl.ds(pl.multiple_of(j * SUBLANES, SUBLANES), SUBLANES)
        hrow = us_ref[rows, :] + as_ref[rows, :] * entering
        for cidx in range(LRU_WIDTH // LANES):
            hpad_ref[cidx, pl.ds(j, SUBLANES, stride=pitch), :] = hrow[:, cidx * LANES:
                                                                       (cidx + 1) * LANES]
        return carry

    lax.fori_loop(0, seg, restore, 0, unroll=8)
    for s in range(SUBLANES):
        rows = slice(s * seg, (s + 1) * seg)
        hs = jnp.concatenate([hpad_ref[cidx, s * pitch:s * pitch + seg, :]
                              for cidx in range(LRU_WIDTH // LANES)], axis=1)
        rec_ref[rows, :] = (hs * gl_ref[rows, :]).astype(BF16)


def _lru_proj(h, g, w_in, conv_w, conv_b, wa, ba, wx, bx, lam, tables, tl=LRU_TILE):
    bsz, seq, d = h.shape
    tok = lambda w: pl.BlockSpec((None, tl, w), lambda b, i: (b, i, 0))
    out_shape = (jax.ShapeDtypeStruct((bsz, seq, LRU_WIDTH), BF16),) + tuple(
        jax.ShapeDtypeStruct((bsz, seq, B_WIDTH), BF16) for _ in range(3))
    params = (g, w_in, conv_w, conv_b, wa, ba, wx, bx, lam)
    return pl.pallas_call(
        functools.partial(_lru_proj_kernel, tl=tl),
        grid=(bsz, seq // tl),
        in_specs=[tok(d)] + [_layer_spec(a.shape, 0) for a in params] + _table_specs(tl),
        out_specs=(tok(LRU_WIDTH), tok(B_WIDTH), tok(B_WIDTH), tok(B_WIDTH)),
        out_shape=out_shape,
        scratch_shapes=[pltpu.VMEM((tl, d), BF16),
                        pltpu.VMEM(((CONV_WIDTH - 1) * SUBLANES, LRU_WIDTH), F32),
                        pltpu.VMEM((tl, LRU_WIDTH), F32),
                        pltpu.VMEM((tl, LRU_WIDTH), F32),
                        pltpu.VMEM((tl, LRU_WIDTH), F32),
                        pltpu.VMEM((SUBLANES, LRU_WIDTH), F32),
                        pltpu.VMEM((LRU_HEAD_DIM // LANES, tl + SUBLANES * SEG_PAD, LANES), F32),
                        pltpu.VMEM((LRU_WIDTH // LANES, tl + SUBLANES * SEG_PAD, LANES), F32)],
        compiler_params=_params("arbitrary", "arbitrary"),
        name="lru_proj",
    )(h, *params, tables, tables, tables)


def _sink_column(first):
    return (2 * BLOCK - 1) * (1 - first)


def _fill_band_bias(bias_ref, max_dist, reps, sink_of_rep=None):
    rows = lax.broadcasted_iota(jnp.int32, (BLOCK, 2 * BLOCK), 0)
    cols = lax.broadcasted_iota(jnp.int32, (BLOCK, 2 * BLOCK), 1)
    for first, off in ((0, 0), (1, BLOCK)):
        dist = off + rows - cols
        valid = jnp.logical_and(dist >= 0, dist <= max_dist)
        tile = jnp.where(valid, 0.0, NEG).astype(F32)
        for rep in range(reps):
            rep_tile = tile
            if sink_of_rep is not None:
                assert max_dist < BLOCK
                rep_tile = jnp.where(cols == _sink_column(first), sink_of_rep(rep), tile)
            bias_ref[first, rep * BLOCK:(rep + 1) * BLOCK, :] = rep_tile


def _band_attend(qb, kb, vb, bias, want_lse):
    r = qb.shape[0]
    lane_q = lax.broadcasted_iota(jnp.int32, qb.shape, 1)
    zero = jnp.zeros((), qb.dtype)
    qz = jnp.concatenate([jnp.where(lane_q < HEAD_DIM, qb, zero),
                          jnp.where(lane_q >= HEAD_DIM, qb, zero)], axis=0)
    s = _dot_nt(qz, kb) + bias
    m = jnp.max(s, axis=-1, keepdims=True)
    e = jnp.exp(s - m).astype(vb.dtype)
    ob = _dot(e, jnp.concatenate([vb, jnp.ones_like(vb)], axis=1))
    first_head = lax.broadcasted_iota(jnp.int32, (r, LANES), 1) < HEAD_DIM
    num = jnp.where(first_head, ob[:r, :LANES], ob[r:, :LANES])
    den = jnp.where(first_head, ob[:r, LANES:], ob[r:, LANES:])
    o = num * (1.0 / den)
    if not want_lse:
        return o, None
    return o, jnp.where(first_head, m[:r], m[r:]) + jnp.log(den)


def _band_window(n, nblk):
    first = jnp.minimum(lax.rem(n, nblk), 1)
    q0 = pl.multiple_of(n * BLOCK, BLOCK)
    ks = pl.multiple_of(q0 - first * BLOCK, BLOCK)
    return q0, ks, first


def _cast_specs(ws, layer, steps, step_of):
    in_specs, out_specs, out_shapes = [], [], []
    for w in ws:
        assert w.shape[1] % (steps * 2 * SUBLANES) == 0
        blk = (None, w.shape[1] // steps, w.shape[2])
        in_specs.append(pl.BlockSpec(blk, lambda *ids: (layer, step_of(*ids), 0)))
        out_specs.append(pl.BlockSpec(blk, lambda *ids: (0, step_of(*ids), 0)))
        out_shapes.append(jax.ShapeDtypeStruct((1,) + tuple(w.shape[1:]), BF16))
    return in_specs, out_specs, out_shapes


def _cast_slabs(src_refs, dst_refs):
    for src, dst in zip(src_refs, dst_refs):
        dst[...] = src[...].astype(dst.dtype)


def _dilated_kernel(*refs, seq, max_dist, n_cast):
    q_ref, k_ref, v_ref = refs[:3]
    out_ref = refs[3 + n_cast]
    stage_ref, p4_ref, perm_ref, o_ref, l_ref, bias_ref = refs[4 + 2 * n_cast:]
    _cast_slabs(refs[3:3 + n_cast], refs[4 + n_cast:4 + 2 * n_cast])
    _fill_band_bias(bias_ref, max_dist, 2)

    l4, l16 = seq // 4, seq // 16
    for a, x_ref in enumerate((q_ref, k_ref, v_ref)):
        stage_ref[...] = x_ref[...].astype(F32)
        for r in range(4):
            blk = stage_ref[pl.ds(r, l4, stride=4), :]
            p4_ref[r * l4:(r + 1) * l4, :] = blk
            perm_ref[0, a, r * l4:(r + 1) * l4, :] = blk.astype(BF16)
        for r in range(4):
            for rb in range(4):
                blk = p4_ref[pl.ds(r * l4 + rb, l16, stride=4), :]
                res = r + 4 * rb
                perm_ref[1, a, res * l16:(res + 1) * l16, :] = blk.astype(BF16)

    branches = ((1, (q_ref, k_ref, v_ref)),
                (4, tuple(perm_ref.at[0, a] for a in range(3))),
                (16, tuple(perm_ref.at[1, a] for a in range(3))))
    for bi, (d, (qr, kr, vr)) in enumerate(branches):
        nblk = seq // d // BLOCK

        def body(n, carry, d=d, bi=bi, qr=qr, kr=kr, vr=vr, nblk=nblk):
            q0, ks, first = _band_window(n, nblk)
            o, lse = _band_attend(qr[pl.ds(q0, BLOCK), :], kr[pl.ds(ks, 2 * BLOCK), :],
                                  vr[pl.ds(ks, 2 * BLOCK), :], bias_ref[first], True)
            if d == 1:
                rows = pl.ds(q0, BLOCK)
            else:
                rows = pl.ds(lax.rem(n, nblk) * (BLOCK * d) + n // nblk, BLOCK, stride=d)
            o_ref.at[bi][rows, :] = o
            l_ref.at[bi][rows, :] = lse
            return carry

        lax.fori_loop(0, seq // BLOCK, body, 0, unroll=32)

    chunk = 4 * BLOCK

    def merge(c, carry):
        rows = pl.ds(pl.multiple_of(c * chunk, chunk), chunk)
        ls = [l_ref[bi, rows, :] for bi in range(3)]
        m = jnp.maximum(jnp.maximum(ls[0], ls[1]), ls[2])
        ws = [jnp.exp(l - m) for l in ls]
        acc = ws[0] * o_ref[0, rows, :] + ws[1] * o_ref[1, rows, :] + ws[2] * o_ref[2, rows, :]
        out_ref[rows, :] = (acc * (1.0 / (ws[0] + ws[1] + ws[2]))).astype(out_ref.dtype)
        return carry

    lax.fori_loop(0, seq // chunk, merge, 0)


def _dilated_attention(q, k, v, cast, cast_layer):
    assert tuple(d for _, d in DILATED_PATTERN) == (1, 4, 16)
    assert len({w // d for w, d in DILATED_PATTERN}) == 1
    max_dist = DILATED_PATTERN[0][0] // DILATED_PATTERN[0][1]
    bsz, seq, w = q.shape
    cols = w // LANES
    spec = pl.BlockSpec((None, seq, LANES), lambda b, c: (b, 0, c))
    cast_in, cast_out, cast_shapes = _cast_specs(cast, cast_layer, bsz * cols,
                                                 lambda b, c: b * cols + c)
    return pl.pallas_call(
        functools.partial(_dilated_kernel, seq=seq, max_dist=max_dist, n_cast=len(cast)),
        grid=(bsz, cols),
        in_specs=[spec, spec, spec] + cast_in,
        out_specs=[spec] + cast_out,
        out_shape=[jax.ShapeDtypeStruct((bsz, seq, w), BF16)] + cast_shapes,
        scratch_shapes=[pltpu.VMEM((seq, LANES), F32), pltpu.VMEM((seq, LANES), F32),
                        pltpu.VMEM((2, 3, seq, LANES), BF16),
                        pltpu.VMEM((3, seq, LANES), F32), pltpu.VMEM((3, seq, LANES), F32),
                        pltpu.VMEM((2, 2 * BLOCK, 2 * BLOCK), F32)],
        compiler_params=_params("arbitrary", "arbitrary"),
        name="dilated",
    )(q, k, v, *cast)


def _swa_kernel(*refs, nblk, max_dist, pairs, n_cast):
    sink_ref, q_ref, k_ref, v_ref = refs[:4]
    o_ref = refs[4 + n_cast]
    bias_ref = refs[5 + 2 * n_cast]
    _cast_slabs(refs[4:4 + n_cast], refs[5 + n_cast:5 + 2 * n_cast])
    head0 = pl.program_id(1) * (2 * pairs)
    _fill_band_bias(bias_ref, max_dist, 2 * pairs,
                    lambda rep: sink_ref[head0 + 2 * (rep % pairs) + rep // pairs])
    key_row = lax.broadcasted_iota(jnp.int32, (2 * BLOCK, LANES), 0)

    def body(i, carry):
        q0, ks, first = _band_window(i, nblk)
        qrow = q_ref[pl.ds(q0, BLOCK), :]
        qb = jnp.concatenate([qrow[:, p * LANES:(p + 1) * LANES] for p in range(pairs)], axis=0)
        unseen = key_row == _sink_column(first)
        zero = jnp.zeros((), k_ref.dtype)
        kb = jnp.where(unseen, zero, k_ref[pl.ds(ks, 2 * BLOCK), :])
        vb = jnp.where(unseen, zero, v_ref[pl.ds(ks, 2 * BLOCK), :])
        o, _ = _band_attend(qb, kb, vb, bias_ref[first], False)
        o_ref[pl.ds(q0, BLOCK), :] = jnp.concatenate(
            [o[p * BLOCK:(p + 1) * BLOCK, :] for p in range(pairs)], axis=1).astype(o_ref.dtype)
        return carry

    lax.fori_loop(0, nblk, body, 0, unroll=16)


def _swa_attention(q, kdup, vdup, sinks, cast, cast_layer):
    bsz, seq, w = q.shape
    pairs = w // LANES // C_KV_HEADS
    qspec = pl.BlockSpec((None, seq, pairs * LANES), lambda b, g: (b, 0, g))
    kvspec = pl.BlockSpec((None, seq, LANES), lambda b, g: (b, 0, g))
    cast_in, cast_out, cast_shapes = _cast_specs(cast, cast_layer, bsz * C_KV_HEADS,
                                                 lambda b, g: b * C_KV_HEADS + g)
    return pl.pallas_call(
        functools.partial(_swa_kernel, nblk=seq // BLOCK, max_dist=C_WINDOW - 1, pairs=pairs,
                          n_cast=len(cast)),
        grid=(bsz, C_KV_HEADS),
        in_specs=[pl.BlockSpec(memory_space=pltpu.SMEM), qspec, kvspec, kvspec] + cast_in,
        out_specs=[qspec] + cast_out,
        out_shape=[jax.ShapeDtypeStruct((bsz, seq, w), BF16)] + cast_shapes,
        scratch_shapes=[pltpu.VMEM((2, 2 * pairs * BLOCK, 2 * BLOCK), F32)],
        compiler_params=_params("arbitrary", "arbitrary"),
        name="swa_sink",
    )(sinks, q, kdup, vdup, *cast)


def _dup_heads(a):
    swapped = pltpu.roll(a, HEAD_DIM, 1)
    low = lax.broadcasted_iota(jnp.int32, a.shape, 1) < HEAD_DIM
    return jnp.concatenate([jnp.where(low, a, swapped), jnp.where(low, swapped, a)], axis=1)


def _swa_qkv(h, g_ref, w_ref, b_ref, rc_ref, rs1_ref, rs2_ref, q_ref, k_ref, v_ref):
    hn = _rms(h, g_ref[...]).astype(BF16)
    c, s1, s2 = rc_ref[...], rs1_ref[...], rs2_ref[...]
    ko, vo = C_Q_WIDTH, C_Q_WIDTH + C_KV_WIDTH
    q = _dot(hn, w_ref[:, 0:ko]) + b_ref[:, 0:ko]
    q_ref[...] = (_rope(q, c, s1, s2) * (HEAD_DIM ** -0.5)).astype(BF16)
    kv = _dot(hn, w_ref[:, ko:vo + C_KV_WIDTH]) + b_ref[:, ko:vo + C_KV_WIDTH]
    k_ref[...] = _dup_heads(_rope(kv[:, :C_KV_WIDTH], c, s1, s2)).astype(BF16)
    v_ref[...] = _dup_heads(kv[:, C_KV_WIDTH:]).astype(BF16)


def _mem_kv_kernel(mem_ref, g_ref, w_ref, k_ref, v_ref):
    mn = _rms(mem_ref[...], g_ref[...]).astype(BF16)
    k_ref[...] = _dot(mn, w_ref[:, 0:XA_WIDTH]).astype(BF16)
    v_ref[...] = _dot(mn, w_ref[:, XA_WIDTH:2 * XA_WIDTH]).astype(BF16)


def _mem_kv(mem, g, wkv):
    bsz, m, d = mem.shape
    layers = wkv.shape[0]
    spec = pl.BlockSpec((None, bsz * m, XA_WIDTH), lambda l: (l, 0, 0))
    per_layer = lambda a: pl.BlockSpec((None,) + a.shape[1:], lambda l: (l, 0, 0))
    return pl.pallas_call(
        _mem_kv_kernel,
        grid=(layers,),
        in_specs=[pl.BlockSpec((bsz * m, d), lambda l: (0, 0)), per_layer(g), per_layer(wkv)],
        out_specs=(spec, spec),
        out_shape=(jax.ShapeDtypeStruct((layers, bsz * m, XA_WIDTH), BF16),) * 2,
        compiler_params=_params("arbitrary"),
        name="mem_kv",
    )(mem.reshape(bsz * m, d), g, wkv)


def _layer_tail_kernel(*refs, n_act, d_ff, last):
    h_ref, act_refs = refs[0], refs[1:1 + n_act]
    (wout_ref, bout_ref, xg_ref, wq_ref, mk_ref, mv_ref, wo_ref,
     fg_ref, wgu_ref, wd_ref) = refs[1 + n_act:11 + n_act]
    epilogue = refs[11 + n_act:]

    h = h_ref[...] + bout_ref[...]
    row = 0
    for a_ref in act_refs:
        width = a_ref.shape[-1]
        h = h + _dot(a_ref[...], wout_ref[row:row + width, :])
        row += width

    q = _dot(_rms(h, xg_ref[...]).astype(BF16), wq_ref[...]).astype(BF16)
    scale = XA_HEAD_DIM ** -0.5
    heads = []
    for hd in range(XA_HEADS):
        sl = slice(hd * XA_HEAD_DIM, (hd + 1) * XA_HEAD_DIM)
        s = _dot_nt(q[:, sl], mk_ref[:, sl]) * scale
        e = jnp.exp(s - jnp.max(s, axis=-1, keepdims=True))
        den = jnp.sum(e, axis=-1, keepdims=True)
        heads.append((_dot(e.astype(BF16), mv_ref[:, sl]) * (1.0 / den)).astype(BF16))
    h = h + _dot(jnp.concatenate(heads, axis=1), wo_ref[...])

    hn = _rms(h, fg_ref[...]).astype(BF16)
    for c0, width in _ff_chunks(d_ff):
        gate = _dot(hn, wgu_ref[:, c0:c0 + width])
        up = _dot(hn, wgu_ref[:, d_ff + c0:d_ff + c0 + width])
        act = (gate * jax.nn.sigmoid(gate) * up).astype(BF16)
        h = h + _dot(act, wd_ref[c0:c0 + width, :])
    if last:
        ng_ref, out_ref = epilogue
        out_ref[...] = _rms(h, ng_ref[...])
    else:
        out_ref = epilogue[6]
        out_ref[...] = h
        _swa_qkv(h, *epilogue[:6], *epilogue[7:])


def _layer_tail(h, acts, w_out, b_out, mk, mv, p, ffn, layer, nxt=None, tm=TOK_TILE):
    bsz, seq, d = h.shape
    wgu, wd = ffn
    d_ff = wd.shape[1]
    assert sum(a.shape[-1] for a in acts) == w_out.shape[1]
    tok = lambda width: pl.BlockSpec((None, tm, width), lambda b, i: (b, i, 0))
    kv = pl.BlockSpec((None, N_MEM, XA_WIDTH), lambda b, i: (layer, b, 0))
    res = jax.ShapeDtypeStruct((bsz, seq, d), F32)
    if nxt is None:
        extra, extra_specs = (p["ng"],), [_const_spec(p["ng"].shape)]
        out_specs, out_shape = tok(d), res
    else:
        g2, w2, b2, tables = nxt
        extra = (g2, w2, b2, tables, tables, tables)
        extra_specs = [_layer_spec(a.shape, 0) for a in (g2, w2, b2)] + _table_specs(tm)
        widths = (C_Q_WIDTH, 2 * C_KV_WIDTH, 2 * C_KV_WIDTH)
        out_specs = (tok(d),) + tuple(tok(w) for w in widths)
        out_shape = (res,) + tuple(jax.ShapeDtypeStruct((bsz, seq, w), BF16) for w in widths)
    return pl.pallas_call(
        functools.partial(_layer_tail_kernel, n_act=len(acts), d_ff=d_ff, last=nxt is None),
        grid=(bsz, seq // tm),
        in_specs=[tok(d)] + [tok(a.shape[-1]) for a in acts] + [
            _layer_spec(w_out.shape, 0), _layer_spec(b_out.shape, 0),
            _layer_spec(p["xg"].shape, layer), _layer_spec(p["wq"].shape, layer), kv, kv,
            _layer_spec(p["wo"].shape, layer), _layer_spec(p["fg"].shape, layer),
            _layer_spec(wgu.shape, 0), _layer_spec(wd.shape, 0)] + extra_specs,
        out_specs=out_specs,
        out_shape=out_shape,
        compiler_params=_params("arbitrary", "arbitrary"),
        name="layer_tail",
    )(h, *acts, w_out, b_out, p["xg"], p["wq"], mk, mv, p["wo"], p["fg"], wgu, wd,
      *extra)


def kernel(x, mem, mix_norm, ab_w_in, lru_conv_w, lru_conv_b, lru_wa, lru_ba, lru_wx, lru_bx, lru_lambda, ab_w_out, c_w_qkv, c_b_qkv, c_sinks, c_w_out, c_b_out, xa_norm, xa_mem_norm, xa_wq, xa_wkv, xa_wo, ffn_norm, ffn_w_gate_up, ffn_w_down, final_norm):
    bsz, seq, d = x.shape
    tables = _rope_tables(seq)
    rows = lambda a: a.reshape(a.shape[0], 1, -1)
    bf = lambda a: a.astype(BF16)
    mix_g = rows(mix_norm)
    p = dict(xg=rows(xa_norm), wq=bf(xa_wq), wo=bf(xa_wo), fg=rows(ffn_norm),
             ng=final_norm.reshape(1, d))
    ffn_f32 = (ffn_w_gate_up, ffn_w_down)
    mk, mv = _mem_kv(mem, rows(xa_mem_norm), bf(xa_wkv))

    rec, q, k, v = _lru_proj(x, mix_g, bf(ab_w_in), lru_conv_w, rows(lru_conv_b), bf(lru_wa),
                             rows(lru_ba), bf(lru_wx), rows(lru_bx), rows(lru_lambda), tables)
    att, *ffn = _dilated_attention(q, k, v, ffn_f32, 0)
    h, q, k, v = _layer_tail(x, (rec, att), bf(ab_w_out), jnp.zeros((1, 1, d), F32), mk, mv, p,
                             ffn, layer=0,
                             nxt=(mix_g[1:2], bf(c_w_qkv), rows(c_b_qkv), tables))

    att, *ffn = _swa_attention(q, k, v, c_sinks[0], ffn_f32, 1)
    return _layer_tail(h, (att,), bf(c_w_out), rows(c_b_out), mk, mv, p, ffn, layer=1,
                       tm=LAST_TAIL_TILE)
```

```python
import functools
import math

import jax
import jax.numpy as jnp
from jax import lax
from jax.experimental import pallas as pl
from jax.experimental.pallas import tpu as pltpu

F32 = jnp.float32
BF16 = jnp.bfloat16

D_MODEL = 1024
N_MEM = 256
NORM_EPS = 1e-6
ROPE_THETA = 500000.0
HEAD_DIM = 64
ROT_DIM = HEAD_DIM // 4
ROT_HALF = ROT_DIM // 2
LANES = 128
SUBLANES = 8
BLOCK = 128
LRU_WIDTH = D_MODEL
LRU_HEADS = 4
LRU_HEAD_DIM = LRU_WIDTH // LRU_HEADS
CONV_WIDTH = 4
LRU_C = 8.0
B_HEADS = 8
B_WIDTH = B_HEADS * HEAD_DIM
DILATED_PATTERN = ((128, 1), (512, 4), (2048, 16))
C_HEADS = 16
C_KV_HEADS = 2
C_WINDOW = 128
C_Q_WIDTH = C_HEADS * HEAD_DIM
C_KV_WIDTH = C_KV_HEADS * HEAD_DIM
XA_HEADS = 4
XA_HEAD_DIM = 128
XA_WIDTH = XA_HEADS * XA_HEAD_DIM
NEG = -1e30
VMEM_LIMIT = 56 * 1024 * 1024

TOK_TILE = 512
MXU_TILE = 256
FF_CHUNK_TILES = 3
LRU_TILE = 1024
LAST_TAIL_TILE = 1024


def _ff_chunks(d_ff):
    assert d_ff % MXU_TILE == 0
    step = FF_CHUNK_TILES * MXU_TILE
    return [(c0, min(step, d_ff - c0)) for c0 in range(0, d_ff, step)]


def _dot(a, b):
    return jnp.dot(a, b, preferred_element_type=F32)


def _dot_nt(a, b):
    return lax.dot_general(a, b, (((1,), (1,)), ((), ())), preferred_element_type=F32)


def _rms(x, g):
    return x * lax.rsqrt(jnp.mean(x * x, axis=-1, keepdims=True) + NORM_EPS) * g


def _params(*sem):
    return pltpu.CompilerParams(dimension_semantics=sem, vmem_limit_bytes=VMEM_LIMIT)


def _const_spec(shape):
    nd = len(shape)
    return pl.BlockSpec(shape, lambda *_: (0,) * nd, pipeline_mode=pl.Buffered(1))


def _layer_spec(shape, layer):
    nd = len(shape)
    return pl.BlockSpec((None,) + tuple(shape[1:]), lambda *_: (layer,) + (0,) * (nd - 1),
                        pipeline_mode=pl.Buffered(1))


def _rope(x, c, s1, s2):
    cols = []
    for j in range(x.shape[1] // LANES):
        xc = x[:, j * LANES:(j + 1) * LANES]
        cols.append(xc * c + pltpu.roll(xc, ROT_HALF, 1) * s1
                    + pltpu.roll(xc, LANES - ROT_HALF, 1) * s2)
    return cols[0] if len(cols) == 1 else jnp.concatenate(cols, axis=1)


def _rope_tables(seq):
    lane = jnp.arange(LANES) % HEAD_DIM
    inv = ROPE_THETA ** (-(2 * (lane % ROT_HALF)).astype(F32) / ROT_DIM)
    ang = jnp.arange(seq, dtype=F32)[:, None] * inv[None, :]
    cos, sin = jnp.cos(ang), jnp.sin(ang)
    c = jnp.where(lane < ROT_DIM, cos, 1.0)
    s1 = jnp.where(jnp.logical_and(lane >= ROT_HALF, lane < ROT_DIM), sin, 0.0)
    s2 = jnp.where(lane < ROT_HALF, -sin, 0.0)
    return jnp.stack([c, s1, s2])


def _table_specs(tm):
    return [pl.BlockSpec((None, tm, LANES), lambda b, i, j=j: (j, i, 0)) for j in range(3)]


def _sigmoid(x):
    return 0.5 * jnp.tanh(0.5 * x) + 0.5


SEG_PAD = SUBLANES


def _to_segments(x, pad_ref):
    tl, w = x.shape
    seg = tl // SUBLANES
    pitch = seg + SEG_PAD
    for c in range(w // LANES):
        for s in range(SUBLANES):
            pad_ref[c, s * pitch:s * pitch + seg, :] = x[s * seg:(s + 1) * seg,
                                                         c * LANES:(c + 1) * LANES]
    groups = [jnp.concatenate([pad_ref[c, pl.ds(j, SUBLANES, stride=pitch), :]
                               for c in range(w // LANES)], axis=1) for j in range(seg)]
    return jnp.concatenate(groups, axis=0)


def _shift_segments(xs, k, prev):
    last = xs[xs.shape[0] - k * SUBLANES:, :]
    sub = lax.broadcasted_iota(jnp.int32, last.shape, 0) % SUBLANES
    head = jnp.where(sub == 0, _roll_groups(prev, k), _roll_groups(last, k))
    return jnp.concatenate([head, xs[:xs.shape[0] - k * SUBLANES, :]], axis=0)


def _roll_groups(x, k):
    return jnp.concatenate([pltpu.roll(x[g * SUBLANES:(g + 1) * SUBLANES, :], 1, 0)
                            for g in range(k)], axis=0)


def _lru_proj_kernel(h_ref, g_ref, w_ref, cw_ref, cb_ref, wa_ref, ba_ref, wx_ref, bx_ref,
                     lam_ref, rc_ref, rs1_ref, rs2_ref,
                     rec_ref, q_ref, k_ref, v_ref,
                     hn_ref, tail_ref, hprev_ref, xpad_ref, hpad_ref, loc_ref, prod_ref, *, tl):
    seg = tl // SUBLANES
    pitch = seg + SEG_PAD

    @pl.when(pl.program_id(1) == 0)
    def _():
        tail_ref[...] = jnp.zeros_like(tail_ref)
        hprev_ref[...] = jnp.zeros_like(hprev_ref)

    hn_ref[...] = _rms(h_ref[...], g_ref[...]).astype(BF16)
    hn = hn_ref[...]
    c, s1, s2 = rc_ref[...], rs1_ref[...], rs2_ref[...]
    attn_out = (q_ref, k_ref, v_ref)

    group_cols = lambda hd: slice(hd * LRU_HEAD_DIM, (hd + 1) * LRU_HEAD_DIM)
    half = B_WIDTH // 2

    def attn_piece(hd, j):
        o = 2 * LRU_WIDTH + hd * B_WIDTH + j * half
        t = _dot(hn, w_ref[:, o:o + half])
        if hd < 2:
            t = _rope(t, c, s1, s2)
        if hd == 0:
            t = t * (HEAD_DIM ** -0.5)
        attn_out[hd][:, j * half:(j + 1) * half] = t.astype(BF16)

    x_next = _dot(hn, w_ref[:, group_cols(0)])
    for hd in range(LRU_HEADS):
        cols = group_cols(hd)
        x = x_next
        if hd + 1 < LRU_HEADS:
            x_next = _dot(hn, w_ref[:, group_cols(hd + 1)])

        x = _to_segments(x, xpad_ref)
        tail = tail_ref[:, cols]
        taps = CONV_WIDTH - 1
        xc = cb_ref[:, cols] + x * cw_ref[taps:taps + 1, cols]
        for k in range(1, CONV_WIDTH):
            prev = tail[(taps - k) * SUBLANES:, :]
            xc = xc + _shift_segments(x, k, prev) * cw_ref[taps - k:taps - k + 1, cols]
        tail_ref[:, cols] = x[tl - taps * SUBLANES:tl, :]

        y = _dot(hn, w_ref[:, LRU_WIDTH + hd * LRU_HEAD_DIM:LRU_WIDTH + (hd + 1) * LRU_HEAD_DIM])

        xcb = xc.astype(BF16)
        r = _sigmoid(_dot(xcb, wa_ref[hd]) + ba_ref[:, cols])
        ig = _sigmoid(_dot(xcb, wx_ref[hd]) + bx_ref[:, cols])
        nlam = -lam_ref[:, cols]
        softplus = jnp.maximum(nlam, 0.0) + jnp.log1p(jnp.exp(-jnp.abs(nlam)))
        neg_log_a = r * (LRU_C * softplus)
        a = jnp.exp2(r * ((-LRU_C * math.log2(math.e)) * softplus))
        u = jnp.sqrt(jnp.tanh(neg_log_a) * (1.0 + a * a)) * (ig * xc)

        if hd < len(attn_out):
            attn_piece(hd, 0)

        hloc = jnp.zeros((SUBLANES, LRU_HEAD_DIM), F32)
        prod = jnp.ones((SUBLANES, LRU_HEAD_DIM), F32)
        for j in range(seg):
            rows = slice(j * SUBLANES, (j + 1) * SUBLANES)
            hloc = a[rows, :] * hloc + u[rows, :]
            prod = a[rows, :] * prod
            loc_ref[rows, :] = hloc
            prod_ref[rows, :] = prod

        state = hprev_ref[0:1, cols]
        entering = []
        for s in range(SUBLANES):
            entering.append(state)
            state = prod[s:s + 1, :] * state + hloc[s:s + 1, :]
        hprev_ref[:, cols] = jnp.broadcast_to(state, (SUBLANES, LRU_HEAD_DIM))
        entering = jnp.concatenate(entering, axis=0)

        for j in range(seg):
            rows = slice(j * SUBLANES, (j + 1) * SUBLANES)
            hrow = loc_ref[rows, :] + prod_ref[rows, :] * entering
            for cidx in range(LRU_HEAD_DIM // LANES):
                hpad_ref[cidx, pl.ds(j, SUBLANES, stride=pitch), :] = hrow[:, cidx * LANES:
                                                                           (cidx + 1) * LANES]

        if hd < len(attn_out):
            attn_piece(hd, 1)

        gelu = 0.5 * y * (1.0 + jnp.tanh(math.sqrt(2.0 / math.pi) * (y + 0.044715 * (y * y * y))))
        for s in range(SUBLANES):
            rows = slice(s * seg, (s + 1) * seg)
            hs = jnp.concatenate([hpad_ref[cidx, s * pitch:s * pitch + seg, :]
                                  for cidx in range(LRU_HEAD_DIM // LANES)], axis=1)
            rec_ref[rows, cols] = (hs * gelu[rows, :]).astype(BF16)


def _lru_proj(h, g, w_in, conv_w, conv_b, wa, ba, wx, bx, lam, tables, tl=LRU_TILE):
    bsz, seq, d = h.shape
    tok = lambda w: pl.BlockSpec((None, tl, w), lambda b, i: (b, i, 0))
    out_shape = (jax.ShapeDtypeStruct((bsz, seq, LRU_WIDTH), BF16),) + tuple(
        jax.ShapeDtypeStruct((bsz, seq, B_WIDTH), BF16) for _ in range(3))
    params = (g, w_in, conv_w, conv_b, wa, ba, wx, bx, lam)
    return pl.pallas_call(
        functools.partial(_lru_proj_kernel, tl=tl),
        grid=(bsz, seq // tl),
        in_specs=[tok(d)] + [_layer_spec(a.shape, 0) for a in params] + _table_specs(tl),
        out_specs=(tok(LRU_WIDTH), tok(B_WIDTH), tok(B_WIDTH), tok(B_WIDTH)),
        out_shape=out_shape,
        scratch_shapes=[pltpu.VMEM((tl, d), BF16),
                        pltpu.VMEM(((CONV_WIDTH - 1) * SUBLANES, LRU_WIDTH), F32),
                        pltpu.VMEM((SUBLANES, LRU_WIDTH), F32),
                        pltpu.VMEM((LRU_HEAD_DIM // LANES, tl + SUBLANES * SEG_PAD, LANES), F32),
                        pltpu.VMEM((LRU_HEAD_DIM // LANES, tl + SUBLANES * SEG_PAD, LANES), F32),
                        pltpu.VMEM((tl, LRU_HEAD_DIM), F32), pltpu.VMEM((tl, LRU_HEAD_DIM), F32)],
        compiler_params=_params("arbitrary", "arbitrary"),
        name="lru_proj",
    )(h, *params, tables, tables, tables)


def _sink_column(first):
    return (2 * BLOCK - 1) * (1 - first)


def _fill_band_bias(bias_ref, max_dist, reps, sink_of_rep=None):
    rows = lax.broadcasted_iota(jnp.int32, (BLOCK, 2 * BLOCK), 0)
    cols = lax.broadcasted_iota(jnp.int32, (BLOCK, 2 * BLOCK), 1)
    for first, off in ((0, 0), (1, BLOCK)):
        dist = off + rows - cols
        valid = jnp.logical_and(dist >= 0, dist <= max_dist)
        tile = jnp.where(valid, 0.0, NEG).astype(F32)
        for rep in range(reps):
            rep_tile = tile
            if sink_of_rep is not None:
                assert max_dist < BLOCK
                rep_tile = jnp.where(cols == _sink_column(first), sink_of_rep(rep), tile)
            bias_ref[first, rep * BLOCK:(rep + 1) * BLOCK, :] = rep_tile


def _band_attend(qb, kb, vb, bias, want_lse):
    r = qb.shape[0]
    lane_q = lax.broadcasted_iota(jnp.int32, qb.shape, 1)
    zero = jnp.zeros((), qb.dtype)
    qz = jnp.concatenate([jnp.where(lane_q < HEAD_DIM, qb, zero),
                          jnp.where(lane_q >= HEAD_DIM, qb, zero)], axis=0)
    s = _dot_nt(qz, kb) + bias
    m = jnp.max(s, axis=-1, keepdims=True)
    e = jnp.exp(s - m).astype(vb.dtype)
    ob = _dot(e, jnp.concatenate([vb, jnp.ones_like(vb)], axis=1))
    first_head = lax.broadcasted_iota(jnp.int32, (r, LANES), 1) < HEAD_DIM
    num = jnp.where(first_head, ob[:r, :LANES], ob[r:, :LANES])
    den = jnp.where(first_head, ob[:r, LANES:], ob[r:, LANES:])
    o = num * (1.0 / den)
    if not want_lse:
        return o, None
    return o, jnp.where(first_head, m[:r], m[r:]) + jnp.log(den)


def _band_window(n, nblk):
    first = jnp.minimum(lax.rem(n, nblk), 1)
    q0 = pl.multiple_of(n * BLOCK, BLOCK)
    ks = pl.multiple_of(q0 - first * BLOCK, BLOCK)
    return q0, ks, first


def _cast_specs(ws, layer, steps, step_of):
    in_specs, out_specs, out_shapes = [], [], []
    for w in ws:
        assert w.shape[1] % (steps * 2 * SUBLANES) == 0
        blk = (None, w.shape[1] // steps, w.shape[2])
        in_specs.append(pl.BlockSpec(blk, lambda *ids: (layer, step_of(*ids), 0)))
        out_specs.append(pl.BlockSpec(blk, lambda *ids: (0, step_of(*ids), 0)))
        out_shapes.append(jax.ShapeDtypeStruct((1,) + tuple(w.shape[1:]), BF16))
    return in_specs, out_specs, out_shapes


def _cast_slabs(src_refs, dst_refs):
    for src, dst in zip(src_refs, dst_refs):
        dst[...] = src[...].astype(dst.dtype)


def _dilated_kernel(*refs, seq, max_dist, n_cast):
    q_ref, k_ref, v_ref = refs[:3]
    out_ref = refs[3 + n_cast]
    stage_ref, p4_ref, perm_ref, o_ref, l_ref, bias_ref = refs[4 + 2 * n_cast:]
    _cast_slabs(refs[3:3 + n_cast], refs[4 + n_cast:4 + 2 * n_cast])
    _fill_band_bias(bias_ref, max_dist, 2)

    l4, l16 = seq // 4, seq // 16
    for a, x_ref in enumerate((q_ref, k_ref, v_ref)):
        stage_ref[...] = x_ref[...].astype(F32)
        for r in range(4):
            blk = stage_ref[pl.ds(r, l4, stride=4), :]
            p4_ref[r * l4:(r + 1) * l4, :] = blk
            perm_ref[0, a, r * l4:(r + 1) * l4, :] = blk.astype(BF16)
        for r in range(4):
            for rb in range(4):
                blk = p4_ref[pl.ds(r * l4 + rb, l16, stride=4), :]
                res = r + 4 * rb
                perm_ref[1, a, res * l16:(res + 1) * l16, :] = blk.astype(BF16)

    branches = ((1, (q_ref, k_ref, v_ref)),
                (4, tuple(perm_ref.at[0, a] for a in range(3))),
                (16, tuple(perm_ref.at[1, a] for a in range(3))))
    for bi, (d, (qr, kr, vr)) in enumerate(branches):
        nblk = seq // d // BLOCK

        def body(n, carry, d=d, bi=bi, qr=qr, kr=kr, vr=vr, nblk=nblk):
            q0, ks, first = _band_window(n, nblk)
            o, lse = _band_attend(qr[pl.ds(q0, BLOCK), :], kr[pl.ds(ks, 2 * BLOCK), :],
                                  vr[pl.ds(ks, 2 * BLOCK), :], bias_ref[first], True)
            if d == 1:
                rows = pl.ds(q0, BLOCK)
            else:
                rows = pl.ds(lax.rem(n, nblk) * (BLOCK * d) + n // nblk, BLOCK, stride=d)
            o_ref.at[bi][rows, :] = o
            l_ref.at[bi][rows, :] = lse
            return carry

        lax.fori_loop(0, seq // BLOCK, body, 0, unroll=32)

    chunk = 4 * BLOCK

    def merge(c, carry):
        rows = pl.ds(pl.multiple_of(c * chunk, chunk), chunk)
        ls = [l_ref[bi, rows, :] for bi in range(3)]
        m = jnp.maximum(jnp.maximum(ls[0], ls[1]), ls[2])
        ws = [jnp.exp(l - m) for l in ls]
        acc = ws[0] * o_ref[0, rows, :] + ws[1] * o_ref[1, rows, :] + ws[2] * o_ref[2, rows, :]
        out_ref[rows, :] = (acc * (1.0 / (ws[0] + ws[1] + ws[2]))).astype(out_ref.dtype)
        return carry

    lax.fori_loop(0, seq // chunk, merge, 0)


def _dilated_attention(q, k, v, cast, cast_layer):
    assert tuple(d for _, d in DILATED_PATTERN) == (1, 4, 16)
    assert len({w // d for w, d in DILATED_PATTERN}) == 1
    max_dist = DILATED_PATTERN[0][0] // DILATED_PATTERN[0][1]
    bsz, seq, w = q.shape
    cols = w // LANES
    spec = pl.BlockSpec((None, seq, LANES), lambda b, c: (b, 0, c))
    cast_in, cast_out, cast_shapes = _cast_specs(cast, cast_layer, bsz * cols,
                                                 lambda b, c: b * cols + c)
    return pl.pallas_call(
        functools.partial(_dilated_kernel, seq=seq, max_dist=max_dist, n_cast=len(cast)),
        grid=(bsz, cols),
        in_specs=[spec, spec, spec] + cast_in,
        out_specs=[spec] + cast_out,
        out_shape=[jax.ShapeDtypeStruct((bsz, seq, w), BF16)] + cast_shapes,
        scratch_shapes=[pltpu.VMEM((seq, LANES), F32), pltpu.VMEM((seq, LANES), F32),
                        pltpu.VMEM((2, 3, seq, LANES), BF16),
                        pltpu.VMEM((3, seq, LANES), F32), pltpu.VMEM((3, seq, LANES), F32),
                        pltpu.VMEM((2, 2 * BLOCK, 2 * BLOCK), F32)],
        compiler_params=_params("arbitrary", "arbitrary"),
        name="dilated",
    )(q, k, v, *cast)


def _swa_kernel(*refs, nblk, max_dist, pairs, n_cast):
    sink_ref, q_ref, k_ref, v_ref = refs[:4]
    o_ref = refs[4 + n_cast]
    bias_ref = refs[5 + 2 * n_cast]
    _cast_slabs(refs[4:4 + n_cast], refs[5 + n_cast:5 + 2 * n_cast])
    head0 = pl.program_id(1) * (2 * pairs)
    _fill_band_bias(bias_ref, max_dist, 2 * pairs,
                    lambda rep: sink_ref[head0 + 2 * (rep % pairs) + rep // pairs])
    key_row = lax.broadcasted_iota(jnp.int32, (2 * BLOCK, LANES), 0)

    def body(i, carry):
        q0, ks, first = _band_window(i, nblk)
        qrow = q_ref[pl.ds(q0, BLOCK), :]
        qb = jnp.concatenate([qrow[:, p * LANES:(p + 1) * LANES] for p in range(pairs)], axis=0)
        unseen = key_row == _sink_column(first)
        zero = jnp.zeros((), k_ref.dtype)
        kb = jnp.where(unseen, zero, k_ref[pl.ds(ks, 2 * BLOCK), :])
        vb = jnp.where(unseen, zero, v_ref[pl.ds(ks, 2 * BLOCK), :])
        o, _ = _band_attend(qb, kb, vb, bias_ref[first], False)
        o_ref[pl.ds(q0, BLOCK), :] = jnp.concatenate(
            [o[p * BLOCK:(p + 1) * BLOCK, :] for p in range(pairs)], axis=1).astype(o_ref.dtype)
        return carry

    lax.fori_loop(0, nblk, body, 0, unroll=16)


def _swa_attention(q, kdup, vdup, sinks, cast, cast_layer):
    bsz, seq, w = q.shape
    pairs = w // LANES // C_KV_HEADS
    qspec = pl.BlockSpec((None, seq, pairs * LANES), lambda b, g: (b, 0, g))
    kvspec = pl.BlockSpec((None, seq, LANES), lambda b, g: (b, 0, g))
    cast_in, cast_out, cast_shapes = _cast_specs(cast, cast_layer, bsz * C_KV_HEADS,
                                                 lambda b, g: b * C_KV_HEADS + g)
    return pl.pallas_call(
        functools.partial(_swa_kernel, nblk=seq // BLOCK, max_dist=C_WINDOW - 1, pairs=pairs,
                          n_cast=len(cast)),
        grid=(bsz, C_KV_HEADS),
        in_specs=[pl.BlockSpec(memory_space=pltpu.SMEM), qspec, kvspec, kvspec] + cast_in,
        out_specs=[qspec] + cast_out,
        out_shape=[jax.ShapeDtypeStruct((bsz, seq, w), BF16)] + cast_shapes,
        scratch_shapes=[pltpu.VMEM((2, 2 * pairs * BLOCK, 2 * BLOCK), F32)],
        compiler_params=_params("arbitrary", "arbitrary"),
        name="swa_sink",
    )(sinks, q, kdup, vdup, *cast)


def _dup_heads(a):
    swapped = pltpu.roll(a, HEAD_DIM, 1)
    low = lax.broadcasted_iota(jnp.int32, a.shape, 1) < HEAD_DIM
    return jnp.concatenate([jnp.where(low, a, swapped), jnp.where(low, swapped, a)], axis=1)


def _swa_qkv(h, g_ref, w_ref, b_ref, rc_ref, rs1_ref, rs2_ref, q_ref, k_ref, v_ref):
    hn = _rms(h, g_ref[...]).astype(BF16)
    c, s1, s2 = rc_ref[...], rs1_ref[...], rs2_ref[...]
    ko, vo = C_Q_WIDTH, C_Q_WIDTH + C_KV_WIDTH
    q = _dot(hn, w_ref[:, 0:ko]) + b_ref[:, 0:ko]
    q_ref[...] = (_rope(q, c, s1, s2) * (HEAD_DIM ** -0.5)).astype(BF16)
    kv = _dot(hn, w_ref[:, ko:vo + C_KV_WIDTH]) + b_ref[:, ko:vo + C_KV_WIDTH]
    k_ref[...] = _dup_heads(_rope(kv[:, :C_KV_WIDTH], c, s1, s2)).astype(BF16)
    v_ref[...] = _dup_heads(kv[:, C_KV_WIDTH:]).astype(BF16)


def _mem_kv_kernel(mem_ref, g_ref, w_ref, k_ref, v_ref):
    mn = _rms(mem_ref[...], g_ref[...]).astype(BF16)
    k_ref[...] = _dot(mn, w_ref[:, 0:XA_WIDTH]).astype(BF16)
    v_ref[...] = _dot(mn, w_ref[:, XA_WIDTH:2 * XA_WIDTH]).astype(BF16)


def _mem_kv(mem, g, wkv):
    bsz, m, d = mem.shape
    layers = wkv.shape[0]
    spec = pl.BlockSpec((None, bsz * m, XA_WIDTH), lambda l: (l, 0, 0))
    per_layer = lambda a: pl.BlockSpec((None,) + a.shape[1:], lambda l: (l, 0, 0))
    return pl.pallas_call(
        _mem_kv_kernel,
        grid=(layers,),
        in_specs=[pl.BlockSpec((bsz * m, d), lambda l: (0, 0)), per_layer(g), per_layer(wkv)],
        out_specs=(spec, spec),
        out_shape=(jax.ShapeDtypeStruct((layers, bsz * m, XA_WIDTH), BF16),) * 2,
        compiler_params=_params("arbitrary"),
        name="mem_kv",
    )(mem.reshape(bsz * m, d), g, wkv)


def _layer_tail_kernel(*refs, n_act, d_ff, last):
    h_ref, act_refs = refs[0], refs[1:1 + n_act]
    (wout_ref, bout_ref, xg_ref, wq_ref, mk_ref, mv_ref, wo_ref,
     fg_ref, wgu_ref, wd_ref) = refs[1 + n_act:11 + n_act]
    epilogue = refs[11 + n_act:]

    h = h_ref[...] + bout_ref[...]
    row = 0
    for a_ref in act_refs:
        width = a_ref.shape[-1]
        h = h + _dot(a_ref[...], wout_ref[row:row + width, :])
        row += width

    q = _dot(_rms(h, xg_ref[...]).astype(BF16), wq_ref[...]).astype(BF16)
    scale = XA_HEAD_DIM ** -0.5
    heads = []
    for hd in range(XA_HEADS):
        sl = slice(hd * XA_HEAD_DIM, (hd + 1) * XA_HEAD_DIM)
        s = _dot_nt(q[:, sl], mk_ref[:, sl]) * scale
        e = jnp.exp(s - jnp.max(s, axis=-1, keepdims=True))
        den = jnp.sum(e, axis=-1, keepdims=True)
        heads.append((_dot(e.astype(BF16), mv_ref[:, sl]) * (1.0 / den)).astype(BF16))
    h = h + _dot(jnp.concatenate(heads, axis=1), wo_ref[...])

    hn = _rms(h, fg_ref[...]).astype(BF16)
    for c0, width in _ff_chunks(d_ff):
        gate = _dot(hn, wgu_ref[:, c0:c0 + width])
        up = _dot(hn, wgu_ref[:, d_ff + c0:d_ff + c0 + width])
        act = (gate * jax.nn.sigmoid(gate) * up).astype(BF16)
        h = h + _dot(act, wd_ref[c0:c0 + width, :])
    if last:
        ng_ref, out_ref = epilogue
        out_ref[...] = _rms(h, ng_ref[...])
    else:
        out_ref = epilogue[6]
        out_ref[...] = h
        _swa_qkv(h, *epilogue[:6], *epilogue[7:])


def _layer_tail(h, acts, w_out, b_out, mk, mv, p, ffn, layer, nxt=None, tm=TOK_TILE):
    bsz, seq, d = h.shape
    wgu, wd = ffn
    d_ff = wd.shape[1]
    assert sum(a.shape[-1] for a in acts) == w_out.shape[1]
    tok = lambda width: pl.BlockSpec((None, tm, width), lambda b, i: (b, i, 0))
    kv = pl.BlockSpec((None, N_MEM, XA_WIDTH), lambda b, i: (layer, b, 0))
    res = jax.ShapeDtypeStruct((bsz, seq, d), F32)
    if nxt is None:
        extra, extra_specs = (p["ng"],), [_const_spec(p["ng"].shape)]
        out_specs, out_shape = tok(d), res
    else:
        g2, w2, b2, tables = nxt
        extra = (g2, w2, b2, tables, tables, tables)
        extra_specs = [_layer_spec(a.shape, 0) for a in (g2, w2, b2)] + _table_specs(tm)
        widths = (C_Q_WIDTH, 2 * C_KV_WIDTH, 2 * C_KV_WIDTH)
        out_specs = (tok(d),) + tuple(tok(w) for w in widths)
        out_shape = (res,) + tuple(jax.ShapeDtypeStruct((bsz, seq, w), BF16) for w in widths)
    return pl.pallas_call(
        functools.partial(_layer_tail_kernel, n_act=len(acts), d_ff=d_ff, last=nxt is None),
        grid=(bsz, seq // tm),
        in_specs=[tok(d)] + [tok(a.shape[-1]) for a in acts] + [
            _layer_spec(w_out.shape, 0), _layer_spec(b_out.shape, 0),
            _layer_spec(p["xg"].shape, layer), _layer_spec(p["wq"].shape, layer), kv, kv,
            _layer_spec(p["wo"].shape, layer), _layer_spec(p["fg"].shape, layer),
            _layer_spec(wgu.shape, 0), _layer_spec(wd.shape, 0)] + extra_specs,
        out_specs=out_specs,
        out_shape=out_shape,
        compiler_params=_params("arbitrary", "arbitrary"),
        name="layer_tail",
    )(h, *acts, w_out, b_out, p["xg"], p["wq"], mk, mv, p["wo"], p["fg"], wgu, wd,
      *extra)


def kernel(x, mem, mix_norm, ab_w_in, lru_conv_w, lru_conv_b, lru_wa, lru_ba, lru_wx, lru_bx, lru_lambda, ab_w_out, c_w_qkv, c_b_qkv, c_sinks, c_w_out, c_b_out, xa_norm, xa_mem_norm, xa_wq, xa_wkv, xa_wo, ffn_norm, ffn_w_gate_up, ffn_w_down, final_norm):
    bsz, seq, d = x.shape
    tables = _rope_tables(seq)
    rows = lambda a: a.reshape(a.shape[0], 1, -1)
    bf = lambda a: a.astype(BF16)
    mix_g = rows(mix_norm)
    p = dict(xg=rows(xa_norm), wq=bf(xa_wq), wo=bf(xa_wo), fg=rows(ffn_norm),
             ng=final_norm.reshape(1, d))
    ffn_f32 = (ffn_w_gate_up, ffn_w_down)
    mk, mv = _mem_kv(mem, rows(xa_mem_norm), bf(xa_wkv))

    rec, q, k, v = _lru_proj(x, mix_g, bf(ab_w_in), lru_conv_w, rows(lru_conv_b), bf(lru_wa),
                             rows(lru_ba), bf(lru_wx), rows(lru_bx), rows(lru_lambda), tables)
    att, *ffn = _dilated_attention(q, k, v, ffn_f32, 0)
    h, q, k, v = _layer_tail(x, (rec, att), bf(ab_w_out), jnp.zeros((1, 1, d), F32), mk, mv, p,
                             ffn, layer=0,
                             nxt=(mix_g[1:2], bf(c_w_qkv), rows(c_b_qkv), tables))

    att, *ffn = _swa_attention(q, k, v, c_sinks[0], ffn_f32, 1)
    return _layer_tail(h, (att,), bf(c_w_out), rows(c_b_out), mk, mv, p, ffn, layer=1,
                       tm=LAST_TAIL_TILE)
```

```python
import functools
import math

import jax
import jax.numpy as jnp
from jax import lax
from jax.experimental import pallas as pl
from jax.experimental.pallas import tpu as pltpu

F32 = jnp.float32
BF16 = jnp.bfloat16

D_MODEL = 1024
N_MEM = 256
NORM_EPS = 1e-6
ROPE_THETA = 500000.0
HEAD_DIM = 64
ROT_DIM = HEAD_DIM // 4
ROT_HALF = ROT_DIM // 2
LANES = 128
SUBLANES = 8
BLOCK = 128
LRU_WIDTH = D_MODEL
LRU_HEADS = 4
LRU_HEAD_DIM = LRU_WIDTH // LRU_HEADS
CONV_WIDTH = 4
LRU_C = 8.0
B_HEADS = 8
B_WIDTH = B_HEADS * HEAD_DIM
DILATED_PATTERN = ((128, 1), (512, 4), (2048, 16))
C_HEADS = 16
C_KV_HEADS = 2
C_WINDOW = 128
C_Q_WIDTH = C_HEADS * HEAD_DIM
C_KV_WIDTH = C_KV_HEADS * HEAD_DIM
XA_HEADS = 4
XA_HEAD_DIM = 128
XA_WIDTH = XA_HEADS * XA_HEAD_DIM
NEG = -1e30
VMEM_LIMIT = 56 * 1024 * 1024

TOK_TILE = 512
MXU_TILE = 256
FF_CHUNK_TILES = 3
LRU_TILE = 1024
LAST_TAIL_TILE = 1024


def _ff_chunks(d_ff):
    assert d_ff % MXU_TILE == 0
    step = FF_CHUNK_TILES * MXU_TILE
    return [(c0, min(step, d_ff - c0)) for c0 in range(0, d_ff, step)]


def _dot(a, b):
    return jnp.dot(a, b, preferred_element_type=F32)


def _dot_nt(a, b):
    return lax.dot_general(a, b, (((1,), (1,)), ((), ())), preferred_element_type=F32)


def _rms(x, g):
    return x * lax.rsqrt(jnp.mean(x * x, axis=-1, keepdims=True) + NORM_EPS) * g


def _params(*sem):
    return pltpu.CompilerParams(dimension_semantics=sem, vmem_limit_bytes=VMEM_LIMIT)


def _const_spec(shape):
    nd = len(shape)
    return pl.BlockSpec(shape, lambda *_: (0,) * nd, pipeline_mode=pl.Buffered(1))


def _layer_spec(shape, layer):
    nd = len(shape)
    return pl.BlockSpec((None,) + tuple(shape[1:]), lambda *_: (layer,) + (0,) * (nd - 1),
                        pipeline_mode=pl.Buffered(1))


def _rope(x, c, s1, s2):
    cols = []
    for j in range(x.shape[1] // LANES):
        xc = x[:, j * LANES:(j + 1) * LANES]
        cols.append(xc * c + pltpu.roll(xc, ROT_HALF, 1) * s1
                    + pltpu.roll(xc, LANES - ROT_HALF, 1) * s2)
    return cols[0] if len(cols) == 1 else jnp.concatenate(cols, axis=1)


def _rope_tables(seq):
    lane = jnp.arange(LANES) % HEAD_DIM
    inv = ROPE_THETA ** (-(2 * (lane % ROT_HALF)).astype(F32) / ROT_DIM)
    ang = jnp.arange(seq, dtype=F32)[:, None] * inv[None, :]
    cos, sin = jnp.cos(ang), jnp.sin(ang)
    c = jnp.where(lane < ROT_DIM, cos, 1.0)
    s1 = jnp.where(jnp.logical_and(lane >= ROT_HALF, lane < ROT_DIM), sin, 0.0)
    s2 = jnp.where(lane < ROT_HALF, -sin, 0.0)
    return jnp.stack([c, s1, s2])


def _table_specs(tm):
    return [pl.BlockSpec((None, tm, LANES), lambda b, i, j=j: (j, i, 0)) for j in range(3)]


def _sigmoid(x):
    return 0.5 * jnp.tanh(0.5 * x) + 0.5


SEG_PAD = SUBLANES // 2


def _to_segments(x, pad_ref):
    tl, w = x.shape
    seg = tl // SUBLANES
    pitch = seg + SEG_PAD
    for c in range(w // LANES):
        for s in range(SUBLANES):
            pad_ref[c, s * pitch:s * pitch + seg, :] = x[s * seg:(s + 1) * seg,
                                                         c * LANES:(c + 1) * LANES]
    groups = [jnp.concatenate([pad_ref[c, pl.ds(j, SUBLANES, stride=pitch), :]
                               for c in range(w // LANES)], axis=1) for j in range(seg)]
    return jnp.concatenate(groups, axis=0)


def _shift_segments(xs, k, prev):
    last = xs[xs.shape[0] - k * SUBLANES:, :]
    sub = lax.broadcasted_iota(jnp.int32, last.shape, 0) % SUBLANES
    head = jnp.where(sub == 0, _roll_groups(prev, k), _roll_groups(last, k))
    return jnp.concatenate([head, xs[:xs.shape[0] - k * SUBLANES, :]], axis=0)


def _roll_groups(x, k):
    return jnp.concatenate([pltpu.roll(x[g * SUBLANES:(g + 1) * SUBLANES, :], 1, 0)
                            for g in range(k)], axis=0)


def _lru_proj_kernel(h_ref, g_ref, w_ref, cw_ref, cb_ref, wa_ref, ba_ref, wx_ref, bx_ref,
                     lam_ref, rc_ref, rs1_ref, rs2_ref,
                     rec_ref, q_ref, k_ref, v_ref,
                     hn_ref, tail_ref, hprev_ref, xpad_ref, hpad_ref, loc_ref, prod_ref, *, tl):
    seg = tl // SUBLANES
    pitch = seg + SEG_PAD

    @pl.when(pl.program_id(1) == 0)
    def _():
        tail_ref[...] = jnp.zeros_like(tail_ref)
        hprev_ref[...] = jnp.zeros_like(hprev_ref)

    hn_ref[...] = _rms(h_ref[...], g_ref[...]).astype(BF16)
    hn = hn_ref[...]
    c, s1, s2 = rc_ref[...], rs1_ref[...], rs2_ref[...]
    attn_out = (q_ref, k_ref, v_ref)

    group_cols = lambda hd: slice(hd * LRU_HEAD_DIM, (hd + 1) * LRU_HEAD_DIM)
    half = B_WIDTH // 2

    def attn_piece(hd, j):
        o = 2 * LRU_WIDTH + hd * B_WIDTH + j * half
        t = _dot(hn, w_ref[:, o:o + half])
        if hd < 2:
            t = _rope(t, c, s1, s2)
        if hd == 0:
            t = t * (HEAD_DIM ** -0.5)
        attn_out[hd][:, j * half:(j + 1) * half] = t.astype(BF16)

    x_next = _dot(hn, w_ref[:, group_cols(0)])
    for hd in range(LRU_HEADS):
        cols = group_cols(hd)
        x = x_next
        if hd + 1 < LRU_HEADS:
            x_next = _dot(hn, w_ref[:, group_cols(hd + 1)])

        x = _to_segments(x, xpad_ref)
        tail = tail_ref[:, cols]
        taps = CONV_WIDTH - 1
        xc = cb_ref[:, cols] + x * cw_ref[taps:taps + 1, cols]
        for k in range(1, CONV_WIDTH):
            prev = tail[(taps - k) * SUBLANES:, :]
            xc = xc + _shift_segments(x, k, prev) * cw_ref[taps - k:taps - k + 1, cols]
        tail_ref[:, cols] = x[tl - taps * SUBLANES:tl, :]

        y = _dot(hn, w_ref[:, LRU_WIDTH + hd * LRU_HEAD_DIM:LRU_WIDTH + (hd + 1) * LRU_HEAD_DIM])

        xcb = xc.astype(BF16)
        r = _sigmoid(_dot(xcb, wa_ref[hd]) + ba_ref[:, cols])
        ig = _sigmoid(_dot(xcb, wx_ref[hd]) + bx_ref[:, cols])
        nlam = -lam_ref[:, cols]
        softplus = jnp.maximum(nlam, 0.0) + jnp.log1p(jnp.exp(-jnp.abs(nlam)))
        neg_log_a = r * (LRU_C * softplus)
        a = jnp.exp2(r * ((-LRU_C * math.log2(math.e)) * softplus))
        u = jnp.sqrt(jnp.tanh(neg_log_a) * (1.0 + a * a)) * (ig * xc)

        if hd < len(attn_out):
            attn_piece(hd, 0)

        hloc = jnp.zeros((SUBLANES, LRU_HEAD_DIM), F32)
        prod = jnp.ones((SUBLANES, LRU_HEAD_DIM), F32)
        for j in range(seg):
            rows = slice(j * SUBLANES, (j + 1) * SUBLANES)
            hloc = a[rows, :] * hloc + u[rows, :]
            prod = a[rows, :] * prod
            loc_ref[rows, :] = hloc
            prod_ref[rows, :] = prod

        state = hprev_ref[0:1, cols]
        entering = []
        for s in range(SUBLANES):
            entering.append(state)
            state = prod[s:s + 1, :] * state + hloc[s:s + 1, :]
        hprev_ref[:, cols] = jnp.broadcast_to(state, (SUBLANES, LRU_HEAD_DIM))
        entering = jnp.concatenate(entering, axis=0)

        for j in range(seg):
            rows = slice(j * SUBLANES, (j + 1) * SUBLANES)
            hrow = loc_ref[rows, :] + prod_ref[rows, :] * entering
            for cidx in range(LRU_HEAD_DIM // LANES):
                hpad_ref[cidx, pl.ds(j, SUBLANES, stride=pitch), :] = hrow[:, cidx * LANES:
                                                                           (cidx + 1) * LANES]

        if hd < len(attn_out):
            attn_piece(hd, 1)

        gelu = 0.5 * y * (1.0 + jnp.tanh(math.sqrt(2.0 / math.pi) * (y + 0.044715 * (y * y * y))))
        for s in range(SUBLANES):
            rows = slice(s * seg, (s + 1) * seg)
            hs = jnp.concatenate([hpad_ref[cidx, s * pitch:s * pitch + seg, :]
                                  for cidx in range(LRU_HEAD_DIM // LANES)], axis=1)
            rec_ref[rows, cols] = (hs * gelu[rows, :]).astype(BF16)


def _lru_proj(h, g, w_in, conv_w, conv_b, wa, ba, wx, bx, lam, tables, tl=LRU_TILE):
    bsz, seq, d = h.shape
    tok = lambda w: pl.BlockSpec((None, tl, w), lambda b, i: (b, i, 0))
    out_shape = (jax.ShapeDtypeStruct((bsz, seq, LRU_WIDTH), BF16),) + tuple(
        jax.ShapeDtypeStruct((bsz, seq, B_WIDTH), BF16) for _ in range(3))
    params = (g, w_in, conv_w, conv_b, wa, ba, wx, bx, lam)
    return pl.pallas_call(
        functools.partial(_lru_proj_kernel, tl=tl),
        grid=(bsz, seq // tl),
        in_specs=[tok(d)] + [_layer_spec(a.shape, 0) for a in params] + _table_specs(tl),
        out_specs=(tok(LRU_WIDTH), tok(B_WIDTH), tok(B_WIDTH), tok(B_WIDTH)),
        out_shape=out_shape,
        scratch_shapes=[pltpu.VMEM((tl, d), BF16),
                        pltpu.VMEM(((CONV_WIDTH - 1) * SUBLANES, LRU_WIDTH), F32),
                        pltpu.VMEM((SUBLANES, LRU_WIDTH), F32),
                        pltpu.VMEM((LRU_HEAD_DIM // LANES, tl + SUBLANES * SEG_PAD, LANES), F32),
                        pltpu.VMEM((LRU_HEAD_DIM // LANES, tl + SUBLANES * SEG_PAD, LANES), F32),
                        pltpu.VMEM((tl, LRU_HEAD_DIM), F32), pltpu.VMEM((tl, LRU_HEAD_DIM), F32)],
        compiler_params=_params("arbitrary", "arbitrary"),
        name="lru_proj",
    )(h, *params, tables, tables, tables)


def _sink_column(first):
    return (2 * BLOCK - 1) * (1 - first)


def _fill_band_bias(bias_ref, max_dist, reps, sink_of_rep=None):
    rows = lax.broadcasted_iota(jnp.int32, (BLOCK, 2 * BLOCK), 0)
    cols = lax.broadcasted_iota(jnp.int32, (BLOCK, 2 * BLOCK), 1)
    for first, off in ((0, 0), (1, BLOCK)):
        dist = off + rows - cols
        valid = jnp.logical_and(dist >= 0, dist <= max_dist)
        tile = jnp.where(valid, 0.0, NEG).astype(F32)
        for rep in range(reps):
            rep_tile = tile
            if sink_of_rep is not None:
                assert max_dist < BLOCK
                rep_tile = jnp.where(cols == _sink_column(first), sink_of_rep(rep), tile)
            bias_ref[first, rep * BLOCK:(rep + 1) * BLOCK, :] = rep_tile


def _band_attend(qb, kb, vb, bias, want_lse):
    r = qb.shape[0]
    lane_q = lax.broadcasted_iota(jnp.int32, qb.shape, 1)
    zero = jnp.zeros((), qb.dtype)
    qz = jnp.concatenate([jnp.where(lane_q < HEAD_DIM, qb, zero),
                          jnp.where(lane_q >= HEAD_DIM, qb, zero)], axis=0)
    s = _dot_nt(qz, kb) + bias
    m = jnp.max(s, axis=-1, keepdims=True)
    e = jnp.exp(s - m).astype(vb.dtype)
    ob = _dot(e, jnp.concatenate([vb, jnp.ones_like(vb)], axis=1))
    first_head = lax.broadcasted_iota(jnp.int32, (r, LANES), 1) < HEAD_DIM
    num = jnp.where(first_head, ob[:r, :LANES], ob[r:, :LANES])
    den = jnp.where(first_head, ob[:r, LANES:], ob[r:, LANES:])
    o = num * (1.0 / den)
    if not want_lse:
        return o, None
    return o, jnp.where(first_head, m[:r], m[r:]) + jnp.log(den)


def _band_window(n, nblk):
    first = jnp.minimum(lax.rem(n, nblk), 1)
    q0 = pl.multiple_of(n * BLOCK, BLOCK)
    ks = pl.multiple_of(q0 - first * BLOCK, BLOCK)
    return q0, ks, first


def _cast_specs(ws, layer, steps, step_of):
    in_specs, out_specs, out_shapes = [], [], []
    for w in ws:
        assert w.shape[1] % (steps * 2 * SUBLANES) == 0
        blk = (None, w.shape[1] // steps, w.shape[2])
        in_specs.append(pl.BlockSpec(blk, lambda *ids: (layer, step_of(*ids), 0)))
        out_specs.append(pl.BlockSpec(blk, lambda *ids: (0, step_of(*ids), 0)))
        out_shapes.append(jax.ShapeDtypeStruct((1,) + tuple(w.shape[1:]), BF16))
    return in_specs, out_specs, out_shapes


def _cast_slabs(src_refs, dst_refs):
    for src, dst in zip(src_refs, dst_refs):
        dst[...] = src[...].astype(dst.dtype)


def _dilated_kernel(*refs, seq, max_dist, n_cast):
    q_ref, k_ref, v_ref = refs[:3]
    out_ref = refs[3 + n_cast]
    stage_ref, p4_ref, perm_ref, o_ref, l_ref, bias_ref = refs[4 + 2 * n_cast:]
    _cast_slabs(refs[3:3 + n_cast], refs[4 + n_cast:4 + 2 * n_cast])
    _fill_band_bias(bias_ref, max_dist, 2)

    l4, l16 = seq // 4, seq // 16
    for a, x_ref in enumerate((q_ref, k_ref, v_ref)):
        stage_ref[...] = x_ref[...].astype(F32)
        for r in range(4):
            blk = stage_ref[pl.ds(r, l4, stride=4), :]
            p4_ref[r * l4:(r + 1) * l4, :] = blk
            perm_ref[0, a, r * l4:(r + 1) * l4, :] = blk.astype(BF16)
        for r in range(4):
            for rb in range(4):
                blk = p4_ref[pl.ds(r * l4 + rb, l16, stride=4), :]
                res = r + 4 * rb
                perm_ref[1, a, res * l16:(res + 1) * l16, :] = blk.astype(BF16)

    branches = ((1, (q_ref, k_ref, v_ref)),
                (4, tuple(perm_ref.at[0, a] for a in range(3))),
                (16, tuple(perm_ref.at[1, a] for a in range(3))))
    for bi, (d, (qr, kr, vr)) in enumerate(branches):
        nblk = seq // d // BLOCK

        def body(n, carry, d=d, bi=bi, qr=qr, kr=kr, vr=vr, nblk=nblk):
            q0, ks, first = _band_window(n, nblk)
            o, lse = _band_attend(qr[pl.ds(q0, BLOCK), :], kr[pl.ds(ks, 2 * BLOCK), :],
                                  vr[pl.ds(ks, 2 * BLOCK), :], bias_ref[first], True)
            if d == 1:
                rows = pl.ds(q0, BLOCK)
            else:
                rows = pl.ds(lax.rem(n, nblk) * (BLOCK * d) + n // nblk, BLOCK, stride=d)
            o_ref.at[bi][rows, :] = o
            l_ref.at[bi][rows, :] = lse
            return carry

        lax.fori_loop(0, seq // BLOCK, body, 0, unroll=32)

    chunk = 4 * BLOCK

    def merge(c, carry):
        rows = pl.ds(pl.multiple_of(c * chunk, chunk), chunk)
        ls = [l_ref[bi, rows, :] for bi in range(3)]
        m = jnp.maximum(jnp.maximum(ls[0], ls[1]), ls[2])
        ws = [jnp.exp(l - m) for l in ls]
        acc = ws[0] * o_ref[0, rows, :] + ws[1] * o_ref[1, rows, :] + ws[2] * o_ref[2, rows, :]
        out_ref[rows, :] = (acc * (1.0 / (ws[0] + ws[1] + ws[2]))).astype(out_ref.dtype)
        return carry

    lax.fori_loop(0, seq // chunk, merge, 0)


def _dilated_attention(q, k, v, cast, cast_layer):
    assert tuple(d for _, d in DILATED_PATTERN) == (1, 4, 16)
    assert len({w // d for w, d in DILATED_PATTERN}) == 1
    max_dist = DILATED_PATTERN[0][0] // DILATED_PATTERN[0][1]
    bsz, seq, w = q.shape
    cols = w // LANES
    spec = pl.BlockSpec((None, seq, LANES), lambda b, c: (b, 0, c))
    cast_in, cast_out, cast_shapes = _cast_specs(cast, cast_layer, bsz * cols,
                                                 lambda b, c: b * cols + c)
    return pl.pallas_call(
        functools.partial(_dilated_kernel, seq=seq, max_dist=max_dist, n_cast=len(cast)),
        grid=(bsz, cols),
        in_specs=[spec, spec, spec] + cast_in,
        out_specs=[spec] + cast_out,
        out_shape=[jax.ShapeDtypeStruct((bsz, seq, w), BF16)] + cast_shapes,
        scratch_shapes=[pltpu.VMEM((seq, LANES), F32), pltpu.VMEM((seq, LANES), F32),
                        pltpu.VMEM((2, 3, seq, LANES), BF16),
                        pltpu.VMEM((3, seq, LANES), F32), pltpu.VMEM((3, seq, LANES), F32),
                        pltpu.VMEM((2, 2 * BLOCK, 2 * BLOCK), F32)],
        compiler_params=_params("arbitrary", "arbitrary"),
        name="dilated",
    )(q, k, v, *cast)


def _swa_kernel(*refs, nblk, max_dist, pairs, n_cast):
    sink_ref, q_ref, k_ref, v_ref = refs[:4]
    o_ref = refs[4 + n_cast]
    bias_ref = refs[5 + 2 * n_cast]
    _cast_slabs(refs[4:4 + n_cast], refs[5 + n_cast:5 + 2 * n_cast])
    head0 = pl.program_id(1) * (2 * pairs)
    _fill_band_bias(bias_ref, max_dist, 2 * pairs,
                    lambda rep: sink_ref[head0 + 2 * (rep % pairs) + rep // pairs])
    key_row = lax.broadcasted_iota(jnp.int32, (2 * BLOCK, LANES), 0)

    def body(i, carry):
        q0, ks, first = _band_window(i, nblk)
        qrow = q_ref[pl.ds(q0, BLOCK), :]
        qb = jnp.concatenate([qrow[:, p * LANES:(p + 1) * LANES] for p in range(pairs)], axis=0)
        unseen = key_row == _sink_column(first)
        zero = jnp.zeros((), k_ref.dtype)
        kb = jnp.where(unseen, zero, k_ref[pl.ds(ks, 2 * BLOCK), :])
        vb = jnp.where(unseen, zero, v_ref[pl.ds(ks, 2 * BLOCK), :])
        o, _ = _band_attend(qb, kb, vb, bias_ref[first], False)
        o_ref[pl.ds(q0, BLOCK), :] = jnp.concatenate(
            [o[p * BLOCK:(p + 1) * BLOCK, :] for p in range(pairs)], axis=1).astype(o_ref.dtype)
        return carry

    lax.fori_loop(0, nblk, body, 0, unroll=16)


def _swa_attention(q, kdup, vdup, sinks, cast, cast_layer):
    bsz, seq, w = q.shape
    pairs = w // LANES // C_KV_HEADS
    qspec = pl.BlockSpec((None, seq, pairs * LANES), lambda b, g: (b, 0, g))
    kvspec = pl.BlockSpec((None, seq, LANES), lambda b, g: (b, 0, g))
    cast_in, cast_out, cast_shapes = _cast_specs(cast, cast_layer, bsz * C_KV_HEADS,
                                                 lambda b, g: b * C_KV_HEADS + g)
    return pl.pallas_call(
        functools.partial(_swa_kernel, nblk=seq // BLOCK, max_dist=C_WINDOW - 1, pairs=pairs,
                          n_cast=len(cast)),
        grid=(bsz, C_KV_HEADS),
        in_specs=[pl.BlockSpec(memory_space=pltpu.SMEM), qspec, kvspec, kvspec] + cast_in,
        out_specs=[qspec] + cast_out,
        out_shape=[jax.ShapeDtypeStruct((bsz, seq, w), BF16)] + cast_shapes,
        scratch_shapes=[pltpu.VMEM((2, 2 * pairs * BLOCK, 2 * BLOCK), F32)],
        compiler_params=_params("arbitrary", "arbitrary"),
        name="swa_sink",
    )(sinks, q, kdup, vdup, *cast)


def _dup_heads(a):
    swapped = pltpu.roll(a, HEAD_DIM, 1)
    low = lax.broadcasted_iota(jnp.int32, a.shape, 1) < HEAD_DIM
    return jnp.concatenate([jnp.where(low, a, swapped), jnp.where(low, swapped, a)], axis=1)


def _swa_qkv(h, g_ref, w_ref, b_ref, rc_ref, rs1_ref, rs2_ref, q_ref, k_ref, v_ref):
    hn = _rms(h, g_ref[...]).astype(BF16)
    c, s1, s2 = rc_ref[...], rs1_ref[...], rs2_ref[...]
    ko, vo = C_Q_WIDTH, C_Q_WIDTH + C_KV_WIDTH
    q = _dot(hn, w_ref[:, 0:ko]) + b_ref[:, 0:ko]
    q_ref[...] = (_rope(q, c, s1, s2) * (HEAD_DIM ** -0.5)).astype(BF16)
    kv = _dot(hn, w_ref[:, ko:vo + C_KV_WIDTH]) + b_ref[:, ko:vo + C_KV_WIDTH]
    k_ref[...] = _dup_heads(_rope(kv[:, :C_KV_WIDTH], c, s1, s2)).astype(BF16)
    v_ref[...] = _dup_heads(kv[:, C_KV_WIDTH:]).astype(BF16)


def _mem_kv_kernel(mem_ref, g_ref, w_ref, k_ref, v_ref):
    mn = _rms(mem_ref[...], g_ref[...]).astype(BF16)
    k_ref[...] = _dot(mn, w_ref[:, 0:XA_WIDTH]).astype(BF16)
    v_ref[...] = _dot(mn, w_ref[:, XA_WIDTH:2 * XA_WIDTH]).astype(BF16)


def _mem_kv(mem, g, wkv):
    bsz, m, d = mem.shape
    layers = wkv.shape[0]
    spec = pl.BlockSpec((None, bsz * m, XA_WIDTH), lambda l: (l, 0, 0))
    per_layer = lambda a: pl.BlockSpec((None,) + a.shape[1:], lambda l: (l, 0, 0))
    return pl.pallas_call(
        _mem_kv_kernel,
        grid=(layers,),
        in_specs=[pl.BlockSpec((bsz * m, d), lambda l: (0, 0)), per_layer(g), per_layer(wkv)],
        out_specs=(spec, spec),
        out_shape=(jax.ShapeDtypeStruct((layers, bsz * m, XA_WIDTH), BF16),) * 2,
        compiler_params=_params("arbitrary"),
        name="mem_kv",
    )(mem.reshape(bsz * m, d), g, wkv)


def _layer_tail_kernel(*refs, n_act, d_ff, last):
    h_ref, act_refs = refs[0], refs[1:1 + n_act]
    (wout_ref, bout_ref, xg_ref, wq_ref, mk_ref, mv_ref, wo_ref,
     fg_ref, wgu_ref, wd_ref) = refs[1 + n_act:11 + n_act]
    epilogue = refs[11 + n_act:]

    h = h_ref[...] + bout_ref[...]
    row = 0
    for a_ref in act_refs:
        width = a_ref.shape[-1]
        h = h + _dot(a_ref[...], wout_ref[row:row + width, :])
        row += width

    q = _dot(_rms(h, xg_ref[...]).astype(BF16), wq_ref[...]).astype(BF16)
    scale = XA_HEAD_DIM ** -0.5
    heads = []
    for hd in range(XA_HEADS):
        sl = slice(hd * XA_HEAD_DIM, (hd + 1) * XA_HEAD_DIM)
        s = _dot_nt(q[:, sl], mk_ref[:, sl]) * scale
        e = jnp.exp(s - jnp.max(s, axis=-1, keepdims=True))
        den = jnp.sum(e, axis=-1, keepdims=True)
        heads.append((_dot(e.astype(BF16), mv_ref[:, sl]) * (1.0 / den)).astype(BF16))
    h = h + _dot(jnp.concatenate(heads, axis=1), wo_ref[...])

    hn = _rms(h, fg_ref[...]).astype(BF16)
    for c0, width in _ff_chunks(d_ff):
        gate = _dot(hn, wgu_ref[:, c0:c0 + width])
        up = _dot(hn, wgu_ref[:, d_ff + c0:d_ff + c0 + width])
        act = (gate * jax.nn.sigmoid(gate) * up).astype(BF16)
        h = h + _dot(act, wd_ref[c0:c0 + width, :])
    if last:
        ng_ref, out_ref = epilogue
        out_ref[...] = _rms(h, ng_ref[...])
    else:
        out_ref = epilogue[6]
        out_ref[...] = h
        _swa_qkv(h, *epilogue[:6], *epilogue[7:])


def _layer_tail(h, acts, w_out, b_out, mk, mv, p, ffn, layer, nxt=None, tm=TOK_TILE):
    bsz, seq, d = h.shape
    wgu, wd = ffn
    d_ff = wd.shape[1]
    assert sum(a.shape[-1] for a in acts) == w_out.shape[1]
    tok = lambda width: pl.BlockSpec((None, tm, width), lambda b, i: (b, i, 0))
    kv = pl.BlockSpec((None, N_MEM, XA_WIDTH), lambda b, i: (layer, b, 0))
    res = jax.ShapeDtypeStruct((bsz, seq, d), F32)
    if nxt is None:
        extra, extra_specs = (p["ng"],), [_const_spec(p["ng"].shape)]
        out_specs, out_shape = tok(d), res
    else:
        g2, w2, b2, tables = nxt
        extra = (g2, w2, b2, tables, tables, tables)
        extra_specs = [_layer_spec(a.shape, 0) for a in (g2, w2, b2)] + _table_specs(tm)
        widths = (C_Q_WIDTH, 2 * C_KV_WIDTH, 2 * C_KV_WIDTH)
        out_specs = (tok(d),) + tuple(tok(w) for w in widths)
        out_shape = (res,) + tuple(jax.ShapeDtypeStruct((bsz, seq, w), BF16) for w in widths)
    return pl.pallas_call(
        functools.partial(_layer_tail_kernel, n_act=len(acts), d_ff=d_ff, last=nxt is None),
        grid=(bsz, seq // tm),
        in_specs=[tok(d)] + [tok(a.shape[-1]) for a in acts] + [
            _layer_spec(w_out.shape, 0), _layer_spec(b_out.shape, 0),
            _layer_spec(p["xg"].shape, layer), _layer_spec(p["wq"].shape, layer), kv, kv,
            _layer_spec(p["wo"].shape, layer), _layer_spec(p["fg"].shape, layer),
            _layer_spec(wgu.shape, 0), _layer_spec(wd.shape, 0)] + extra_specs,
        out_specs=out_specs,
        out_shape=out_shape,
        compiler_params=_params("arbitrary", "arbitrary"),
        name="layer_tail",
    )(h, *acts, w_out, b_out, p["xg"], p["wq"], mk, mv, p["wo"], p["fg"], wgu, wd,
      *extra)


def kernel(x, mem, mix_norm, ab_w_in, lru_conv_w, lru_conv_b, lru_wa, lru_ba, lru_wx, lru_bx, lru_lambda, ab_w_out, c_w_qkv, c_b_qkv, c_sinks, c_w_out, c_b_out, xa_norm, xa_mem_norm, xa_wq, xa_wkv, xa_wo, ffn_norm, ffn_w_gate_up, ffn_w_down, final_norm):
    bsz, seq, d = x.shape
    tables = _rope_tables(seq)
    rows = lambda a: a.reshape(a.shape[0], 1, -1)
    bf = lambda a: a.astype(BF16)
    mix_g = rows(mix_norm)
    p = dict(xg=rows(xa_norm), wq=bf(xa_wq), wo=bf(xa_wo), fg=rows(ffn_norm),
             ng=final_norm.reshape(1, d))
    ffn_f32 = (ffn_w_gate_up, ffn_w_down)
    mk, mv = _mem_kv(mem, rows(xa_mem_norm), bf(xa_wkv))

    rec, q, k, v = _lru_proj(x, mix_g, bf(ab_w_in), lru_conv_w, rows(lru_conv_b), bf(lru_wa),
                             rows(lru_ba), bf(lru_wx), rows(lru_bx), rows(lru_lambda), tables)
    att, *ffn = _dilated_attention(q, k, v, ffn_f32, 0)
    h, q, k, v = _layer_tail(x, (rec, att), bf(ab_w_out), jnp.zeros((1, 1, d), F32), mk, mv, p,
                             ffn, layer=0,
                             nxt=(mix_g[1:2], bf(c_w_qkv), rows(c_b_qkv), tables))

    att, *ffn = _swa_attention(q, k, v, c_sinks[0], ffn_f32, 1)
    return _layer_tail(h, (att,), bf(c_w_out), rows(c_b_out), mk, mv, p, ffn, layer=1,
                       tm=LAST_TAIL_TILE)
```
